```python
import jax
import jax.numpy as jnp
from jax import lax
import numpy as np

D_MODEL = 1024
BATCH = 2
SEQ = 16384
DEPTH = 1
DEC_BATCH = 128
DEC_SEQ = 1
PAST_LEN = 8192
PAGE_SIZE = 128

N_HEADS = 16
N_KV_HEADS = 4
GROUP = N_HEADS // N_KV_HEADS
HEAD_DIM = D_MODEL // N_HEADS
ROT_DIM = HEAD_DIM // 4
ROPE_THETA = 500000.0
CMP_LEN = 32
CMP_STRIDE = 16
CMP_HIDDEN = 256
SEL_BLOCK = 64
N_SELECT = 16
WINDOW = 512
Q_BLOCK = 128
D_CONV = D_MODEL // 2
CONV_WIDTH = 31
D_FF = 2816
EPS = 1e-6
NEG = -1e30
BIG = 1e6
KV_W = N_KV_HEADS * HEAD_DIM
COL_SIZES = (2 * D_CONV, N_HEADS * HEAD_DIM, KV_W, KV_W, KV_W, KV_W, KV_W, KV_W, 3 * N_HEADS, 2 * D_MODEL)
D_IN = sum(COL_SIZES)

kernel_name = 'conformer_conv_nsa_macaron_step'


def rmsnorm(x, g):
    xf = x.astype(jnp.float32)
    return (xf * lax.rsqrt(jnp.mean(xf * xf, -1, keepdims=True) + EPS)).astype(x.dtype) * g


def layernorm(x, g, b):
    xf = x.astype(jnp.float32)
    mu = jnp.mean(xf, -1, keepdims=True)
    var = jnp.mean(jnp.square(xf - mu), -1, keepdims=True)
    return ((xf - mu) * lax.rsqrt(var + EPS)).astype(x.dtype) * g + b


def swiglu(x, wg, wu, wd):
    return (jax.nn.silu(x @ wg) * (x @ wu)) @ wd


def masked_softmax(s, mask, axis=-1):
    s = jnp.where(mask, s.astype(jnp.float32), NEG)
    m = jnp.max(s, axis=axis, keepdims=True)
    p = jnp.where(mask, jnp.exp(s - m), 0.0)
    return p / jnp.maximum(jnp.sum(p, axis=axis, keepdims=True), 1e-30)


def rope(x, pos):
    half = ROT_DIM // 2
    inv = ROPE_THETA ** (-jnp.arange(half, dtype=jnp.float32) / half)
    ang = pos.astype(jnp.float32)[:, None] * inv[None, :]
    cos = jnp.cos(ang)[None, :, None, :].astype(x.dtype)
    sin = jnp.sin(ang)[None, :, None, :].astype(x.dtype)
    x1 = x[..., :half]
    x2 = x[..., half:ROT_DIM]
    return jnp.concatenate([x1 * cos - x2 * sin, x2 * cos + x1 * sin, x[..., ROT_DIM:]], axis=-1)


def split_columns(z):
    out, start = [], 0
    for n in COL_SIZES:
        out.append(z[..., start:start + n])
        start += n
    return out


def layer_front(x, n1, g1, u1, d1, nm, w_in):
    h = x + 0.5 * swiglu(rmsnorm(x, n1), g1, u1, d1)
    return h, split_columns(rmsnorm(h, nm) @ w_in)


def layer_back(h, y_conv, y_attn, mg, w_attn_out, w_out, n2, g2, u2, d2):
    ga, gb = jnp.split(mg, 2, axis=-1)
    m = jax.nn.sigmoid(ga) * y_conv + jax.nn.sigmoid(gb) * (y_attn @ w_attn_out)
    h = h + m @ w_out
    return h + 0.5 * swiglu(rmsnorm(h, n2), g2, u2, d2)


def attn_inputs(q, kc, vc, ks, vs, kw, vw, ng, pos):
    B, T = q.shape[:2]
    heads = lambda z, n: z.reshape(B, T, n, HEAD_DIM)
    q = rope(heads(q, N_HEADS), pos).reshape(B, T, N_KV_HEADS, GROUP, HEAD_DIM)
    ks = rope(heads(ks, N_KV_HEADS), pos)
    kw = rope(heads(kw, N_KV_HEADS), pos)
    gates = jax.nn.sigmoid(ng).reshape(B, T, N_KV_HEADS, GROUP, 3)
    return (q, heads(kc, N_KV_HEADS), heads(vc, N_KV_HEADS), ks, heads(vs, N_KV_HEADS),
            kw, heads(vw, N_KV_HEADS), gates)


def conv_branch(glu_in, ctx, w, b, ln_g, ln_b, w_out):
    a, gate = jnp.split(glu_in, 2, axis=-1)
    v = a * jax.nn.sigmoid(gate)
    if ctx is None:
        ctx = jnp.zeros((v.shape[0], CONV_WIDTH - 1, D_CONV), v.dtype)
    ext = jnp.concatenate([ctx.astype(v.dtype), v], axis=1)
    y = lax.conv_general_dilated(ext, w[:, None, :].astype(ext.dtype), (1,), 'VALID',
                                 dimension_numbers=('NWC', 'WIO', 'NWC'),
                                 feature_group_count=D_CONV) + b
    y = jax.nn.silu(layernorm(y, ln_g, ln_b)) @ w_out
    return y, ext[:, -(CONV_WIDTH - 1):]


def compress(k, pe, w1, b1, w2):
    B, L = k.shape[:2]
    n_sub = L // CMP_STRIDE
    sub = k[:, :n_sub * CMP_STRIDE].reshape(B, n_sub, CMP_STRIDE, N_KV_HEADS, HEAD_DIM)
    first = jnp.einsum('bnlhd,lde->bnhe', sub, w1[:CMP_STRIDE])
    second = jnp.einsum('bnlhd,lde->bnhe', sub, w1[CMP_STRIDE:])
    c = jnp.einsum('ld,lde->e', pe, w1) + b1
    hid = jax.nn.silu(first[:, :-1] + second[:, 1:] + c)
    cmp_end = jnp.arange(n_sub - 1) * CMP_STRIDE + CMP_LEN - 1
    return hid @ w2, cmp_end


def nsa_core(q, t, kc, vc, cmp_end, n_blk, gather_sel, kw, vw, win_pos, gates):
    B, Tq = q.shape[:2]
    scale = HEAD_DIM ** -0.5
    s = jnp.einsum('bqhgd,bchd->bhgqc', q, kc).astype(jnp.float32) * scale
    p_c = masked_softmax(s, cmp_end[None, :] <= t[:, None])
    o_c = jnp.einsum('bhgqc,bchd->bqhgd', p_c.astype(vc.dtype), vc)
    n_cmp = kc.shape[1]
    ci = jnp.arange(n_cmp) * CMP_STRIDE
    bs = jnp.arange(n_blk) * SEL_BLOCK
    overlap = ((ci[:, None] < bs[None, :] + SEL_BLOCK) & (ci[:, None] + CMP_LEN > bs[None, :])).astype(jnp.float32)
    imp = jnp.einsum('bhgqc,cn->bhqn', p_c, overlap)
    blk = jnp.arange(n_blk)[None, :]
    cur = (t // SEL_BLOCK)[:, None]
    forced = (blk == 0) | (blk == cur) | (blk == cur - 1)
    valid = blk * SEL_BLOCK <= t[:, None]
    score = jnp.where(valid, imp + jnp.where(forced, BIG, 0.0), -BIG)
    _, idx = lax.top_k(score, min(N_SELECT, n_blk))
    ks, vs = gather_sel(idx)
    sel_pos = idx[..., None] * SEL_BLOCK + jnp.arange(SEL_BLOCK)
    mask_s = (sel_pos <= t[None, None, :, None, None])[:, :, None]
    s = jnp.einsum('bqhgd,bhqnkd->bhgqnk', q, ks).astype(jnp.float32) * scale
    p_s = masked_softmax(s, mask_s, axis=(-2, -1))
    o_s = jnp.einsum('bhgqnk,bhqnkd->bqhgd', p_s.astype(vs.dtype), vs)
    mask_w = (win_pos[None, :] <= t[:, None]) & (win_pos[None, :] > t[:, None] - WINDOW) & (win_pos[None, :] >= 0)
    s = jnp.einsum('bqhgd,bwhd->bhgqw', q, kw).astype(jnp.float32) * scale
    p_w = masked_softmax(s, mask_w)
    o_w = jnp.einsum('bhgqw,bwhd->bqhgd', p_w.astype(vw.dtype), vw)
    o = gates[..., 0, None] * o_c + gates[..., 1, None] * o_s + gates[..., 2, None] * o_w
    return o.reshape(B, Tq, N_HEADS * HEAD_DIM)


def nsa_prompt(q, kc_raw, vc_raw, ks, vs, kw, vw, gates, cmpk, cmpv):
    B, S = q.shape[:2]
    kc, cmp_end = compress(kc_raw, *cmpk)
    vc, _ = compress(vc_raw, *cmpv)
    n_blk = S // SEL_BLOCK
    ks_b = ks.reshape(B, n_blk, SEL_BLOCK, N_KV_HEADS, HEAD_DIM)
    vs_b = vs.reshape(B, n_blk, SEL_BLOCK, N_KV_HEADS, HEAD_DIM)
    bi = jnp.arange(B)[:, None, None, None]
    hi = jnp.arange(N_KV_HEADS)[None, :, None, None]

    def gather_sel(idx):
        return ks_b[bi, idx, :, hi, :], vs_b[bi, idx, :, hi, :]

    kw_pad = jnp.pad(kw, ((0, 0), (WINDOW, 0), (0, 0), (0, 0)))
    vw_pad = jnp.pad(vw, ((0, 0), (WINDOW, 0), (0, 0), (0, 0)))

    def one_block(i):
        s0 = i * Q_BLOCK
        qb = lax.dynamic_slice_in_dim(q, s0, Q_BLOCK, axis=1)
        gb = lax.dynamic_slice_in_dim(gates, s0, Q_BLOCK, axis=1)
        kwb = lax.dynamic_slice_in_dim(kw_pad, s0, WINDOW + Q_BLOCK, axis=1)
        vwb = lax.dynamic_slice_in_dim(vw_pad, s0, WINDOW + Q_BLOCK, axis=1)
        q_pos = s0 + jnp.arange(Q_BLOCK)
        win_pos = s0 - WINDOW + jnp.arange(WINDOW + Q_BLOCK)
        return nsa_core(qb, q_pos, kc, vc, cmp_end, n_blk, gather_sel, kwb, vwb, win_pos, gb)

    out = lax.map(one_block, jnp.arange(S // Q_BLOCK))
    return jnp.moveaxis(out, 0, 1).reshape(B, S, N_HEADS * HEAD_DIM)


def nsa_sample(q, kc_new, vc_new, ks_new, vs_new, kw_new, vw_new, gates,
               cache_ck, cache_cv, cache_sk, cache_sv, win_k, win_v, page_table, cmpk, cmpv):
    DB, T = q.shape[:2]
    n_pages = page_table.shape[1]
    past = n_pages * PAGE_SIZE
    kc_all = jnp.concatenate([cache_ck[page_table].reshape(DB, past, N_KV_HEADS, HEAD_DIM), kc_new], axis=1)
    vc_all = jnp.concatenate([cache_cv[page_table].reshape(DB, past, N_KV_HEADS, HEAD_DIM), vc_new], axis=1)
    kc, cmp_end = compress(kc_all, *cmpk)
    vc, _ = compress(vc_all, *cmpv)
    L = past + T
    n_blk = -(-L // SEL_BLOCK)
    n_past_blk = past // SEL_BLOCK
    n_tail = n_blk - n_past_blk
    pad = n_tail * SEL_BLOCK - T
    tail_k = jnp.pad(ks_new, ((0, 0), (0, pad), (0, 0), (0, 0))).reshape(DB, n_tail, SEL_BLOCK, N_KV_HEADS, HEAD_DIM)
    tail_v = jnp.pad(vs_new, ((0, 0), (0, pad), (0, 0), (0, 0))).reshape(DB, n_tail, SEL_BLOCK, N_KV_HEADS, HEAD_DIM)
    sub = PAGE_SIZE // SEL_BLOCK
    pool_k = cache_sk.reshape(-1, SEL_BLOCK, N_KV_HEADS, HEAD_DIM)
    pool_v = cache_sv.reshape(-1, SEL_BLOCK, N_KV_HEADS, HEAD_DIM)
    bi = jnp.arange(DB)[:, None, None, None]
    hi = jnp.arange(N_KV_HEADS)[None, :, None, None]

    def gather_sel(idx):
        in_past = (idx < n_past_blk)[..., None, None]
        pb = jnp.minimum(idx, n_past_blk - 1)
        phys = page_table[bi, pb // sub] * sub + pb % sub
        tb = jnp.clip(idx - n_past_blk, 0, n_tail - 1)
        k = jnp.where(in_past, pool_k[phys, :, hi, :], tail_k[bi, tb, :, hi, :])
        v = jnp.where(in_past, pool_v[phys, :, hi, :], tail_v[bi, tb, :, hi, :])
        return k, v

    wl = win_k.shape[1]
    kw = jnp.concatenate([win_k.astype(kw_new.dtype), kw_new], axis=1)
    vw = jnp.concatenate([win_v.astype(vw_new.dtype), vw_new], axis=1)
    win_pos = past - wl + jnp.arange(wl + T)
    q_pos = past + jnp.arange(T)
    out = nsa_core(q, q_pos, kc, vc, cmp_end, n_blk, gather_sel, kw, vw, win_pos, gates)
    return out, kw[:, -wl:], vw[:, -wl:]


def setup_inputs(seed: int = 0) -> dict:
    key = jax.random.key(seed)
    keys = iter(jax.random.split(key, 64))
    f32 = jnp.float32

    def dense(shape, fan_in):
        return jax.random.normal(next(keys), shape, f32) * fan_in ** -0.5

    def gain(shape):
        return 1.0 + 0.05 * jax.random.normal(next(keys), shape, f32)

    def small(shape, scale=0.02):
        return scale * jax.random.normal(next(keys), shape, f32)

    n_pages = PAST_LEN // PAGE_SIZE
    n_used = DEC_BATCH * n_pages
    n_phys = n_used + n_used // 4
    win_len = min(WINDOW, PAST_LEN)
    pool = (DEPTH, n_phys, PAGE_SIZE, N_KV_HEADS, HEAD_DIM)
    Ly = DEPTH
    return {
        'x_prompt': jax.random.normal(next(keys), (BATCH, SEQ, D_MODEL), f32),
        'x_sample': jax.random.normal(next(keys), (DEC_BATCH, DEC_SEQ, D_MODEL), f32),
        'cache_cmp_k': jax.random.normal(next(keys), pool, f32),
        'cache_cmp_v': jax.random.normal(next(keys), pool, f32),
        'cache_sel_k': jax.random.normal(next(keys), pool, f32),
        'cache_sel_v': jax.random.normal(next(keys), pool, f32),
        'state_win_k': jax.random.normal(next(keys), (DEPTH, DEC_BATCH, win_len, N_KV_HEADS, HEAD_DIM), f32),
        'state_win_v': jax.random.normal(next(keys), (DEPTH, DEC_BATCH, win_len, N_KV_HEADS, HEAD_DIM), f32),
        'state_conv': jax.random.normal(next(keys), (DEPTH, DEC_BATCH, CONV_WIDTH - 1, D_CONV), f32),
        'page_table': jax.random.permutation(next(keys), n_phys)[:n_used].reshape(DEC_BATCH, n_pages).astype(jnp.int32),
        'norm_ffn1': gain((Ly, D_MODEL)),
        'ffn1_gate': dense((Ly, D_MODEL, D_FF), D_MODEL),
        'ffn1_up': dense((Ly, D_MODEL, D_FF), D_MODEL),
        'ffn1_down': dense((Ly, D_FF, D_MODEL), D_FF),
        'norm_mix': gain((Ly, D_MODEL)),
        'w_in': dense((Ly, D_MODEL, D_IN), D_MODEL),
        'conv_w': dense((Ly, CONV_WIDTH, D_CONV), CONV_WIDTH),
        'conv_b': small((Ly, D_CONV)),
        'conv_ln_g': gain((Ly, D_CONV)),
        'conv_ln_b': small((Ly, D_CONV)),
        'w_conv_out': dense((Ly, D_CONV, D_MODEL), D_CONV),
        'cmp_pe_k': small((Ly, CMP_LEN, HEAD_DIM), 0.5),
        'cmp_w1_k': dense((Ly, CMP_LEN, HEAD_DIM, CMP_HIDDEN), CMP_LEN * HEAD_DIM),
        'cmp_b1_k': small((Ly, CMP_HIDDEN)),
        'cmp_w2_k': dense((Ly, CMP_HIDDEN, HEAD_DIM), CMP_HIDDEN),
        'cmp_pe_v': small((Ly, CMP_LEN, HEAD_DIM), 0.5),
        'cmp_w1_v': dense((Ly, CMP_LEN, HEAD_DIM, CMP_HIDDEN), CMP_LEN * HEAD_DIM),
        'cmp_b1_v': small((Ly, CMP_HIDDEN)),
        'cmp_w2_v': dense((Ly, CMP_HIDDEN, HEAD_DIM), CMP_HIDDEN),
        'w_attn_out': dense((Ly, N_HEADS * HEAD_DIM, D_MODEL), N_HEADS * HEAD_DIM),
        'w_out': dense((Ly, D_MODEL, D_MODEL), D_MODEL),
        'norm_ffn2': gain((Ly, D_MODEL)),
        'ffn2_gate': dense((Ly, D_MODEL, D_FF), D_MODEL),
        'ffn2_up': dense((Ly, D_MODEL, D_FF), D_MODEL),
        'ffn2_down': dense((Ly, D_FF, D_MODEL), D_FF),
        'norm_final': gain((D_MODEL,)),
    }


def reference(x_prompt, x_sample, cache_cmp_k, cache_cmp_v, cache_sel_k, cache_sel_v,
              state_win_k, state_win_v, state_conv, page_table,
              norm_ffn1, ffn1_gate, ffn1_up, ffn1_down, norm_mix, w_in,
              conv_w, conv_b, conv_ln_g, conv_ln_b, w_conv_out,
              cmp_pe_k, cmp_w1_k, cmp_b1_k, cmp_w2_k, cmp_pe_v, cmp_w1_v, cmp_b1_v, cmp_w2_v,
              w_attn_out, w_out, norm_ffn2, ffn2_gate, ffn2_up, ffn2_down, norm_final):
    S = x_prompt.shape[1]
    T = x_sample.shape[1]
    past = page_table.shape[1] * PAGE_SIZE
    pos_p = jnp.arange(S, dtype=jnp.int32)
    pos_s = past + jnp.arange(T, dtype=jnp.int32)
    hp, hs = x_prompt, x_sample
    st = [[] for _ in range(14)]
    for l in range(DEPTH):
        ffn1 = (norm_ffn1[l], ffn1_gate[l], ffn1_up[l], ffn1_down[l])
        ffn2 = (norm_ffn2[l], ffn2_gate[l], ffn2_up[l], ffn2_down[l])
        convp = (conv_w[l], conv_b[l], conv_ln_g[l], conv_ln_b[l], w_conv_out[l])
        cmpk = (cmp_pe_k[l], cmp_w1_k[l], cmp_b1_k[l], cmp_w2_k[l])
        cmpv = (cmp_pe_v[l], cmp_w1_v[l], cmp_b1_v[l], cmp_w2_v[l])
        h_res, (glu_in, q, kc, vc, ks, vs, kw, vw, ng, mg) = layer_front(hp, *ffn1, norm_mix[l], w_in[l])
        q, kc, vc, ks, vs, kw, vw, gates = attn_inputs(q, kc, vc, ks, vs, kw, vw, ng, pos_p)
        y_conv, conv_new = conv_branch(glu_in, None, *convp)
        y_attn = nsa_prompt(q, kc, vc, ks, vs, kw, vw, gates, cmpk, cmpv)
        hp = layer_back(h_res, y_conv, y_attn, mg, w_attn_out[l], w_out[l], *ffn2)
        wl = min(WINDOW, S)
        prompt_state = (kc, vc, ks, vs, kw[:, S - wl:], vw[:, S - wl:], conv_new)
        h_res, (glu_in, q, kc, vc, ks, vs, kw, vw, ng, mg) = layer_front(hs, *ffn1, norm_mix[l], w_in[l])
        q, kc, vc, ks, vs, kw, vw, gates = attn_inputs(q, kc, vc, ks, vs, kw, vw, ng, pos_s)
        y_conv, conv_new_s = conv_branch(glu_in, state_conv[l], *convp)
        y_attn, wk_new, wv_new = nsa_sample(q, kc, vc, ks, vs, kw, vw, gates,
                                            cache_cmp_k[l], cache_cmp_v[l], cache_sel_k[l], cache_sel_v[l],
                                            state_win_k[l], state_win_v[l], page_table, cmpk, cmpv)
        hs = layer_back(h_res, y_conv, y_attn, mg, w_attn_out[l], w_out[l], *ffn2)
        for j, a in enumerate(prompt_state + (kc, vc, ks, vs, wk_new, wv_new, conv_new_s)):
            st[j].append(a)
    y_prompt = rmsnorm(hp, norm_final)
    y_sample = rmsnorm(hs, norm_final)
    (p_cmp_k, p_cmp_v, p_sel_k, p_sel_v, p_win_k, p_win_v, p_conv,
     s_cmp_k, s_cmp_v, s_sel_k, s_sel_v, s_win_k, s_win_v, s_conv) = [jnp.stack(a) for a in st]
    return (y_prompt, y_sample, p_cmp_k, p_cmp_v, p_sel_k, p_sel_v, p_win_k, p_win_v, p_conv,
            s_cmp_k, s_cmp_v, s_sel_k, s_sel_v, s_win_k, s_win_v, s_conv)
```

```python
import functools

import numpy as np
import jax
import jax.numpy as jnp
from jax import lax
from jax.experimental import pallas as pl
from jax.experimental.pallas import tpu as pltpu

N_HEADS = 16
N_KV_HEADS = 4
GROUP = N_HEADS // N_KV_HEADS
HEAD_DIM = 64
ROT_DIM = HEAD_DIM // 4
ROPE_THETA = 500000.0
CMP_LEN = 32
CMP_STRIDE = 16
SEL_BLOCK = 64
N_SELECT = 16
WINDOW = 512
Q_TILE = 128
K_TILE = 512
PAGE_SIZE = 128
CONV_WIDTH = 31
CONV_HALO = 32
EPS = 1e-6
NEG = -1e30
BIG = 1e6
REMOVED = -3e38
KV_W = N_KV_HEADS * HEAD_DIM
LANES = 128
SUBLANES = 8
SUB_PER_PAGE = PAGE_SIZE // CMP_STRIDE
CMP_ROW = CMP_STRIDE * KV_W

MXU_DTYPE = jnp.bfloat16
VMEM_LIMIT = 56 * 1024 * 1024
F32 = jnp.float32


def _mm(a, b):
    return jnp.dot(a.astype(MXU_DTYPE), b.astype(MXU_DTYPE), preferred_element_type=F32)


def _mm_nt(a, b):
    return lax.dot_general(a.astype(MXU_DTYPE), b.astype(MXU_DTYPE), (((1,), (1,)), ((), ())),
                           preferred_element_type=F32)


def _mm_split(a, b):
    if MXU_DTYPE == F32:
        return _mm(a, b)
    hi = a.astype(MXU_DTYPE)
    lo = a - hi.astype(F32)
    return _mm(hi, b) + _mm(lo, b)


def _sigmoid(x):
    return 1.0 / (1.0 + jnp.exp(-x))


def _const_spec(shape):
    nd = len(shape)
    return pl.BlockSpec(shape, lambda *_: (0,) * nd, pipeline_mode=pl.Buffered(1))


def _params(*sem):
    return pltpu.CompilerParams(dimension_semantics=sem, vmem_limit_bytes=VMEM_LIMIT)


def _ff_chunk(f):
    best = LANES
    for k in range(1, f // LANES + 1):
        c = k * LANES
        if f % c == 0 and c <= 1408:
            best = c
    return best


def _ffn_kernel(*refs, fc, final):
    if final:
        x_ref, g_ref, wg_ref, wu_ref, wd_ref, gf_ref, o_ref = refs
    else:
        x_ref, g_ref, wg_ref, wu_ref, wd_ref, o_ref = refs
    x = x_ref[...]
    xn = x * lax.rsqrt(jnp.mean(x * x, -1, keepdims=True) + EPS) * g_ref[...]
    xb = xn.astype(MXU_DTYPE)
    acc = None
    for c in range(wg_ref.shape[1] // fc):
        sl = slice(c * fc, (c + 1) * fc)
        gate = _mm(xb, wg_ref[:, sl])
        up = _mm(xb, wu_ref[:, sl])
        d = _mm(gate * _sigmoid(gate) * up, wd_ref[sl, :])
        acc = d if acc is None else acc + d
    h = x + 0.5 * acc
    if final:
        h = h * lax.rsqrt(jnp.mean(h * h, -1, keepdims=True) + EPS) * gf_ref[...]
    o_ref[...] = h


def _ffn(x, g, wg, wu, wd, gfin=None, tm=512):
    r, d = x.shape
    f = wg.shape[1]
    tm = min(tm, r)
    row = pl.BlockSpec((tm, d), lambda i: (i, 0))
    in_specs = [row, _const_spec((1, d)), _const_spec((d, f)), _const_spec((d, f)), _const_spec((f, d))]
    args = [x, g.reshape(1, d), wg, wu, wd]
    if gfin is not None:
        in_specs.append(_const_spec((1, d)))
        args.append(gfin.reshape(1, d))
    return pl.pallas_call(
        functools.partial(_ffn_kernel, fc=_ff_chunk(f), final=gfin is not None),
        grid=(r // tm,), in_specs=in_specs, out_specs=row,
        out_shape=jax.ShapeDtypeStruct((r, d), F32),
        compiler_params=_params("parallel"), name="ffn")(*args)


def _rope_tables(pos):
    half = ROT_DIM // 2
    inv = ROPE_THETA ** (-jnp.arange(half, dtype=F32) / half)
    ang = pos.astype(F32)[:, None] * inv[None, :]
    cos, sin = jnp.cos(ang), jnp.sin(ang)
    t = pos.shape[0]
    rest = HEAD_DIM - ROT_DIM
    c = jnp.concatenate([cos, cos, jnp.ones((t, rest), F32)], 1)
    s1 = jnp.concatenate([jnp.zeros((t, half), F32), sin, jnp.zeros((t, rest), F32)], 1)
    s2 = jnp.concatenate([-sin, jnp.zeros((t, half + rest), F32)], 1)
    rep = LANES // HEAD_DIM
    return jnp.tile(c, (1, rep)), jnp.tile(s1, (1, rep)), jnp.tile(s2, (1, rep))


def _rope_store(z, c, s1, s2, o_ref):
    for j in range(z.shape[1] // LANES):
        x = z[:, j * LANES:(j + 1) * LANES]
        o_ref[:, j * LANES:(j + 1) * LANES] = (
            x * c + pltpu.roll(x, ROT_DIM // 2, 1) * s1 + pltpu.roll(x, LANES - ROT_DIM // 2, 1) * s2)


def _inproj_kernel(h_ref, g_ref, w_ref, c_ref, s1_ref, s2_ref,
                   v_ref, q_ref, kc_ref, vc_ref, ks_ref, vs_ref, kw_ref, vw_ref, ng_ref, mg_ref, *, offs):
    x = h_ref[...]
    xb = (x * lax.rsqrt(jnp.mean(x * x, -1, keepdims=True) + EPS) * g_ref[...]).astype(MXU_DTYPE)
    c, s1, s2 = c_ref[...], s1_ref[...], s2_ref[...]

    def seg(i):
        return _mm(xb, w_ref[:, offs[i]:offs[i + 1]])

    glu = seg(0)
    dc = glu.shape[1] // 2
    v_ref[...] = glu[:, :dc] * _sigmoid(glu[:, dc:])
    _rope_store(seg(1), c, s1, s2, q_ref)
    kc_ref[...] = seg(2)
    vc_ref[...] = seg(3)
    _rope_store(seg(4), c, s1, s2, ks_ref)
    vs_ref[...] = seg(5)
    _rope_store(seg(6), c, s1, s2, kw_ref)
    vw_ref[...] = seg(7)
    ng_ref[...] = _sigmoid(seg(8))
    mg_ref[...] = _sigmoid(seg(9))


def _inproj(h, g, w_pad, offs, tables, rows_per_seq, tm=512):
    r, d = h.shape
    tm = min(tm, r, rows_per_seq)
    nper = rows_per_seq // tm
    row = lambda w: pl.BlockSpec((tm, w), lambda i: (i, 0))
    tab = pl.BlockSpec((tm, LANES), lambda i: (i % nper, 0))
    widths = [(offs[1] - offs[0]) // 2] + [offs[i + 1] - offs[i] for i in range(1, 10)]
    return pl.pallas_call(
        functools.partial(_inproj_kernel, offs=offs),
        grid=(r // tm,),
        in_specs=[row(d), _const_spec((1, d)), _const_spec(w_pad.shape), tab, tab, tab],
        out_specs=[row(w) for w in widths],
        out_shape=[jax.ShapeDtypeStruct((r, w), F32) for w in widths],
        compiler_params=_params("parallel"), name="inproj")(h, g.reshape(1, d), w_pad, *tables)


def _conv_kernel(main_ref, tail_ref, w_ref, b_ref, lg_ref, lb_ref, wo_ref, o_ref, ext_ref):
    ts = main_ref.shape[0]
    ext_ref[0:ts, :] = main_ref[...]
    ext_ref[ts:ts + CONV_HALO, :] = tail_ref[...]
    lead = CONV_HALO - (CONV_WIDTH - 1)
    acc = None
    for k in range(CONV_WIDTH):
        term = ext_ref[pl.ds(lead + k, ts), :] * w_ref[k:k + 1, :]
        acc = term if acc is None else acc + term
    y = acc + b_ref[...]
    mu = jnp.mean(y, -1, keepdims=True)
    yc = y - mu
    var = jnp.mean(yc * yc, -1, keepdims=True)
    yn = yc * lax.rsqrt(var + EPS) * lg_ref[...] + lb_ref[...]
    o_ref[...] = _mm(yn * _sigmoid(yn), wo_ref[...])


def _conv(ext, w, b, lg, lb, wo, ts):
    bsz, lp, c = ext.shape
    l = lp - CONV_HALO
    d = wo.shape[1]
    step = ts // CONV_HALO
    return pl.pallas_call(
        _conv_kernel,
        grid=(bsz, l // ts),
        in_specs=[pl.BlockSpec((None, ts, c), lambda bi, i: (bi, i, 0)),
                  pl.BlockSpec((None, CONV_HALO, c), lambda bi, i: (bi, (i + 1) * step, 0)),
                  _const_spec((CONV_HALO, c)), _const_spec((1, c)), _const_spec((1, c)), _const_spec((1, c)),
                  _const_spec((c, d))],
        out_specs=pl.BlockSpec((None, ts, d), lambda bi, i: (bi, i, 0)),
        out_shape=jax.ShapeDtypeStruct((bsz, l, d), F32),
        scratch_shapes=[pltpu.VMEM((ts + CONV_HALO, c), F32)],
        compiler_params=_params("parallel", "parallel"), name="conv")(
            ext, ext, w, b.reshape(1, c), lg.reshape(1, c), lb.reshape(1, c), wo)


def _compress_kernel(pt_ref, x_hbm, w1_ref, pe_ref, w1f_ref, b1_ref, w2_ref, o_ref, buf, second, sem, *, pc, n_pages):
    s = pl.program_id(0)
    ci = pl.program_id(1)
    m = pc * SUB_PER_PAGE

    def page_copy(p):
        pg = pt_ref[s * n_pages + jnp.minimum(ci * pc + p, n_pages - 1)]
        return pltpu.make_async_copy(x_hbm.at[pg], buf.at[pl.ds(p * SUB_PER_PAGE, SUB_PER_PAGE)], sem)

    for p in range(pc + 1):
        page_copy(p).start()
    for p in range(pc + 1):
        page_copy(p).wait()
    fs = _mm(buf[...], w1_ref[...])
    nh = fs.shape[1] // 2
    second[...] = fs[:, nh:]
    c = _mm(pe_ref[...], w1f_ref[...])[0:1, :] + b1_ref[...]
    ct = jnp.concatenate([c] * N_KV_HEADS, axis=1)
    hid = fs[0:m, :nh] + second[pl.ds(1, m), :] + ct
    o_ref[...] = _mm(hid * _sigmoid(hid), w2_ref[...])


def _compress(x_pages, page_table, w1_big, pe8, w1_flat, b1, w2_big, pc):
    n_seq, n_pages = page_table.shape
    m = pc * SUB_PER_PAGE
    hidden = b1.shape[0]
    grid_spec = pltpu.PrefetchScalarGridSpec(
        num_scalar_prefetch=1,
        grid=(n_seq, n_pages // pc),
        in_specs=[pl.BlockSpec(memory_space=pl.ANY),
                  _const_spec(w1_big.shape), _const_spec(pe8.shape), _const_spec(w1_flat.shape),
                  _const_spec((1, hidden)), _const_spec(w2_big.shape)],
        out_specs=pl.BlockSpec((None, m, KV_W), lambda s, ci, pt: (s, ci, 0)),
        scratch_shapes=[pltpu.VMEM((m + SUB_PER_PAGE, CMP_ROW), F32),
                        pltpu.VMEM((m + SUB_PER_PAGE, w1_big.shape[1] // 2), F32),
                        pltpu.SemaphoreType.DMA(())])
    return pl.pallas_call(
        functools.partial(_compress_kernel, pc=pc, n_pages=n_pages),
        grid_spec=grid_spec,
        out_shape=jax.ShapeDtypeStruct((n_seq, n_pages * SUB_PER_PAGE, KV_W), F32),
        compiler_params=_params("arbitrary", "arbitrary"), name="compress")(
            page_table.reshape(-1), x_pages, w1_big, pe8, w1_flat, b1.reshape(1, hidden), w2_big)


def _compress_weights(pe, w1, b1, w2):
    hidden = w1.shape[-1]
    eye = jnp.eye(N_KV_HEADS, dtype=F32)
    halves = w1.reshape(2, CMP_STRIDE, HEAD_DIM, hidden)
    w1_big = jnp.einsum('sldk,hg->lhdsgk', halves, eye).reshape(CMP_ROW, 2 * N_KV_HEADS * hidden)
    w2_big = jnp.einsum('kd,hg->hkgd', w2, eye).reshape(N_KV_HEADS * hidden, KV_W)
    pe8 = jnp.zeros((SUBLANES, CMP_LEN * HEAD_DIM), F32).at[0].set(pe.reshape(-1))
    return (w1_big.astype(MXU_DTYPE), pe8, w1.reshape(CMP_LEN * HEAD_DIM, hidden).astype(MXU_DTYPE), b1,
            w2_big.astype(MXU_DTYPE))


def _overlap(n_cmp_pad, n_blk_pad):
    ci = np.arange(n_cmp_pad)[:, None] * CMP_STRIDE
    bs = np.arange(n_blk_pad)[None, :] * SEL_BLOCK
    return ((ci < bs + SEL_BLOCK) & (ci + CMP_LEN > bs)).astype(np.float32)


def _softmax_cols(s, mask):
    s = jnp.where(mask, s, NEG)
    m = jnp.max(s, axis=0, keepdims=True)
    p = jnp.where(mask, jnp.exp(s - m), 0.0)
    return p / jnp.maximum(jnp.sum(p, axis=0, keepdims=True), 1e-30)


def _attn_kernel(q_ref, g_ref, kc_ref, vct_ref, ov_ref, ks_ref, vst_ref, kw_ref, vwt_ref, o_ref, bias_ref,
                 *, n_cmp, n_blk):
    s0 = pl.program_id(2) * Q_TILE
    qt = q_ref[...]
    cols = GROUP * Q_TILE
    tq = s0 + (lax.broadcasted_iota(jnp.int32, (1, cols), 1) & (Q_TILE - 1))
    tq1 = s0 + lax.broadcasted_iota(jnp.int32, (1, Q_TILE), 1)

    ncp = kc_ref.shape[0]
    ci = lax.broadcasted_iota(jnp.int32, (ncp, 1), 0)
    p_c = _softmax_cols(_mm(kc_ref[...], qt), (ci * CMP_STRIDE + (CMP_LEN - 1) <= tq) & (ci < n_cmp))
    o_c = _mm(vct_ref[...], p_c)
    p_sum = p_c[:, 0:Q_TILE]
    for g in range(1, GROUP):
        p_sum = p_sum + p_c[:, g * Q_TILE:(g + 1) * Q_TILE]
    imp = _mm_split_t(ov_ref[...], p_sum)

    blk = lax.broadcasted_iota(jnp.int32, (n_blk, 1), 0)
    blk_f = blk.astype(F32)
    cur = lax.shift_right_logical(tq1, SEL_BLOCK.bit_length() - 1)
    forced = (blk == 0) | (blk == cur) | (blk == cur - 1)
    score = jnp.where(blk * SEL_BLOCK <= tq1, imp + jnp.where(forced, BIG, 0.0), -BIG)
    bias = jnp.full((n_blk, Q_TILE), NEG, F32)
    for _ in range(min(N_SELECT, n_blk)):
        top = jnp.max(score, axis=0, keepdims=True)
        first = jnp.min(jnp.where(score == top, blk_f, float(n_blk)), axis=0, keepdims=True)
        hit = blk_f == first
        bias = jnp.where(hit, 0.0, bias)
        score = jnp.where(hit, REMOVED, score)
    bias_ref[...] = bias

    bpt = K_TILE // SEL_BLOCK
    kpos0 = lax.broadcasted_iota(jnp.int32, (K_TILE, 1), 0)

    def sel_step(j, carry):
        m, l, acc = carry
        k0 = pl.multiple_of(j * K_TILE, K_TILE)
        s = _mm(ks_ref[pl.ds(k0, K_TILE), :], qt)
        b8 = bias_ref[pl.ds(pl.multiple_of(j * bpt, bpt), bpt), :]
        slab = jnp.concatenate([jnp.broadcast_to(b8[i:i + 1, :], (SEL_BLOCK, Q_TILE)) for i in range(bpt)], axis=0)
        s = jnp.where(kpos0 + k0 <= tq, s + jnp.concatenate([slab] * GROUP, axis=1), NEG)
        m_new = jnp.maximum(m, jnp.max(s, axis=0, keepdims=True))
        alpha = jnp.exp(m - m_new)
        p = jnp.exp(s - m_new)
        l = alpha * l + jnp.sum(p, axis=0, keepdims=True)
        acc = alpha * acc + _mm(vst_ref[:, pl.ds(k0, K_TILE)], p)
        return m_new, l, acc

    n_kt = (s0 + Q_TILE - 1) // K_TILE + 1
    init = (jnp.full((1, cols), NEG, F32), jnp.zeros((1, cols), F32), jnp.zeros((HEAD_DIM, cols), F32))
    _, l_s, acc_s = lax.fori_loop(0, n_kt, sel_step, init)
    o_s = acc_s / l_s

    wk = WINDOW + Q_TILE
    w0 = pl.multiple_of(jnp.maximum(s0 - WINDOW, 0), Q_TILE)
    wpos = w0 + lax.broadcasted_iota(jnp.int32, (wk, 1), 0)
    p_w = _softmax_cols(_mm(kw_ref[pl.ds(w0, wk), :], qt), (wpos <= tq) & (wpos > tq - WINDOW))
    o_w = _mm(vwt_ref[:, pl.ds(w0, wk)], p_w)

    g = g_ref[...]
    o_ref[...] = g[0:1, :] * o_c + g[1:2, :] * o_s + g[2:3, :] * o_w


def _mm_split_t(a, b):
    if MXU_DTYPE == F32:
        return _mm(a, b)
    hi = b.astype(MXU_DTYPE)
    lo = b - hi.astype(F32)
    return _mm(a, hi) + _mm(a, lo)


def _attn_prompt(qt, gt, kc, vct, ks, vst, kw, vwt, n_cmp):
    b, h, nt, hd, cols = qt.shape
    s = ks.shape[2]
    ncp = kc.shape[2]
    n_blk = s // SEL_BLOCK
    ov = jnp.asarray(_overlap(ncp, n_blk).T, MXU_DTYPE)
    tile = lambda r: pl.BlockSpec((None, None, None, r, cols), lambda bi, hi, t: (bi, hi, t, 0, 0))
    rows = lambda n: pl.BlockSpec((None, None, n, hd), lambda bi, hi, t: (bi, hi, 0, 0))
    colsp = lambda n: pl.BlockSpec((None, None, hd, n), lambda bi, hi, t: (bi, hi, 0, 0))
    return pl.pallas_call(
        functools.partial(_attn_kernel, n_cmp=n_cmp, n_blk=n_blk),
        grid=(b, h, nt),
        in_specs=[tile(hd), tile(3), rows(ncp), colsp(ncp), _const_spec(ov.shape),
                  rows(s), colsp(s), rows(s), colsp(s)],
        out_specs=tile(hd),
        out_shape=jax.ShapeDtypeStruct((b, h, nt, hd, cols), F32),
        scratch_shapes=[pltpu.VMEM((n_blk, Q_TILE), F32)],
        compiler_params=_params("parallel", "parallel", "arbitrary"), name="attn_prompt")(
            qt, gt, kc, vct, ov, ks, vst, kw, vwt)


def _sample_cmp_kernel(q_ref, kc_ref, vc_ref, ov_ref, oc_ref, idx_ref, *, n_cmp, n_blk, t):
    ncp = kc_ref.shape[1]
    nbp = ov_ref.shape[1]
    ci = lax.broadcasted_iota(jnp.int32, (1, ncp), 1)
    mask = (ci * CMP_STRIDE + (CMP_LEN - 1) <= t) & (ci < n_cmp)
    sums = []
    for h in range(N_KV_HEADS):
        s = jnp.where(mask, _mm_nt(q_ref[h], kc_ref[h]), NEG)
        m = jnp.max(s, axis=1, keepdims=True)
        p = jnp.where(mask, jnp.exp(s - m), 0.0)
        p = p / jnp.maximum(jnp.sum(p, axis=1, keepdims=True), 1e-30)
        oc_ref[h] = _mm(p, vc_ref[h])
        sums.append(jnp.sum(p, axis=0, keepdims=True))
    imp = _mm_split(jnp.concatenate(sums, axis=0), ov_ref[...])
    blk = lax.broadcasted_iota(jnp.int32, (1, nbp), 1)
    cur = t // SEL_BLOCK
    forced = (blk == 0) | (blk == cur) | (blk == cur - 1)
    score = jnp.where(blk * SEL_BLOCK <= t, imp + jnp.where(forced, BIG, 0.0), -BIG)
    score = jnp.where(blk < n_blk, score, REMOVED)
    blk_f = blk.astype(F32)
    lane = lax.broadcasted_iota(jnp.int32, (1, LANES), 1)
    out = jnp.zeros((N_KV_HEADS, LANES), F32)
    for r in range(min(N_SELECT, n_blk)):
        top = jnp.max(score, axis=1, keepdims=True)
        first = jnp.min(jnp.where(score == top, blk_f, float(nbp)), axis=1, keepdims=True)
        out = jnp.where(lane == r, first, out)
        score = jnp.where(blk_f == first, REMOVED, score)
    idx_ref[...] = out.astype(jnp.int32)


def _sample_cmp(q4, kc, vc, n_cmp, n_blk, t):
    db, _, _, hd = q4.shape
    ncp = kc.shape[2]
    nbp = -(-n_blk // LANES) * LANES
    ov = jnp.asarray(_overlap(ncp, nbp), MXU_DTYPE)
    per = lambda n: pl.BlockSpec((None, N_KV_HEADS, n, hd), lambda b: (b, 0, 0, 0))
    return pl.pallas_call(
        functools.partial(_sample_cmp_kernel, n_cmp=n_cmp, n_blk=n_blk, t=t),
        grid=(db,),
        in_specs=[per(GROUP), per(ncp), per(ncp), _const_spec(ov.shape)],
        out_specs=[per(GROUP), pl.BlockSpec((None, N_KV_HEADS, LANES), lambda b: (b, 0, 0))],
        out_shape=[jax.ShapeDtypeStruct((db, N_KV_HEADS, GROUP, hd), F32),
                   jax.ShapeDtypeStruct((db, N_KV_HEADS, LANES), jnp.int32)],
        compiler_params=_params("parallel"), name="sample_cmp")(q4, kc, vc, ov)


def _sample_sel_kernel(idx_ref, pt_ref, qx_ref, ocx_ref, g_ref, ksn_ref, vsn_ref, wk_ref, wv_ref, kwn_ref, vwn_ref,
                       *rest, n_past_blk, t):
    kblk = rest[0:N_KV_HEADS]
    vblk = rest[N_KV_HEADS:2 * N_KV_HEADS]
    o_ref, m_ref, l_ref, acc_ref = rest[2 * N_KV_HEADS:]
    b = pl.program_id(0)
    n = pl.program_id(1)
    nsel = pl.num_programs(1)
    qx = qx_ref[...]

    @pl.when(n == 0)
    def _():
        m_ref[...] = jnp.full(m_ref.shape, NEG, F32)
        l_ref[...] = jnp.zeros(l_ref.shape, F32)
        acc_ref[...] = jnp.zeros(acc_ref.shape, F32)

    kp = lax.broadcasted_iota(jnp.int32, (1, SEL_BLOCK), 1)
    for h in range(N_KV_HEADS):
        rs = slice(h * GROUP, (h + 1) * GROUP)
        blk = idx_ref[(b * N_KV_HEADS + h) * nsel + n]
        ok = (blk < n_past_blk) & (blk * SEL_BLOCK + kp <= t)
        s = jnp.where(ok, _mm_nt(qx[rs], kblk[h][...]), NEG)
        m_old = m_ref[rs]
        m_new = jnp.maximum(m_old, jnp.max(s, axis=1, keepdims=True))
        alpha = jnp.exp(m_old - m_new)
        p = jnp.where(ok, jnp.exp(s - m_new), 0.0)
        l_ref[rs] = alpha * l_ref[rs] + jnp.sum(p, axis=1, keepdims=True)
        acc_ref[rs] = alpha * acc_ref[rs] + _mm(p, vblk[h][...])
        m_ref[rs] = m_new

    @pl.when(n == nsel - 1)
    def _():
        s_new = jnp.sum(qx * ksn_ref[...], axis=1, keepdims=True)
        m_old = m_ref[...]
        m_new = jnp.maximum(m_old, s_new)
        alpha = jnp.exp(m_old - m_new)
        p_new = jnp.exp(s_new - m_new)
        l = alpha * l_ref[...] + p_new
        o_s = (alpha * acc_ref[...] + p_new * vsn_ref[...]) / l
        wl = wk_ref.shape[0]
        wpos = t - wl + lax.broadcasted_iota(jnp.int32, (1, wl), 1)
        okw = (wpos > t - WINDOW) & (wpos >= 0)
        sw = jnp.where(okw, _mm_nt(qx, wk_ref[...]), NEG)
        sw_new = jnp.sum(qx * kwn_ref[...], axis=1, keepdims=True)
        mw = jnp.maximum(jnp.max(sw, axis=1, keepdims=True), sw_new)
        pw = jnp.where(okw, jnp.exp(sw - mw), 0.0)
        pw_new = jnp.exp(sw_new - mw)
        o_w = (_mm(pw, wv_ref[...]) + pw_new * vwn_ref[...]) / (jnp.sum(pw, axis=1, keepdims=True) + pw_new)
        g = g_ref[...]
        o_ref[...] = g[:, 0:1] * ocx_ref[...] + g[:, 1:2] * o_s + g[:, 2:3] * o_w


def _sample_sel(idx, page_table, qx, ocx, gates, ks_new, vs_new, win_k, win_v, kw_new, vw_new, pool_k, pool_v, t):
    db, rows, _ = qx.shape
    nsel = idx.shape[2]
    n_pages = page_table.shape[1]
    sub = PAGE_SIZE // SEL_BLOCK
    n_past_blk = n_pages * sub
    wl = win_k.shape[1]

    def blk_spec(h):
        def index(b, n, idx_ref, pt_ref):
            pb = jnp.minimum(idx_ref[(b * N_KV_HEADS + h) * nsel + n], n_past_blk - 1)
            return (pt_ref[b * n_pages + pb // sub] * sub + pb % sub, 0, 0)
        return pl.BlockSpec((None, SEL_BLOCK, KV_W), index)

    per = lambda r, w: pl.BlockSpec((None, r, w), lambda b, n, *_: (b, 0, 0))
    grid_spec = pltpu.PrefetchScalarGridSpec(
        num_scalar_prefetch=2,
        grid=(db, nsel),
        in_specs=[per(rows, KV_W), per(rows, KV_W), per(rows, LANES), per(1, KV_W), per(1, KV_W),
                  per(wl, KV_W), per(wl, KV_W), per(1, KV_W), per(1, KV_W)]
                 + [blk_spec(h) for h in range(N_KV_HEADS)] * 2,
        out_specs=per(rows, KV_W),
        scratch_shapes=[pltpu.VMEM((rows, 1), F32), pltpu.VMEM((rows, 1), F32), pltpu.VMEM((rows, KV_W), F32)])
    return pl.pallas_call(
        functools.partial(_sample_sel_kernel, n_past_blk=n_past_blk, t=t),
        grid_spec=grid_spec,
        out_shape=jax.ShapeDtypeStruct((db, rows, KV_W), F32),
        compiler_params=_params("parallel", "arbitrary"), name="sample_sel")(
            idx.reshape(-1), page_table.reshape(-1), qx, ocx, gates, ks_new, vs_new, win_k, win_v, kw_new, vw_new,
            *([pool_k] * N_KV_HEADS), *([pool_v] * N_KV_HEADS))


def _merge_kernel(h_ref, yc_ref, ya_ref, mg_ref, wa_ref, wo_ref, o_ref):
    d = h_ref.shape[1]
    m = mg_ref[:, :d] * yc_ref[...] + mg_ref[:, d:] * _mm(ya_ref[...], wa_ref[...])
    o_ref[...] = h_ref[...] + _mm(m, wo_ref[...])


def _merge(h, y_conv, y_attn, mg, wa, wo, tm=512):
    r, d = h.shape
    tm = min(tm, r)
    row = lambda w: pl.BlockSpec((tm, w), lambda i: (i, 0))
    return pl.pallas_call(
        _merge_kernel, grid=(r // tm,),
        in_specs=[row(d), row(d), row(y_attn.shape[1]), row(2 * d), _const_spec(wa.shape), _const_spec(wo.shape)],
        out_specs=row(d), out_shape=jax.ShapeDtypeStruct((r, d), F32),
        compiler_params=_params("parallel"), name="merge")(h, y_conv, y_attn, mg, wa, wo)


def _heads_major(x, b, l):
    return x.reshape(b, l, N_KV_HEADS, HEAD_DIM).transpose(0, 2, 1, 3).astype(MXU_DTYPE)


def _heads_major_t(x, b, l):
    return x.reshape(b, l, N_KV_HEADS, HEAD_DIM).transpose(0, 2, 3, 1).astype(MXU_DTYPE)


def kernel(x_prompt, x_sample, cache_cmp_k, cache_cmp_v, cache_sel_k, cache_sel_v, state_win_k, state_win_v, state_conv, page_table, norm_ffn1, ffn1_gate, ffn1_up, ffn1_down, norm_mix, w_in, conv_w, conv_b, conv_ln_g, conv_ln_b, w_conv_out, cmp_pe_k, cmp_w1_k, cmp_b1_k, cmp_w2_k, cmp_pe_v, cmp_w1_v, cmp_b1_v, cmp_w2_v, w_attn_out, w_out, norm_ffn2, ffn2_gate, ffn2_up, ffn2_down, norm_final):
    bsz, seq, d = x_prompt.shape
    db, dseq, _ = x_sample.shape
    depth = norm_ffn1.shape[0]
    assert depth == 1 and dseq == 1 and d == N_HEADS * HEAD_DIM
    assert seq % K_TILE == 0 and seq >= WINDOW + Q_TILE
    n_pages = page_table.shape[1]
    past = n_pages * PAGE_SIZE
    d_conv = d // 2
    scale = HEAD_DIM ** -0.5
    cast = lambda w: w.astype(MXU_DTYPE)

    col_sizes = (2 * d_conv, N_HEADS * HEAD_DIM) + (KV_W,) * 6 + (3 * N_HEADS, 2 * d)
    starts = np.concatenate([[0], np.cumsum(col_sizes)])
    offs, pieces, pos = [0], [], 0
    for i, n in enumerate(col_sizes):
        pad = -n % LANES
        pieces.append(w_in[0][:, starts[i]:starts[i + 1]])
        if pad:
            pieces.append(jnp.zeros((d, pad), F32))
        pos += n + pad
        offs.append(pos)
    w_in_pad = cast(jnp.concatenate(pieces, axis=1))
    offs = tuple(offs)
    ffn1 = (norm_ffn1[0], cast(ffn1_gate[0]), cast(ffn1_up[0]), cast(ffn1_down[0]))
    ffn2 = (norm_ffn2[0], cast(ffn2_gate[0]), cast(ffn2_up[0]), cast(ffn2_down[0]))
    wa, wo, wco = cast(w_attn_out[0]), cast(w_out[0]), cast(w_conv_out[0])
    conv_w_pad = jnp.pad(conv_w[0], ((0, CONV_HALO - CONV_WIDTH), (0, 0)))
    cmpk = _compress_weights(cmp_pe_k[0], cmp_w1_k[0], cmp_b1_k[0], cmp_w2_k[0])
    cmpv = _compress_weights(cmp_pe_v[0], cmp_w1_v[0], cmp_b1_v[0], cmp_w2_v[0])

    def front(x, pos, rows_per_seq):
        h = _ffn(x, *ffn1)
        return (h,) + tuple(_inproj(h, norm_mix[0], w_in_pad, offs, _rope_tables(pos), rows_per_seq))

    def back(h, y_conv, y_attn, mg):
        return _ffn(_merge(h, y_conv, y_attn, mg, wa, wo), *ffn2, gfin=norm_final)

    rp = bsz * seq
    h_p, v_p, q_p, kc_p, vc_p, ks_p, vs_p, kw_p, vw_p, ng_p, mg_p = front(
        x_prompt.reshape(rp, d), jnp.arange(seq, dtype=jnp.int32), seq)
    ext_p = jnp.pad(v_p.reshape(bsz, seq, d_conv), ((0, 0), (CONV_HALO, 0), (0, 0)))
    y_conv_p = _conv(ext_p, conv_w_pad, conv_b[0], conv_ln_g[0], conv_ln_b[0], wco, ts=512).reshape(rp, d)

    pc_p = min(32, seq // PAGE_SIZE)
    ident = jnp.arange(rp // PAGE_SIZE, dtype=jnp.int32).reshape(bsz, seq // PAGE_SIZE)
    n_cmp_p = seq // CMP_STRIDE - 1
    kcc = _compress(kc_p.reshape(-1, SUB_PER_PAGE, CMP_ROW), ident, *cmpk, pc=pc_p)
    vcc = _compress(vc_p.reshape(-1, SUB_PER_PAGE, CMP_ROW), ident, *cmpv, pc=pc_p)
    nt = seq // Q_TILE
    qt = (q_p * scale).reshape(bsz, nt, Q_TILE, N_KV_HEADS, GROUP, HEAD_DIM).transpose(0, 3, 1, 5, 4, 2)
    qt = qt.reshape(bsz, N_KV_HEADS, nt, HEAD_DIM, GROUP * Q_TILE).astype(MXU_DTYPE)
    gt = ng_p[:, :3 * N_HEADS].reshape(bsz, nt, Q_TILE, N_KV_HEADS, GROUP, 3).transpose(0, 3, 1, 5, 4, 2)
    gt = gt.reshape(bsz, N_KV_HEADS, nt, 3, GROUP * Q_TILE)
    ncp = seq // CMP_STRIDE
    ot = _attn_prompt(qt, gt, _heads_major(kcc, bsz, ncp), _heads_major_t(vcc, bsz, ncp),
                      _heads_major(ks_p, bsz, seq), _heads_major_t(vs_p, bsz, seq),
                      _heads_major(kw_p, bsz, seq), _heads_major_t(vw_p, bsz, seq), n_cmp_p)
    y_attn_p = ot.reshape(bsz, N_KV_HEADS, nt, HEAD_DIM, GROUP, Q_TILE).transpose(0, 2, 5, 1, 4, 3).reshape(rp, d)
    y_prompt = back(h_p, y_conv_p, y_attn_p, mg_p).reshape(bsz, seq, d)

    t = past
    h_s, v_s, q_s, kc_s, vc_s, ks_s, vs_s, kw_s, vw_s, ng_s, mg_s = front(
        x_sample.reshape(db, d), jnp.full((db,), t, jnp.int32), db)
    conv_rows = CONV_HALO
    ext_s = jnp.concatenate([state_conv[0], v_s[:, None, :]], axis=1)
    ext_s_pad = jnp.pad(ext_s, ((0, 0), (CONV_HALO - (CONV_WIDTH - 1), conv_rows - 1), (0, 0)))
    y_conv_s = _conv(ext_s_pad, conv_w_pad, conv_b[0], conv_ln_g[0], conv_ln_b[0], wco, ts=conv_rows)[:, 0]

    pc_s = min(32, n_pages)
    n_sub_s = (past + dseq) // CMP_STRIDE
    n_cmp_s = n_sub_s - 1
    assert n_sub_s == n_pages * SUB_PER_PAGE
    kcc_s = _compress(cache_cmp_k[0].reshape(-1, SUB_PER_PAGE, CMP_ROW), page_table, *cmpk, pc=pc_s)
    vcc_s = _compress(cache_cmp_v[0].reshape(-1, SUB_PER_PAGE, CMP_ROW), page_table, *cmpv, pc=pc_s)
    n_blk_s = -(-(past + dseq) // SEL_BLOCK)
    q4 = (q_s * scale).reshape(db, N_KV_HEADS, GROUP, HEAD_DIM)
    oc_s, idx_s = _sample_cmp(q4, _heads_major(kcc_s, db, n_sub_s), _heads_major(vcc_s, db, n_sub_s),
                              n_cmp_s, n_blk_s, t)
    nsel = min(N_SELECT, n_blk_s)
    rows = N_HEADS
    head_of_row = jnp.arange(rows) // GROUP
    lane_head = jnp.arange(KV_W) // HEAD_DIM
    own = (head_of_row[:, None] == lane_head[None, :])
    spread = lambda x: jnp.tile(x.reshape(db, rows, HEAD_DIM), (1, 1, N_KV_HEADS))
    qx = jnp.where(own[None], spread(q_s * scale), 0.0)
    gates_s = jnp.pad(ng_s[:, :3 * N_HEADS].reshape(db, rows, 3), ((0, 0), (0, 0), (0, LANES - 3)))
    sub = PAGE_SIZE // SEL_BLOCK
    ox = _sample_sel(idx_s[:, :, :nsel], page_table, qx, spread(oc_s), gates_s,
                     ks_s[:, None, :], vs_s[:, None, :],
                     state_win_k[0].reshape(db, -1, KV_W), state_win_v[0].reshape(db, -1, KV_W),
                     kw_s[:, None, :], vw_s[:, None, :],
                     cache_sel_k[0].reshape(-1, SEL_BLOCK, KV_W), cache_sel_v[0].reshape(-1, SEL_BLOCK, KV_W), t)
    y_attn_s = jnp.sum(jnp.where(own.reshape(1, rows, N_KV_HEADS, HEAD_DIM),
                                 ox.reshape(db, rows, N_KV_HEADS, HEAD_DIM), 0.0), axis=2).reshape(db, d)
    y_sample = back(h_s, y_conv_s, y_attn_s, mg_s).reshape(db, dseq, d)

    kv5 = lambda x, b, l: x.reshape(1, b, l, N_KV_HEADS, HEAD_DIM)
    wl_p = min(WINDOW, seq)
    tail = lambda x: x.reshape(bsz, seq, KV_W)[:, seq - wl_p:].reshape(1, bsz, wl_p, N_KV_HEADS, HEAD_DIM)
    p_conv = v_p.reshape(bsz, seq, d_conv)[:, seq - (CONV_WIDTH - 1):][None]
    wl_s = state_win_k.shape[2]
    s_win_k = jnp.concatenate([state_win_k[0], kw_s.reshape(db, 1, N_KV_HEADS, HEAD_DIM)], axis=1)[:, -wl_s:][None]
    s_win_v = jnp.concatenate([state_win_v[0], vw_s.reshape(db, 1, N_KV_HEADS, HEAD_DIM)], axis=1)[:, -wl_s:][None]
    s_conv = ext_s[:, -(CONV_WIDTH - 1):][None]
    return (y_prompt, y_sample,
            kv5(kc_p, bsz, seq), kv5(vc_p, bsz, seq), kv5(ks_p, bsz, seq), kv5(vs_p, bsz, seq),
            tail(kw_p), tail(vw_p), p_conv,
            kv5(kc_s, db, 1), kv5(vc_s, db, 1), kv5(ks_s, db, 1), kv5(vs_s, db, 1),
            s_win_k, s_win_v, s_conv)
```

```python
import functools

import numpy as np
import jax
import jax.numpy as jnp
from jax import lax
from jax.experimental import pallas as pl
from jax.experimental.pallas import tpu as pltpu

N_HEADS = 16
N_KV_HEADS = 4
GROUP = N_HEADS // N_KV_HEADS
HEAD_DIM = 64
ROT_DIM = HEAD_DIM // 4
ROPE_THETA = 500000.0
CMP_LEN = 32
CMP_STRIDE = 16
SEL_BLOCK = 64
N_SELECT = 16
WINDOW = 512
Q_TILE = 128
K_TILE = 512
CMP_CHUNK = 256
PAGE_SIZE = 128
CONV_WIDTH = 31
CONV_HALO = 32
EPS = 1e-6
NEG = -1e30
BIG = 1e6
REMOVED = -3e38
LOG2E = 1.4426950408889634
KV_W = N_KV_HEADS * HEAD_DIM
LANES = 128
SUBLANES = 8
SUB_PER_PAGE = PAGE_SIZE // CMP_STRIDE
CMP_ROW = CMP_STRIDE * KV_W
BLK_PER_TILE = K_TILE // SEL_BLOCK
BLK_ROWS = 16
V_ROWS = HEAD_DIM + 16

MXU_DTYPE = jnp.bfloat16
VMEM_LIMIT = 56 * 1024 * 1024
F32 = jnp.float32


def _mm(a, b):
    return jnp.dot(a.astype(MXU_DTYPE), b.astype(MXU_DTYPE), preferred_element_type=F32)


def _mm_nt(a, b):
    return lax.dot_general(a.astype(MXU_DTYPE), b.astype(MXU_DTYPE), (((1,), (1,)), ((), ())),
                           preferred_element_type=F32)


def _sigmoid(x):
    return 1.0 / (1.0 + jnp.exp(-x))


def _const_spec(shape):
    nd = len(shape)
    return pl.BlockSpec(shape, lambda *_: (0,) * nd, pipeline_mode=pl.Buffered(1))


def _params(*sem):
    return pltpu.CompilerParams(dimension_semantics=sem, vmem_limit_bytes=VMEM_LIMIT)


def _ff_chunk(f):
    best = LANES
    for k in range(1, f // LANES + 1):
        c = k * LANES
        if f % c == 0 and c <= 1408:
            best = c
    return best


def _ffn_kernel(*refs, fc, final):
    if final:
        x_ref, g_ref, wg_ref, wu_ref, wd_ref, gf_ref, o_ref = refs
    else:
        x_ref, g_ref, wg_ref, wu_ref, wd_ref, o_ref = refs
    x = x_ref[...]
    xn = x * lax.rsqrt(jnp.mean(x * x, -1, keepdims=True) + EPS) * g_ref[...]
    xb = xn.astype(MXU_DTYPE)
    acc = None
    for c in range(wg_ref.shape[1] // fc):
        sl = slice(c * fc, (c + 1) * fc)
        gate = _mm(xb, wg_ref[:, sl])
        up = _mm(xb, wu_ref[:, sl])
        d = _mm(gate * _sigmoid(gate) * up, wd_ref[sl, :])
        acc = d if acc is None else acc + d
    h = x + 0.5 * acc
    if final:
        h = h * lax.rsqrt(jnp.mean(h * h, -1, keepdims=True) + EPS) * gf_ref[...]
    o_ref[...] = h


def _ffn(x, g, wg, wu, wd, gfin=None, tm=512):
    r, d = x.shape
    f = wg.shape[1]
    tm = min(tm, r)
    row = pl.BlockSpec((tm, d), lambda i: (i, 0))
    in_specs = [row, _const_spec((1, d)), _const_spec((d, f)), _const_spec((d, f)), _const_spec((f, d))]
    args = [x, g.reshape(1, d), wg, wu, wd]
    if gfin is not None:
        in_specs.append(_const_spec((1, d)))
        args.append(gfin.reshape(1, d))
    return pl.pallas_call(
        functools.partial(_ffn_kernel, fc=_ff_chunk(f), final=gfin is not None),
        grid=(r // tm,), in_specs=in_specs, out_specs=row,
        out_shape=jax.ShapeDtypeStruct((r, d), F32),
        compiler_params=_params("parallel"), name="ffn")(*args)


def _rope_tables(pos):
    half = ROT_DIM // 2
    inv = ROPE_THETA ** (-jnp.arange(half, dtype=F32) / half)
    ang = pos.astype(F32)[:, None] * inv[None, :]
    cos, sin = jnp.cos(ang), jnp.sin(ang)
    t = pos.shape[0]
    rest = HEAD_DIM - ROT_DIM
    c = jnp.concatenate([cos, cos, jnp.ones((t, rest), F32)], 1)
    s1 = jnp.concatenate([jnp.zeros((t, half), F32), sin, jnp.zeros((t, rest), F32)], 1)
    s2 = jnp.concatenate([-sin, jnp.zeros((t, half + rest), F32)], 1)
    rep = LANES // HEAD_DIM
    return jnp.tile(c, (1, rep)), jnp.tile(s1, (1, rep)), jnp.tile(s2, (1, rep))


def _rope(z, c, s1, s2):
    out = []
    for j in range(z.shape[1] // LANES):
        x = z[:, j * LANES:(j + 1) * LANES]
        out.append(x * c + pltpu.roll(x, ROT_DIM // 2, 1) * s1 + pltpu.roll(x, LANES - ROT_DIM // 2, 1) * s2)
    return jnp.concatenate(out, axis=1)


def _head_tiles(z, extra=None):
    low = lax.broadcasted_iota(jnp.int32, (1, LANES), 1) < HEAD_DIM
    tiles = []
    for h in range(N_KV_HEADS):
        pair = z[:, (h // 2) * LANES:(h // 2 + 1) * LANES]
        if h % 2:
            pair = pltpu.roll(pair, HEAD_DIM, 1)
        tiles.append(jnp.where(low, pair, 0.0 if extra is None else extra))
    return tiles


def _inproj_kernel(h_ref, g_ref, w_ref, c_ref, s1_ref, s2_ref, *outs, offs, sample, qscale, nper):
    x = h_ref[...]
    tm = x.shape[0]
    xb = (x * lax.rsqrt(jnp.mean(x * x, -1, keepdims=True) + EPS) * g_ref[...]).astype(MXU_DTYPE)
    c, s1, s2 = c_ref[...], s1_ref[...], s2_ref[...]

    def seg(i):
        return _mm(xb, w_ref[:, offs[i]:offs[i + 1]])

    glu = seg(0)
    dc = glu.shape[1] // 2
    v = glu[:, :dc] * _sigmoid(glu[:, dc:])
    q = _rope(seg(1), c, s1, s2) * qscale
    kc, vc = seg(2), seg(3)
    ks, vs = _rope(seg(4), c, s1, s2), seg(5)
    kw, vw = _rope(seg(6), c, s1, s2), seg(7)
    ng = _sigmoid(seg(8))
    mg = _sigmoid(seg(9))
    if sample:
        (v_ref, q_ref, kct_ref, vct_ref, kst_ref, vst_ref, kwt_ref, vwt_ref,
         ks_ref, vs_ref, kw_ref, vw_ref, ng_ref, mg_ref) = outs
        q_ref[...] = q
        for ref, val in ((ks_ref, ks), (vs_ref, vs), (kw_ref, kw), (vw_ref, vw)):
            ref[...] = val
        ng_ref[...] = ng
    else:
        (v_ref, qt_ref, kct_ref, vct_ref, kst_ref, vst_ref, kwt_ref, vwt_ref,
         ksa_ref, kwa_ref, vstb_ref, vwtb_ref, ngt_ref, mg_ref) = outs
        qt_ref[...] = q.T.astype(MXU_DTYPE)
        pos = (pl.program_id(0) % nper) * tm + lax.broadcasted_iota(jnp.int32, (tm, 1), 0)
        blk_in_tile = lax.shift_right_logical(pos, SEL_BLOCK.bit_length() - 1) & (BLK_PER_TILE - 1)
        lane = lax.broadcasted_iota(jnp.int32, (1, LANES), 1)
        onehot = jnp.where(lane - HEAD_DIM == blk_in_tile, 1.0, 0.0)
        for h, tile in enumerate(_head_tiles(ks, onehot)):
            ksa_ref[h] = tile.astype(MXU_DTYPE)
        for h, tile in enumerate(_head_tiles(kw)):
            kwa_ref[h] = tile.astype(MXU_DTYPE)
        ngt_ref[...] = ng.T[0:3 * N_HEADS, :]
    v_ref[...] = v
    mg_ref[...] = mg
    kct_ref[...] = kc.T
    vct_ref[...] = vc.T
    kst_ref[...] = ks.T
    kwt_ref[...] = kw.T
    vst = vs.T
    vwt = vw.T
    vst_ref[...] = vst
    vwt_ref[...] = vwt
    if not sample:
        ones = jnp.where(lax.broadcasted_iota(jnp.int32, (V_ROWS - HEAD_DIM, tm), 0) == 0, 1.0, 0.0)
        for h in range(N_KV_HEADS):
            rs = slice(h * HEAD_DIM, (h + 1) * HEAD_DIM)
            vstb_ref[h] = jnp.concatenate([vst[rs], ones], axis=0).astype(MXU_DTYPE)
            vwtb_ref[h] = jnp.concatenate([vwt[rs], ones], axis=0).astype(MXU_DTYPE)


def _inproj(h, g, w_pad, offs, tables, nseq, rows_per_seq, sample, qscale, tm=512):
    r, d = h.shape
    tm = min(tm, rows_per_seq)
    nper = rows_per_seq // tm
    nat = lambda w, dt=F32: (pl.BlockSpec((tm, w), lambda i: (i, 0)), jax.ShapeDtypeStruct((r, w), dt))
    tr = lambda w, dt=F32: (pl.BlockSpec((None, w, tm), lambda i: (i // nper, 0, i % nper)),
                            jax.ShapeDtypeStruct((nseq, w, rows_per_seq), dt))
    aug = (pl.BlockSpec((None, N_KV_HEADS, tm, LANES), lambda i: (i // nper, 0, i % nper, 0)),
           jax.ShapeDtypeStruct((nseq, N_KV_HEADS, rows_per_seq, LANES), MXU_DTYPE))
    tab = pl.BlockSpec((tm, LANES), lambda i: (i % nper, 0))
    dc = (offs[1] - offs[0]) // 2
    dq = offs[2] - offs[1]
    planes = [tr(KV_W)] * 6
    if sample:
        outs = [nat(dc), nat(dq)] + planes + [nat(KV_W)] * 4 + [nat(LANES), nat(offs[10] - offs[9])]
    else:
        vplane = (pl.BlockSpec((None, N_KV_HEADS, V_ROWS, tm), lambda i: (i // nper, 0, 0, i % nper)),
                  jax.ShapeDtypeStruct((nseq, N_KV_HEADS, V_ROWS, rows_per_seq), MXU_DTYPE))
        outs = ([nat(dc), tr(dq, MXU_DTYPE)] + planes + [aug, aug, vplane, vplane,
                tr(3 * N_HEADS), nat(offs[10] - offs[9])])
    return pl.pallas_call(
        functools.partial(_inproj_kernel, offs=offs, sample=sample, qscale=qscale, nper=nper),
        grid=(r // tm,),
        in_specs=[pl.BlockSpec((tm, d), lambda i: (i, 0)), _const_spec((1, d)), _const_spec(w_pad.shape),
                  tab, tab, tab],
        out_specs=[o[0] for o in outs], out_shape=[o[1] for o in outs],
        compiler_params=_params("parallel"), name="inproj")(h, g.reshape(1, d), w_pad, *tables)


def _conv_tail(y, b_ref, lg_ref, lb_ref, wo_ref):
    y = y + b_ref[...]
    mu = jnp.mean(y, -1, keepdims=True)
    yc = y - mu
    var = jnp.mean(yc * yc, -1, keepdims=True)
    yn = yc * lax.rsqrt(var + EPS) * lg_ref[...] + lb_ref[...]
    return _mm(yn * _sigmoid(yn), wo_ref[...])


def _conv_kernel(main_ref, halo_ref, w_ref, b_ref, lg_ref, lb_ref, wo_ref, o_ref, ext_ref):
    ts = main_ref.shape[0]
    ext_ref[0:CONV_HALO, :] = jnp.where(pl.program_id(1) == 0, 0.0, halo_ref[...])
    ext_ref[CONV_HALO:CONV_HALO + ts, :] = main_ref[...]
    lead = CONV_HALO - (CONV_WIDTH - 1)
    acc = None
    for k in range(CONV_WIDTH):
        term = ext_ref[pl.ds(lead + k, ts), :] * w_ref[k:k + 1, :]
        acc = term if acc is None else acc + term
    o_ref[...] = _conv_tail(acc, b_ref, lg_ref, lb_ref, wo_ref)


def _conv(v, w, b, lg, lb, wo, ts=512):
    bsz, l, c = v.shape
    d = wo.shape[1]
    step = ts // CONV_HALO
    vec = lambda a: a.reshape(1, c)
    return pl.pallas_call(
        _conv_kernel,
        grid=(bsz, l // ts),
        in_specs=[pl.BlockSpec((None, ts, c), lambda bi, i: (bi, i, 0)),
                  pl.BlockSpec((None, CONV_HALO, c), lambda bi, i: (bi, jnp.maximum(i * step - 1, 0), 0)),
                  _const_spec((CONV_WIDTH, c)), _const_spec((1, c)), _const_spec((1, c)), _const_spec((1, c)),
                  _const_spec((c, d))],
        out_specs=pl.BlockSpec((None, ts, d), lambda bi, i: (bi, i, 0)),
        out_shape=jax.ShapeDtypeStruct((bsz, l, d), F32),
        scratch_shapes=[pltpu.VMEM((ts + CONV_HALO, c), F32)],
        compiler_params=_params("parallel", "parallel"), name="conv")(v, v, w, vec(b), vec(lg), vec(lb), wo)


def _conv_sample_kernel(st_ref, v_ref, w_ref, b_ref, lg_ref, lb_ref, wo_ref, o_ref):
    acc = v_ref[...] * w_ref[CONV_WIDTH - 1:CONV_WIDTH, :]
    for k in range(CONV_WIDTH - 1):
        acc = acc + st_ref[k] * w_ref[k:k + 1, :]
    o_ref[...] = _conv_tail(acc, b_ref, lg_ref, lb_ref, wo_ref)


def _conv_sample(st, v, w, b, lg, lb, wo):
    rows, c = v.shape
    d = wo.shape[1]
    vec = lambda a: a.reshape(1, c)
    args = (st, v, w, vec(b), vec(lg), vec(lb), wo)
    return pl.pallas_call(
        _conv_sample_kernel, grid=(1,),
        in_specs=[_const_spec(a.shape) for a in args],
        out_specs=pl.BlockSpec((rows, d), lambda i: (0, 0)),
        out_shape=jax.ShapeDtypeStruct((rows, d), F32),
        compiler_params=_params("arbitrary"), name="conv_sample")(*args)


def _compress_kernel(pt_ref, x_hbm, w1_ref, pe_ref, w1f_ref, b1_ref, w2_ref, w2h_ref, nat_ref, tr_ref,
                     buf, tbuf, xb, second, sem, *, pc, n_pages, paged):
    s = pl.program_id(0)
    ci = pl.program_id(1)
    m = pc * SUB_PER_PAGE

    def page_copy(p):
        pg = pt_ref[s * n_pages + jnp.minimum(ci * pc + p, n_pages - 1)]
        if paged:
            src = x_hbm.at[pg]
        else:
            src = x_hbm.at[s, :, :, pl.ds(pl.multiple_of(pg * PAGE_SIZE, PAGE_SIZE), PAGE_SIZE)]
        return pltpu.make_async_copy(src, buf.at[p], sem)

    for p in range(pc + 1):
        page_copy(p).start()
    for p in range(pc + 1):
        page_copy(p).wait()

    def to_rows(p, carry):
        r0 = pl.multiple_of(p * PAGE_SIZE, PAGE_SIZE)
        for j in range(N_KV_HEADS // 2):
            plane = buf[p, 2 * j:2 * j + 2].reshape(2 * HEAD_DIM, PAGE_SIZE)
            tbuf[j, pl.ds(r0, PAGE_SIZE), :] = plane.T
        return carry

    lax.fori_loop(0, pc + 1, to_rows, 0)
    for l in range(CMP_STRIDE):
        for j in range(N_KV_HEADS // 2):
            col = l * KV_W + j * LANES
            xb[:, col:col + LANES] = tbuf[j, pl.ds(l, m + SUB_PER_PAGE, stride=CMP_STRIDE), :].astype(MXU_DTYPE)
    fs = _mm(xb[...], w1_ref[...])
    nh = fs.shape[1] // 2
    second[...] = fs[:, nh:]
    c = _mm(pe_ref[...], w1f_ref[...])[0:1, :] + b1_ref[...]
    ct = jnp.concatenate([c] * N_KV_HEADS, axis=1)
    hid = fs[0:m, :nh] + second[pl.ds(1, m), :] + ct
    act = (hid * _sigmoid(hid)).astype(MXU_DTYPE)
    tr_ref[...] = _mm(act, w2_ref[...]).T.astype(MXU_DTYPE)
    for h in range(N_KV_HEADS):
        nat_ref[h] = _mm(act, w2h_ref[h]).astype(MXU_DTYPE)


def _compress(x, page_table, weights, pc, paged):
    w1_big, pe8, w1_flat, b1, w2_big, w2_heads = weights
    n_seq, n_pages = page_table.shape
    m = pc * SUB_PER_PAGE
    n_sub = n_pages * SUB_PER_PAGE
    hidden = b1.shape[0]
    grid_spec = pltpu.PrefetchScalarGridSpec(
        num_scalar_prefetch=1,
        grid=(n_seq, n_pages // pc),
        in_specs=[pl.BlockSpec(memory_space=pl.ANY),
                  _const_spec(w1_big.shape), _const_spec(pe8.shape), _const_spec(w1_flat.shape),
                  _const_spec((1, hidden)), _const_spec(w2_big.shape), _const_spec(w2_heads.shape)],
        out_specs=[pl.BlockSpec((None, N_KV_HEADS, m, LANES), lambda s, ci, pt: (s, 0, ci, 0)),
                   pl.BlockSpec((None, KV_W, m), lambda s, ci, pt: (s, 0, ci))],
        scratch_shapes=[pltpu.VMEM((pc + 1, N_KV_HEADS, HEAD_DIM, PAGE_SIZE), F32),
                        pltpu.VMEM((N_KV_HEADS // 2, (pc + 1) * PAGE_SIZE, LANES), F32),
                        pltpu.VMEM((m + SUB_PER_PAGE, CMP_ROW), MXU_DTYPE),
                        pltpu.VMEM((m + SUB_PER_PAGE, w1_big.shape[1] // 2), F32),
                        pltpu.SemaphoreType.DMA(())])
    return pl.pallas_call(
        functools.partial(_compress_kernel, pc=pc, n_pages=n_pages, paged=paged),
        grid_spec=grid_spec,
        out_shape=[jax.ShapeDtypeStruct((n_seq, N_KV_HEADS, n_sub, LANES), MXU_DTYPE),
                   jax.ShapeDtypeStruct((n_seq, KV_W, n_sub), MXU_DTYPE)],
        compiler_params=_params("arbitrary", "arbitrary"), name="compress")(
            page_table.reshape(-1), x, w1_big, pe8, w1_flat, b1.reshape(1, hidden), w2_big, w2_heads)


def _compress_weights(pe, w1, b1, w2):
    hidden = w1.shape[-1]
    eye = jnp.eye(N_KV_HEADS, dtype=F32)
    halves = w1.reshape(2, CMP_STRIDE, HEAD_DIM, hidden)
    w1_big = jnp.einsum('sldk,hg->lhdsgk', halves, eye).reshape(CMP_ROW, 2 * N_KV_HEADS * hidden)
    w2_big = jnp.einsum('kd,hg->hkgd', w2, eye).reshape(N_KV_HEADS * hidden, KV_W)
    w2_pad = jnp.pad(w2, ((0, 0), (0, LANES - HEAD_DIM)))
    w2_heads = jnp.einsum('kd,hg->ghkd', w2_pad, eye).reshape(N_KV_HEADS, N_KV_HEADS * hidden, LANES)
    pe8 = jnp.zeros((SUBLANES, CMP_LEN * HEAD_DIM), F32).at[0].set(pe.reshape(-1))
    return (w1_big.astype(MXU_DTYPE), pe8, w1.reshape(CMP_LEN * HEAD_DIM, hidden).astype(MXU_DTYPE), b1,
            w2_big.astype(MXU_DTYPE), w2_heads.astype(MXU_DTYPE))


def _overlap(n_cmp_pad, n_blk_pad):
    ci = np.arange(n_cmp_pad)[:, None] * CMP_STRIDE
    bs = np.arange(n_blk_pad)[None, :] * SEL_BLOCK
    return ((ci < bs + SEL_BLOCK) & (ci + CMP_LEN > bs)).astype(np.float32)


def _attn_kernel(q_ref, g_ref, kc_ref, vct_ref, ov_ref, ks_ref, vst_ref, kw_ref, vwt_ref, o_ref,
                 bias_ref, sc_ref, imp_ref, sa_ref, sb_ref, *, n_cmp, n_blk):
    s0 = pl.program_id(2) * Q_TILE
    cols = GROUP * Q_TILE
    qt = jnp.concatenate([q_ref[g] for g in range(GROUP)], axis=1)
    q0 = jnp.concatenate([qt, jnp.zeros((LANES - HEAD_DIM, cols), qt.dtype)], axis=0)
    tq = s0 + (lax.broadcasted_iota(jnp.int32, (1, cols), 1) & (Q_TILE - 1))
    tq1 = s0 + lax.broadcasted_iota(jnp.int32, (1, Q_TILE), 1)

    ncp = kc_ref.shape[0]
    ch = min(CMP_CHUNK, ncp)
    last_c = lax.shift_right_logical(s0 + (Q_TILE - CMP_LEN), CMP_STRIDE.bit_length() - 1)
    n_cc = jnp.minimum(last_c // ch + 1, ncp // ch)
    ci0 = lax.broadcasted_iota(jnp.int32, (ch, 1), 0)

    def cmp_scores(cc, m):
        r0 = pl.multiple_of(cc * ch, ch)
        c = ci0 + r0
        s = jnp.where((c * CMP_STRIDE + (CMP_LEN - 1) <= tq) & (c < n_cmp), _mm(kc_ref[pl.ds(r0, ch), :], q0), NEG)
        sc_ref[pl.ds(r0, ch), :] = s
        return jnp.maximum(m, jnp.max(s, axis=0, keepdims=True))

    m_c = lax.fori_loop(0, n_cc, cmp_scores, jnp.full((1, cols), NEG, F32))
    imp_ref[...] = jnp.zeros(imp_ref.shape, F32)

    def cmp_weights(cc, carry):
        l, acc = carry
        r0 = pl.multiple_of(cc * ch, ch)
        p = jnp.exp2(sc_ref[pl.ds(r0, ch), :] - m_c)
        pb = p.astype(MXU_DTYPE)
        imp_ref[...] += _mm(ov_ref[:, pl.ds(r0, ch)], pb)
        return l + jnp.sum(p, axis=0, keepdims=True), acc + _mm(vct_ref[:, pl.ds(r0, ch)], pb)

    l_c, acc_c = lax.fori_loop(0, n_cc, cmp_weights,
                               (jnp.zeros((1, cols), F32), jnp.zeros((HEAD_DIM, cols), F32)))
    inv_c = jnp.where(m_c > 0.5 * NEG, 1.0 / jnp.maximum(l_c, 1e-30), 0.0)
    o_c = acc_c * inv_c
    impn = imp_ref[...] * inv_c
    imp = impn[:, 0:Q_TILE]
    for g in range(1, GROUP):
        imp = imp + impn[:, g * Q_TILE:(g + 1) * Q_TILE]

    wk = WINDOW + Q_TILE
    w0 = pl.multiple_of(jnp.maximum(s0 - WINDOW, 0), Q_TILE)
    wpos = w0 + lax.broadcasted_iota(jnp.int32, (wk, 1), 0)
    sw = jnp.where((wpos <= tq) & (wpos > tq - WINDOW), _mm(kw_ref[pl.ds(w0, wk), :], q0), NEG)
    acc_w = _mm(vwt_ref[:, pl.ds(w0, wk)], jnp.exp2(sw - jnp.max(sw, axis=0, keepdims=True)))
    o_w = acc_w[0:HEAD_DIM] / acc_w[HEAD_DIM:HEAD_DIM + 1]

    blk = lax.broadcasted_iota(jnp.int32, (n_blk, 1), 0)
    blk_f = blk.astype(F32)
    cur = lax.shift_right_logical(tq1, SEL_BLOCK.bit_length() - 1)
    forced = (blk == 0) | (blk == cur) | (blk == cur - 1)
    valid = blk * SEL_BLOCK <= tq1
    score = jnp.where(valid, imp + jnp.where(forced, BIG, 0.0), -BIG)
    for _ in range(min(N_SELECT, n_blk)):
        top = jnp.max(score, axis=0, keepdims=True)
        first = jnp.min(jnp.where(score == top, blk_f, float(n_blk)), axis=0, keepdims=True)
        score = jnp.where(blk_f == first, REMOVED, score)
    bias_ref[...] = jnp.where((score == REMOVED) & valid, 0.0, NEG)

    kpos0 = lax.broadcasted_iota(jnp.int32, (K_TILE, 1), 0)
    q_pad = jnp.zeros((LANES - HEAD_DIM - BLK_ROWS, cols), qt.dtype)
    b_pad = jnp.zeros((BLK_ROWS - BLK_PER_TILE, cols), F32)

    n_kt = ks_ref.shape[0] // K_TILE

    def sel_scores(j, s_ref):
        k0 = pl.multiple_of(j * K_TILE, K_TILE)
        b8 = bias_ref[pl.ds(pl.multiple_of(j * BLK_PER_TILE, BLK_PER_TILE), BLK_PER_TILE), :]
        rows = jnp.concatenate([jnp.concatenate([b8] * GROUP, axis=1), b_pad], axis=0).astype(qt.dtype)
        s_ref[...] = _mm(ks_ref[pl.ds(k0, K_TILE), :], jnp.concatenate([qt, rows, q_pad], axis=0))

    def sel_update(j, s, carry):
        m, acc = carry
        m_new = jnp.maximum(m, jnp.max(s, axis=0, keepdims=True))
        p = jnp.exp2(s - m_new)
        acc = jnp.exp2(m - m_new) * acc + _mm(vst_ref[:, pl.ds(pl.multiple_of(j * K_TILE, K_TILE), K_TILE)], p)
        return m_new, acc

    def sel_pair(i, carry):
        sel_scores(2 * i + 1, sb_ref)
        carry = sel_update(2 * i, sa_ref[...], carry)
        sel_scores(2 * i + 2, sa_ref)
        return sel_update(2 * i + 1, sb_ref[...], carry)

    last = (s0 + Q_TILE - 1) // K_TILE
    n_pairs = last // 2
    sel_scores(0, sa_ref)
    carry = lax.fori_loop(0, n_pairs, sel_pair, (jnp.full((1, cols), NEG, F32), jnp.zeros((V_ROWS, cols), F32)))
    ta = 2 * n_pairs
    tb = jnp.minimum(ta + 1, n_kt - 1)
    sel_scores(tb, sb_ref)
    carry = sel_update(ta, jnp.where(kpos0 + ta * K_TILE <= tq, sa_ref[...], NEG), carry)
    _, acc_s = sel_update(tb, jnp.where(kpos0 + (ta + 1) * K_TILE <= tq, sb_ref[...], NEG), carry)
    o_s = acc_s[0:HEAD_DIM] / acc_s[HEAD_DIM:HEAD_DIM + 1]

    gate = lambda br: jnp.concatenate([g_ref[g * 3 + br:g * 3 + br + 1, :] for g in range(GROUP)], axis=1)
    o = gate(0) * o_c + gate(1) * o_s + gate(2) * o_w
    for g in range(GROUP):
        o_ref[g] = o[:, g * Q_TILE:(g + 1) * Q_TILE]


def _attn_prompt(qt, gt, kc, vct, ks, vst, kw, vwt, n_cmp):
    b, h, grp, hd, s = qt.shape
    ncp = kc.shape[2]
    n_blk = s // SEL_BLOCK
    nt = s // Q_TILE
    cols = grp * Q_TILE
    ov = jnp.asarray(_overlap(ncp, n_blk).T, MXU_DTYPE)
    one = pl.Buffered(1)
    rows = lambda n: pl.BlockSpec((None, None, n, LANES), lambda bi, hi, t: (bi, hi, 0, 0), pipeline_mode=one)
    plane = lambda n: pl.BlockSpec((None, hd, n), lambda bi, hi, t: (bi, hi, 0), pipeline_mode=one)
    vplane = pl.BlockSpec((None, None, V_ROWS, s), lambda bi, hi, t: (bi, hi, 0, 0), pipeline_mode=one)
    qspec = pl.BlockSpec((None, None, grp, hd, Q_TILE), lambda bi, hi, t: (bi, hi, 0, 0, t))
    return pl.pallas_call(
        functools.partial(_attn_kernel, n_cmp=n_cmp, n_blk=n_blk),
        grid=(b, h, nt),
        in_specs=[qspec, pl.BlockSpec((None, None, 3 * grp, Q_TILE), lambda bi, hi, t: (bi, hi, 0, t)),
                  rows(ncp), plane(ncp), _const_spec(ov.shape), rows(s), vplane, rows(s), vplane],
        out_specs=qspec,
        out_shape=jax.ShapeDtypeStruct((b, h, grp, hd, s), F32),
        scratch_shapes=[pltpu.VMEM((n_blk, Q_TILE), F32), pltpu.VMEM((ncp, cols), F32),
                        pltpu.VMEM((n_blk, cols), F32), pltpu.VMEM((K_TILE, cols), F32),
                        pltpu.VMEM((K_TILE, cols), F32)],
        compiler_params=_params("parallel", "parallel", "arbitrary"), name="attn_prompt")(
            qt, gt, kc, vct, ov, ks, vst, kw, vwt)


def _sample_cmp_kernel(q_ref, kct_ref, vc_ref, ov_ref, oc_ref, idx_ref, *, n_cmp, n_blk, t):
    ncp = kct_ref.shape[1]
    nbp = ov_ref.shape[1]
    ci = lax.broadcasted_iota(jnp.int32, (1, ncp), 1)
    mask = (ci * CMP_STRIDE + (CMP_LEN - 1) <= t) & (ci < n_cmp)
    sums = []
    for h in range(N_KV_HEADS):
        s = jnp.where(mask, _mm(q_ref[h], kct_ref[h * HEAD_DIM:(h + 1) * HEAD_DIM, :]), NEG)
        m = jnp.max(s, axis=1, keepdims=True)
        p = jnp.where(mask, jnp.exp(s - m), 0.0)
        p = p / jnp.maximum(jnp.sum(p, axis=1, keepdims=True), 1e-30)
        oc_ref[h] = _mm(p, vc_ref[h])
        sums.append(jnp.sum(p, axis=0, keepdims=True))
    imp = _mm(jnp.concatenate(sums, axis=0), ov_ref[...])
    blk = lax.broadcasted_iota(jnp.int32, (1, nbp), 1)
    cur = t // SEL_BLOCK
    forced = (blk == 0) | (blk == cur) | (blk == cur - 1)
    score = jnp.where(blk * SEL_BLOCK <= t, imp + jnp.where(forced, BIG, 0.0), -BIG)
    score = jnp.where(blk < n_blk, score, REMOVED)
    blk_f = blk.astype(F32)
    lane = lax.broadcasted_iota(jnp.int32, (1, LANES), 1)
    out = jnp.zeros((N_KV_HEADS, LANES), F32)
    for r in range(min(N_SELECT, n_blk)):
        top = jnp.max(score, axis=1, keepdims=True)
        first = jnp.min(jnp.where(score == top, blk_f, float(nbp)), axis=1, keepdims=True)
        out = jnp.where(lane == r, first, out)
        score = jnp.where(blk_f == first, REMOVED, score)
    idx_ref[...] = out.astype(jnp.int32)


def _sample_cmp(q4, kct, vc, n_cmp, n_blk, t):
    db, _, _, hd = q4.shape
    ncp = kct.shape[2]
    nbp = -(-n_blk // LANES) * LANES
    ov = jnp.asarray(_overlap(ncp, nbp), MXU_DTYPE)
    per = lambda r, w: pl.BlockSpec((None, N_KV_HEADS, r, w), lambda b: (b, 0, 0, 0))
    return pl.pallas_call(
        functools.partial(_sample_cmp_kernel, n_cmp=n_cmp, n_blk=n_blk, t=t),
        grid=(db,),
        in_specs=[per(GROUP, hd), pl.BlockSpec((None, KV_W, ncp), lambda b: (b, 0, 0)), per(ncp, LANES),
                  _const_spec(ov.shape)],
        out_specs=[per(GROUP, LANES), pl.BlockSpec((None, N_KV_HEADS, LANES), lambda b: (b, 0, 0))],
        out_shape=[jax.ShapeDtypeStruct((db, N_KV_HEADS, GROUP, LANES), F32),
                   jax.ShapeDtypeStruct((db, N_KV_HEADS, LANES), jnp.int32)],
        compiler_params=_params("parallel"), name="sample_cmp")(q4, kct, vc, ov)


def _sample_sel_kernel(idx_ref, pt_ref, q_ref, oc_ref, g_ref, ksn_ref, vsn_ref, kwn_ref, vwn_ref, wk_ref, wv_ref,
                       pk_hbm, pv_hbm, o_ref, kbuf, vbuf, sem, *, n_past_blk, n_pages, nsel, t):
    b = pl.program_id(0)
    sub = PAGE_SIZE // SEL_BLOCK

    def sel_block(h, n):
        return idx_ref[(b * N_KV_HEADS + h) * nsel + n]

    def page_copies(h, n):
        pg = pt_ref[b * n_pages + jnp.minimum(sel_block(h, n), n_past_blk - 1) // sub]
        dst = pl.ds(n * PAGE_SIZE, PAGE_SIZE)
        return (pltpu.make_async_copy(pk_hbm.at[pg, h], kbuf.at[h, :, dst], sem.at[0]),
                pltpu.make_async_copy(pv_hbm.at[pg, h], vbuf.at[h, :, dst], sem.at[1]))

    for h in range(N_KV_HEADS):
        for n in range(nsel):
            for cp in page_copies(h, n):
                cp.start()
    for h in range(N_KV_HEADS):
        for n in range(nsel):
            for cp in page_copies(h, n):
                cp.wait()

    lane = lax.broadcasted_iota(jnp.int32, (1, nsel * PAGE_SIZE), 1)
    seg = lax.shift_right_logical(lane, PAGE_SIZE.bit_length() - 1)
    half = lax.shift_right_logical(lane, SEL_BLOCK.bit_length() - 1) & (sub - 1)
    wl = wk_ref.shape[2]
    wpos = t - wl + lax.broadcasted_iota(jnp.int32, (1, wl), 1)
    okw = (wpos > t - WINDOW) & (wpos >= 0)
    for h in range(N_KV_HEADS):
        q = q_ref[h]
        want = jnp.full(lane.shape, -1, jnp.int32)
        for n in range(nsel):
            blk = sel_block(h, n)
            want = jnp.where(seg == n, jnp.where(blk < n_past_blk, blk % sub, -1), want)
        ok = half == want
        s = jnp.where(ok, _mm(q, kbuf[h]), NEG)
        s_new = jnp.sum(q * ksn_ref[h:h + 1, :], axis=1, keepdims=True)
        m = jnp.maximum(jnp.max(s, axis=1, keepdims=True), s_new)
        p = jnp.where(ok, jnp.exp(s - m), 0.0)
        p_new = jnp.exp(s_new - m)
        o_s = (_mm_nt(p, vbuf[h]) + p_new * vsn_ref[h:h + 1, :]) / (jnp.sum(p, axis=1, keepdims=True) + p_new)
        sw = jnp.where(okw, _mm(q, wk_ref[h]), NEG)
        sw_new = jnp.sum(q * kwn_ref[h:h + 1, :], axis=1, keepdims=True)
        mw = jnp.maximum(jnp.max(sw, axis=1, keepdims=True), sw_new)
        pw = jnp.where(okw, jnp.exp(sw - mw), 0.0)
        pw_new = jnp.exp(sw_new - mw)
        o_w = (_mm_nt(pw, wv_ref[h]) + pw_new * vwn_ref[h:h + 1, :]) / (jnp.sum(pw, axis=1, keepdims=True) + pw_new)
        g = g_ref[h]
        o_ref[h] = g[:, 0:1] * oc_ref[h][:, 0:HEAD_DIM] + g[:, 1:2] * o_s + g[:, 2:3] * o_w


def _sample_sel(idx, page_table, q4, oc, gates, ks_new, vs_new, kw_new, vw_new, win_k, win_v, pool_k, pool_v, t):
    db, _, grp, hd = q4.shape
    nsel = idx.shape[2]
    n_pages = page_table.shape[1]
    n_past_blk = n_pages * (PAGE_SIZE // SEL_BLOCK)
    wl = win_k.shape[3]
    per = lambda *shape: pl.BlockSpec((None,) + shape, lambda b, *_: (b,) + (0,) * len(shape))
    hbm = pl.BlockSpec(memory_space=pl.ANY)
    grid_spec = pltpu.PrefetchScalarGridSpec(
        num_scalar_prefetch=2,
        grid=(db,),
        in_specs=[per(N_KV_HEADS, grp, hd), per(N_KV_HEADS, grp, LANES), per(N_KV_HEADS, grp, LANES)]
                 + [per(N_KV_HEADS, hd)] * 4 + [per(N_KV_HEADS, hd, wl)] * 2 + [hbm, hbm],
        out_specs=per(N_KV_HEADS, grp, hd),
        scratch_shapes=[pltpu.VMEM((N_KV_HEADS, hd, nsel * PAGE_SIZE), F32),
                        pltpu.VMEM((N_KV_HEADS, hd, nsel * PAGE_SIZE), F32),
                        pltpu.SemaphoreType.DMA((2,))])
    return pl.pallas_call(
        functools.partial(_sample_sel_kernel, n_past_blk=n_past_blk, n_pages=n_pages, nsel=nsel, t=t),
        grid_spec=grid_spec,
        out_shape=jax.ShapeDtypeStruct((db, N_KV_HEADS, grp, hd), F32),
        compiler_params=_params("arbitrary"), name="sample_sel")(
            idx.reshape(-1), page_table.reshape(-1), q4, oc, gates, ks_new, vs_new, kw_new, vw_new,
            win_k, win_v, pool_k, pool_v)


def _merge_kernel(h_ref, yc_ref, ya_ref, mg_ref, wa_ref, wo_ref, o_ref, *, planes):
    d = h_ref.shape[1]
    ya = ya_ref[...].T if planes else ya_ref[...]
    m = mg_ref[:, :d] * yc_ref[...] + mg_ref[:, d:] * _mm(ya, wa_ref[...])
    o_ref[...] = h_ref[...] + _mm(m, wo_ref[...])


def _merge(h, y_conv, y_attn, mg, wa, wo, rows_per_seq, planes, tm=512):
    r, d = h.shape
    tm = min(tm, rows_per_seq)
    nper = rows_per_seq // tm
    row = lambda w: pl.BlockSpec((tm, w), lambda i: (i, 0))
    da = wa.shape[0]
    ya_spec = pl.BlockSpec((None, da, tm), lambda i: (i // nper, 0, i % nper)) if planes else row(da)
    return pl.pallas_call(
        functools.partial(_merge_kernel, planes=planes), grid=(r // tm,),
        in_specs=[row(d), row(d), ya_spec, row(2 * d), _const_spec(wa.shape), _const_spec(wo.shape)],
        out_specs=row(d), out_shape=jax.ShapeDtypeStruct((r, d), F32),
        compiler_params=_params("parallel"), name="merge")(h, y_conv, y_attn, mg, wa, wo)


def kernel(x_prompt, x_sample, cache_cmp_k, cache_cmp_v, cache_sel_k, cache_sel_v, state_win_k, state_win_v, state_conv, page_table, norm_ffn1, ffn1_gate, ffn1_up, ffn1_down, norm_mix, w_in, conv_w, conv_b, conv_ln_g, conv_ln_b, w_conv_out, cmp_pe_k, cmp_w1_k, cmp_b1_k, cmp_w2_k, cmp_pe_v, cmp_w1_v, cmp_b1_v, cmp_w2_v, w_attn_out, w_out, norm_ffn2, ffn2_gate, ffn2_up, ffn2_down, norm_final):
    bsz, seq, d = x_prompt.shape
    db, dseq, _ = x_sample.shape
    depth = norm_ffn1.shape[0]
    assert depth == 1 and dseq == 1 and d == N_HEADS * HEAD_DIM
    assert seq % K_TILE == 0 and seq >= WINDOW + Q_TILE
    assert state_conv.shape[2] == CONV_WIDTH - 1
    n_pages = page_table.shape[1]
    past = n_pages * PAGE_SIZE
    d_conv = d // 2
    scale = HEAD_DIM ** -0.5
    cast = lambda w: w.astype(MXU_DTYPE)

    col_sizes = (2 * d_conv, N_HEADS * HEAD_DIM) + (KV_W,) * 6 + (3 * N_HEADS, 2 * d)
    starts = np.concatenate([[0], np.cumsum(col_sizes)])
    offs, pieces, pos = [0], [], 0
    for i, n in enumerate(col_sizes):
        pad = -n % LANES
        pieces.append(w_in[0][:, starts[i]:starts[i + 1]])
        if pad:
            pieces.append(jnp.zeros((d, pad), F32))
        pos += n + pad
        offs.append(pos)
    w_in_pad = cast(jnp.concatenate(pieces, axis=1))
    offs = tuple(offs)
    ffn1 = (norm_ffn1[0], cast(ffn1_gate[0]), cast(ffn1_up[0]), cast(ffn1_down[0]))
    ffn2 = (norm_ffn2[0], cast(ffn2_gate[0]), cast(ffn2_up[0]), cast(ffn2_down[0]))
    wa, wo, wco = cast(w_attn_out[0]), cast(w_out[0]), cast(w_conv_out[0])
    convp = (conv_w[0], conv_b[0], conv_ln_g[0], conv_ln_b[0], wco)
    cmpk = _compress_weights(cmp_pe_k[0], cmp_w1_k[0], cmp_b1_k[0], cmp_w2_k[0])
    cmpv = _compress_weights(cmp_pe_v[0], cmp_w1_v[0], cmp_b1_v[0], cmp_w2_v[0])
    planes4 = lambda x: x.reshape(x.shape[0], N_KV_HEADS, HEAD_DIM, x.shape[2])
    state5 = lambda x: jnp.transpose(planes4(x), (0, 3, 1, 2))[None]
    to_planes = lambda x: jnp.transpose(x, (0, 2, 3, 1))

    rp = bsz * seq
    h_p = _ffn(x_prompt.reshape(rp, d), *ffn1)
    (v_p, qt_p, kct_p, vct_p, kst_p, vst_p, kwt_p, vwt_p, ksa_p, kwa_p, vstb_p, vwtb_p, ngt_p, mg_p) = _inproj(
        h_p, norm_mix[0], w_in_pad, offs, _rope_tables(jnp.arange(seq, dtype=jnp.int32)), bsz, seq,
        sample=False, qscale=scale * LOG2E)
    y_conv_p = _conv(v_p.reshape(bsz, seq, d_conv), *convp).reshape(rp, d)
    pages_p = seq // PAGE_SIZE
    pc_p = min(32, pages_p)
    ident = jnp.tile(jnp.arange(pages_p, dtype=jnp.int32)[None], (bsz, 1))
    kc_rows, _ = _compress(planes4(kct_p), ident, cmpk, pc_p, paged=False)
    _, vc_planes = _compress(planes4(vct_p), ident, cmpv, pc_p, paged=False)
    ot = _attn_prompt(qt_p.reshape(bsz, N_KV_HEADS, GROUP, HEAD_DIM, seq),
                      ngt_p.reshape(bsz, N_KV_HEADS, 3 * GROUP, seq),
                      kc_rows, vc_planes, ksa_p, vstb_p, kwa_p, vwtb_p, seq // CMP_STRIDE - 1)
    h2_p = _merge(h_p, y_conv_p, ot.reshape(bsz, d, seq), mg_p, wa, wo, seq, planes=True)
    y_prompt = _ffn(h2_p, *ffn2, gfin=norm_final).reshape(bsz, seq, d)

    t = past
    h_s = _ffn(x_sample.reshape(db, d), *ffn1)
    (v_s, q_s, kct_s, vct_s, kst_s, vst_s, kwt_s, vwt_s, ks_s, vs_s, kw_s, vw_s, ng_s, mg_s) = _inproj(
        h_s, norm_mix[0], w_in_pad, offs, _rope_tables(jnp.full((db,), t, jnp.int32)), 1, db,
        sample=True, qscale=scale)
    y_conv_s = _conv_sample(jnp.transpose(state_conv[0], (1, 0, 2)), v_s, *convp)
    pc_s = min(32, n_pages)
    n_sub_s = (past + dseq) // CMP_STRIDE
    assert n_sub_s == n_pages * SUB_PER_PAGE
    _, kc_planes_s = _compress(to_planes(cache_cmp_k[0]), page_table, cmpk, pc_s, paged=True)
    vc_rows_s, _ = _compress(to_planes(cache_cmp_v[0]), page_table, cmpv, pc_s, paged=True)
    n_blk_s = -(-(past + dseq) // SEL_BLOCK)
    q4 = q_s.reshape(db, N_KV_HEADS, GROUP, HEAD_DIM)
    oc_s, idx_s = _sample_cmp(q4, kc_planes_s, vc_rows_s, n_sub_s - 1, n_blk_s, t)
    nsel = min(N_SELECT, n_blk_s)
    gates_s = jnp.pad(ng_s[:, :3 * N_HEADS].reshape(db, N_KV_HEADS, GROUP, 3),
                      ((0, 0), (0, 0), (0, 0), (0, LANES - 3)))
    new3 = lambda x: x.reshape(db, N_KV_HEADS, HEAD_DIM)
    y_attn_s = _sample_sel(idx_s[:, :, :nsel], page_table, q4, oc_s, gates_s,
                           new3(ks_s), new3(vs_s), new3(kw_s), new3(vw_s),
                           to_planes(state_win_k[0]), to_planes(state_win_v[0]),
                           to_planes(cache_sel_k[0]), to_planes(cache_sel_v[0]), t).reshape(db, d)
    h2_s = _merge(h_s, y_conv_s, y_attn_s, mg_s, wa, wo, db, planes=False)
    y_sample = _ffn(h2_s, *ffn2, gfin=norm_final).reshape(db, dseq, d)

    wl_p = min(WINDOW, seq)
    p_conv = v_p.reshape(bsz, seq, d_conv)[:, seq - (CONV_WIDTH - 1):][None]
    new5 = lambda x: jnp.transpose(x.reshape(N_KV_HEADS, HEAD_DIM, db), (2, 0, 1)).reshape(1, db, 1, N_KV_HEADS, HEAD_DIM)
    wl_s = state_win_k.shape[2]
    row4 = lambda x: x.reshape(db, 1, N_KV_HEADS, HEAD_DIM)
    s_win_k = jnp.concatenate([state_win_k[0], row4(kw_s)], axis=1)[:, -wl_s:][None]
    s_win_v = jnp.concatenate([state_win_v[0], row4(vw_s)], axis=1)[:, -wl_s:][None]
    s_conv = jnp.concatenate([state_conv[0], v_s[:, None, :]], axis=1)[:, -(CONV_WIDTH - 1):][None]
    return (y_prompt, y_sample,
            state5(kct_p), state5(vct_p), state5(kst_p), state5(vst_p),
            state5(kwt_p[:, :, seq - wl_p:]), state5(vwt_p[:, :, seq - wl_p:]), p_conv,
            new5(kct_s), new5(vct_s), new5(kst_s), new5(vst_s),
            s_win_k, s_win_v, s_conv)
```

```python
import functools

import numpy as np
import jax
import jax.numpy as jnp
from jax import lax
from jax.experimental import pallas as pl
from jax.experimental.pallas import tpu as pltpu

N_HEADS = 16
N_KV_HEADS = 4
GROUP = N_HEADS // N_KV_HEADS
HEAD_DIM = 64
ROT_DIM = HEAD_DIM // 4
ROPE_THETA = 500000.0
CMP_LEN = 32
CMP_STRIDE = 16
SEL_BLOCK = 64
N_SELECT = 16
WINDOW = 512
Q_TILE = 128
K_TILE = 512
CMP_CHUNK = 256
PAGE_SIZE = 128
CONV_WIDTH = 31
CONV_HALO = 32
CONV_ROWS = 128
SAMPLE_BATCH = 8
EPS = 1e-6
NEG = -1e30
BIG = 1e6
REMOVED = -3e38
LOG2E = 1.4426950408889634
KV_W = N_KV_HEADS * HEAD_DIM
LANES = 128
SUBLANES = 8
SUB_PER_PAGE = PAGE_SIZE // CMP_STRIDE
CMP_ROW = CMP_STRIDE * KV_W
BLK_PER_TILE = K_TILE // SEL_BLOCK
BLK_ROWS = 16
V_ROWS = HEAD_DIM + 16

MXU_DTYPE = jnp.bfloat16
VMEM_LIMIT = 56 * 1024 * 1024
F32 = jnp.float32


def _mm(a, b):
    return jnp.dot(a.astype(MXU_DTYPE), b.astype(MXU_DTYPE), preferred_element_type=F32)


def _mm_nt(a, b):
    return lax.dot_general(a.astype(MXU_DTYPE), b.astype(MXU_DTYPE), (((1,), (1,)), ((), ())),
                           preferred_element_type=F32)


def _sigmoid(x):
    return 1.0 / (1.0 + jnp.exp(-x))


def _const_spec(shape):
    nd = len(shape)
    return pl.BlockSpec(shape, lambda *_: (0,) * nd, pipeline_mode=pl.Buffered(1))


def _params(*sem):
    return pltpu.CompilerParams(dimension_semantics=sem, vmem_limit_bytes=VMEM_LIMIT)


def _ff_chunk(f):
    best = LANES
    for k in range(1, f // LANES + 1):
        c = k * LANES
        if f % c == 0 and c <= 1408:
            best = c
    return best


def _ffn_kernel(*refs, fc, final):
    if final:
        x_ref, g_ref, wg_ref, wu_ref, wd_ref, gf_ref, o_ref = refs
    else:
        x_ref, g_ref, wg_ref, wu_ref, wd_ref, o_ref = refs
    x = x_ref[...]
    xn = x * lax.rsqrt(jnp.mean(x * x, -1, keepdims=True) + EPS) * g_ref[...]
    xb = xn.astype(MXU_DTYPE)
    acc = None
    for c in range(wg_ref.shape[1] // fc):
        sl = slice(c * fc, (c + 1) * fc)
        gate = _mm(xb, wg_ref[:, sl])
        up = _mm(xb, wu_ref[:, sl])
        d = _mm(gate * _sigmoid(gate) * up, wd_ref[sl, :])
        acc = d if acc is None else acc + d
    h = x + 0.5 * acc
    if final:
        h = h * lax.rsqrt(jnp.mean(h * h, -1, keepdims=True) + EPS) * gf_ref[...]
    o_ref[...] = h


def _ffn(x, g, wg, wu, wd, gfin=None, tm=512):
    r, d = x.shape
    f = wg.shape[1]
    tm = min(tm, r)
    row = pl.BlockSpec((tm, d), lambda i: (i, 0))
    in_specs = [row, _const_spec((1, d)), _const_spec((d, f)), _const_spec((d, f)), _const_spec((f, d))]
    args = [x, g.reshape(1, d), wg, wu, wd]
    if gfin is not None:
        in_specs.append(_const_spec((1, d)))
        args.append(gfin.reshape(1, d))
    return pl.pallas_call(
        functools.partial(_ffn_kernel, fc=_ff_chunk(f), final=gfin is not None),
        grid=(r // tm,), in_specs=in_specs, out_specs=row,
        out_shape=jax.ShapeDtypeStruct((r, d), F32),
        compiler_params=_params("parallel"), name="ffn")(*args)


def _rope_tables(pos):
    half = ROT_DIM // 2
    inv = ROPE_THETA ** (-jnp.arange(half, dtype=F32) / half)
    ang = pos.astype(F32)[:, None] * inv[None, :]
    cos, sin = jnp.cos(ang), jnp.sin(ang)
    t = pos.shape[0]
    rest = HEAD_DIM - ROT_DIM
    c = jnp.concatenate([cos, cos, jnp.ones((t, rest), F32)], 1)
    s1 = jnp.concatenate([jnp.zeros((t, half), F32), sin, jnp.zeros((t, rest), F32)], 1)
    s2 = jnp.concatenate([-sin, jnp.zeros((t, half + rest), F32)], 1)
    rep = LANES // HEAD_DIM
    return jnp.tile(c, (1, rep)), jnp.tile(s1, (1, rep)), jnp.tile(s2, (1, rep))


def _rope(z, c, s1, s2):
    out = []
    for j in range(z.shape[1] // LANES):
        x = z[:, j * LANES:(j + 1) * LANES]
        out.append(x * c + pltpu.roll(x, ROT_DIM // 2, 1) * s1 + pltpu.roll(x, LANES - ROT_DIM // 2, 1) * s2)
    return jnp.concatenate(out, axis=1)


def _head_tiles(z, extra=None):
    low = lax.broadcasted_iota(jnp.int32, (1, LANES), 1) < HEAD_DIM
    tiles = []
    for h in range(N_KV_HEADS):
        pair = z[:, (h // 2) * LANES:(h // 2 + 1) * LANES]
        if h % 2:
            pair = pltpu.roll(pair, HEAD_DIM, 1)
        tiles.append(jnp.where(low, pair, 0.0 if extra is None else extra))
    return tiles


def _inproj_kernel(h_ref, g_ref, w_ref, c_ref, s1_ref, s2_ref, *outs, offs, sample, qscale, nper):
    x = h_ref[...]
    tm = x.shape[0]
    xb = (x * lax.rsqrt(jnp.mean(x * x, -1, keepdims=True) + EPS) * g_ref[...]).astype(MXU_DTYPE)
    c, s1, s2 = c_ref[...], s1_ref[...], s2_ref[...]

    def seg(i):
        return _mm(xb, w_ref[:, offs[i]:offs[i + 1]])

    glu = seg(0)
    dc = glu.shape[1] // 2
    v = glu[:, :dc] * _sigmoid(glu[:, dc:])
    q = _rope(seg(1), c, s1, s2) * qscale
    kc, vc = seg(2), seg(3)
    ks, vs = _rope(seg(4), c, s1, s2), seg(5)
    kw, vw = _rope(seg(6), c, s1, s2), seg(7)
    ng = _sigmoid(seg(8))
    mg = _sigmoid(seg(9))
    if sample:
        (v_ref, q_ref, kct_ref, vct_ref, kst_ref, vst_ref, kwt_ref, vwt_ref,
         ks_ref, vs_ref, kw_ref, vw_ref, ng_ref, mg_ref) = outs
        q_ref[...] = q
        for ref, val in ((ks_ref, ks), (vs_ref, vs), (kw_ref, kw), (vw_ref, vw)):
            ref[...] = val
        ng_ref[...] = ng
    else:
        (v_ref, qt_ref, kct_ref, vct_ref, kst_ref, vst_ref, kwt_ref, vwt_ref,
         ksa_ref, kwa_ref, vstb_ref, vwtb_ref, ngt_ref, mg_ref) = outs
        qt_ref[...] = q.T.astype(MXU_DTYPE)
        pos = (pl.program_id(0) % nper) * tm + lax.broadcasted_iota(jnp.int32, (tm, 1), 0)
        blk_in_tile = lax.shift_right_logical(pos, SEL_BLOCK.bit_length() - 1) & (BLK_PER_TILE - 1)
        lane = lax.broadcasted_iota(jnp.int32, (1, LANES), 1)
        onehot = jnp.where(lane - HEAD_DIM == blk_in_tile, 1.0, 0.0)
        for h, tile in enumerate(_head_tiles(ks, onehot)):
            ksa_ref[h] = tile.astype(MXU_DTYPE)
        for h, tile in enumerate(_head_tiles(kw)):
            kwa_ref[h] = tile.astype(MXU_DTYPE)
        ngt_ref[...] = ng.T[0:3 * N_HEADS, :]
    v_ref[...] = v
    mg_ref[...] = mg
    kct_ref[...] = kc.T
    vct_ref[...] = vc.T
    kst_ref[...] = ks.T
    kwt_ref[...] = kw.T
    vst = vs.T
    vwt = vw.T
    vst_ref[...] = vst
    vwt_ref[...] = vwt
    if not sample:
        ones = jnp.where(lax.broadcasted_iota(jnp.int32, (V_ROWS - HEAD_DIM, tm), 0) == 0, 1.0, 0.0)
        for h in range(N_KV_HEADS):
            rs = slice(h * HEAD_DIM, (h + 1) * HEAD_DIM)
            vstb_ref[h] = jnp.concatenate([vst[rs], ones], axis=0).astype(MXU_DTYPE)
            vwtb_ref[h] = jnp.concatenate([vwt[rs], ones], axis=0).astype(MXU_DTYPE)


def _inproj(h, g, w_pad, offs, tables, nseq, rows_per_seq, sample, qscale, tm=512):
    r, d = h.shape
    tm = min(tm, rows_per_seq)
    nper = rows_per_seq // tm
    nat = lambda w, dt=F32: (pl.BlockSpec((tm, w), lambda i: (i, 0)), jax.ShapeDtypeStruct((r, w), dt))
    tr = lambda w, dt=F32: (pl.BlockSpec((None, w, tm), lambda i: (i // nper, 0, i % nper)),
                            jax.ShapeDtypeStruct((nseq, w, rows_per_seq), dt))
    aug = (pl.BlockSpec((None, N_KV_HEADS, tm, LANES), lambda i: (i // nper, 0, i % nper, 0)),
           jax.ShapeDtypeStruct((nseq, N_KV_HEADS, rows_per_seq, LANES), MXU_DTYPE))
    tab = pl.BlockSpec((tm, LANES), lambda i: (i % nper, 0))
    dc = (offs[1] - offs[0]) // 2
    dq = offs[2] - offs[1]
    planes = [tr(KV_W)] * 6
    if sample:
        outs = [nat(dc), nat(dq)] + planes + [nat(KV_W)] * 4 + [nat(LANES), nat(offs[10] - offs[9])]
    else:
        vplane = (pl.BlockSpec((None, N_KV_HEADS, V_ROWS, tm), lambda i: (i // nper, 0, 0, i % nper)),
                  jax.ShapeDtypeStruct((nseq, N_KV_HEADS, V_ROWS, rows_per_seq), MXU_DTYPE))
        outs = ([nat(dc), tr(dq, MXU_DTYPE)] + planes + [aug, aug, vplane, vplane,
                tr(3 * N_HEADS), nat(offs[10] - offs[9])])
    return pl.pallas_call(
        functools.partial(_inproj_kernel, offs=offs, sample=sample, qscale=qscale, nper=nper),
        grid=(r // tm,),
        in_specs=[pl.BlockSpec((tm, d), lambda i: (i, 0)), _const_spec((1, d)), _const_spec(w_pad.shape),
                  tab, tab, tab],
        out_specs=[o[0] for o in outs], out_shape=[o[1] for o in outs],
        compiler_params=_params("parallel"), name="inproj")(h, g.reshape(1, d), w_pad, *tables)


def _conv_tail(y, b_ref, lg_ref, lb_ref, wo_ref):
    y = y + b_ref[...]
    mu = jnp.mean(y, -1, keepdims=True)
    yc = y - mu
    var = jnp.mean(yc * yc, -1, keepdims=True)
    yn = yc * lax.rsqrt(var + EPS) * lg_ref[...] + lb_ref[...]
    return _mm(yn * _sigmoid(yn), wo_ref[...])


def _conv_kernel(main_ref, halo_ref, w_ref, b_ref, lg_ref, lb_ref, wo_ref, o_ref, ext_ref, sh_ref, y_ref):
    ts, c = main_ref.shape
    ext_ref[0:CONV_HALO, :] = jnp.where(pl.program_id(1) == 0, 0.0, halo_ref[...])
    ext_ref[CONV_HALO:CONV_HALO + ts, :] = main_ref[...]
    lead = CONV_HALO - (CONV_WIDTH - 1)
    n_sh = ts + CONV_HALO - SUBLANES
    for r in range(1, SUBLANES):
        sh_ref[r - 1] = ext_ref[r:r + n_sh, :]

    def tap_rows(k, r0, cs):
        off = lead + k
        r = off % SUBLANES
        base = r0 + off - r
        src = ext_ref if r == 0 else sh_ref.at[r - 1]
        return src[base:base + CONV_ROWS, cs]

    for cb in range(c // LANES):
        cs = slice(cb * LANES, (cb + 1) * LANES)
        taps = [w_ref[k:k + 1, cs] for k in range(CONV_WIDTH)]
        for r0 in range(0, ts, CONV_ROWS):
            acc = tap_rows(0, r0, cs) * taps[0]
            for k in range(1, CONV_WIDTH):
                acc = acc + tap_rows(k, r0, cs) * taps[k]
            y_ref[r0:r0 + CONV_ROWS, cs] = acc
    o_ref[...] = _conv_tail(y_ref[...], b_ref, lg_ref, lb_ref, wo_ref)


def _conv(v, w, b, lg, lb, wo, ts=512):
    bsz, l, c = v.shape
    d = wo.shape[1]
    step = ts // CONV_HALO
    vec = lambda a: a.reshape(1, c)
    return pl.pallas_call(
        _conv_kernel,
        grid=(bsz, l // ts),
        in_specs=[pl.BlockSpec((None, ts, c), lambda bi, i: (bi, i, 0)),
                  pl.BlockSpec((None, CONV_HALO, c), lambda bi, i: (bi, jnp.maximum(i * step - 1, 0), 0)),
                  _const_spec((CONV_WIDTH, c)), _const_spec((1, c)), _const_spec((1, c)), _const_spec((1, c)),
                  _const_spec((c, d))],
        out_specs=pl.BlockSpec((None, ts, d), lambda bi, i: (bi, i, 0)),
        out_shape=jax.ShapeDtypeStruct((bsz, l, d), F32),
        scratch_shapes=[pltpu.VMEM((ts + CONV_HALO, c), F32),
                        pltpu.VMEM((SUBLANES - 1, ts + CONV_HALO - SUBLANES, c), F32), pltpu.VMEM((ts, c), F32)],
        compiler_params=_params("parallel", "parallel"), name="conv")(v, v, w, vec(b), vec(lg), vec(lb), wo)


def _conv_sample_kernel(st_ref, v_ref, w_ref, b_ref, lg_ref, lb_ref, wo_ref, o_ref):
    acc = v_ref[...] * w_ref[CONV_WIDTH - 1:CONV_WIDTH, :]
    for k in range(CONV_WIDTH - 1):
        acc = acc + st_ref[k] * w_ref[k:k + 1, :]
    o_ref[...] = _conv_tail(acc, b_ref, lg_ref, lb_ref, wo_ref)


def _conv_sample(st, v, w, b, lg, lb, wo):
    rows, c = v.shape
    d = wo.shape[1]
    vec = lambda a: a.reshape(1, c)
    args = (st, v, w, vec(b), vec(lg), vec(lb), wo)
    return pl.pallas_call(
        _conv_sample_kernel, grid=(1,),
        in_specs=[_const_spec(a.shape) for a in args],
        out_specs=pl.BlockSpec((rows, d), lambda i: (0, 0)),
        out_shape=jax.ShapeDtypeStruct((rows, d), F32),
        compiler_params=_params("arbitrary"), name="conv_sample")(*args)


def _compress_kernel(pt_ref, x_hbm, w1_ref, pe_ref, w1f_ref, b1_ref, w2_ref, w2h_ref, nat_ref, tr_ref,
                     buf, tbuf, fsbuf, sem, *, pc, n_pages, paged):
    s = pl.program_id(0)
    ci = pl.program_id(1)
    n_chunks = pl.num_programs(1)
    step = s * n_chunks + ci
    slot = step % 2
    m = pc * SUB_PER_PAGE
    nrow = m + SUB_PER_PAGE
    pairs = N_KV_HEADS // 2

    def page_copy(s_, ci_, slot_, p):
        pg = pt_ref[s_ * n_pages + jnp.minimum(ci_ * pc + p, n_pages - 1)]
        if paged:
            src = x_hbm.at[pg]
        else:
            src = x_hbm.at[s_, :, :, pl.ds(pl.multiple_of(pg * PAGE_SIZE, PAGE_SIZE), PAGE_SIZE)]
        return pltpu.make_async_copy(src, buf.at[slot_, p], sem.at[slot_])

    @pl.when(step == 0)
    def _():
        for p in range(pc + 1):
            page_copy(s, ci, slot, p).start()

    @pl.when(step + 1 < pl.num_programs(0) * n_chunks)
    def _():
        wrap = ci + 1 == n_chunks
        for p in range(pc + 1):
            page_copy(jnp.where(wrap, s + 1, s), jnp.where(wrap, 0, ci + 1), 1 - slot, p).start()

    for p in range(pc + 1):
        page_copy(s, ci, slot, p).wait()

    for p in range(pc + 1):
        for j in range(pairs):
            plane = buf[slot, p, 2 * j:2 * j + 2].reshape(2 * HEAD_DIM, PAGE_SIZE)
            tbuf[j, p * PAGE_SIZE:(p + 1) * PAGE_SIZE, :] = plane.T
    for j in range(pairs):
        acc = None
        for lp in range(CMP_STRIDE // 2):
            lhs = jnp.concatenate([tbuf[j, pl.ds(2 * lp + i, nrow, stride=CMP_STRIDE), :] for i in range(2)], axis=1)
            d = _mm(lhs, w1_ref[lp])
            acc = d if acc is None else acc + d
        fsbuf[j] = acc
    c = _mm(pe_ref[...], w1f_ref[...])[0:1, :] + b1_ref[...]
    nh = c.shape[1]
    parts = []
    for h in range(N_KV_HEADS):
        col = (h % 2) * 2 * nh
        parts.append(fsbuf[h // 2, 0:m, col:col + nh] + fsbuf[h // 2, pl.ds(1, m), col + nh:col + 2 * nh] + c)
    hid = jnp.concatenate(parts, axis=1)
    act = (hid * _sigmoid(hid)).astype(MXU_DTYPE)
    tr_ref[...] = _mm(act, w2_ref[...]).T.astype(MXU_DTYPE)
    for h in range(N_KV_HEADS):
        nat_ref[h] = _mm(act, w2h_ref[h]).astype(MXU_DTYPE)


def _compress(x, page_table, weights, pc, paged):
    w1_big, pe8, w1_flat, b1, w2_big, w2_heads = weights
    n_seq, n_pages = page_table.shape
    m = pc * SUB_PER_PAGE
    n_sub = n_pages * SUB_PER_PAGE
    hidden = b1.shape[0]
    grid_spec = pltpu.PrefetchScalarGridSpec(
        num_scalar_prefetch=1,
        grid=(n_seq, n_pages // pc),
        in_specs=[pl.BlockSpec(memory_space=pl.ANY),
                  _const_spec(w1_big.shape), _const_spec(pe8.shape), _const_spec(w1_flat.shape),
                  _const_spec((1, hidden)), _const_spec(w2_big.shape), _const_spec(w2_heads.shape)],
        out_specs=[pl.BlockSpec((None, N_KV_HEADS, m, LANES), lambda s, ci, pt: (s, 0, ci, 0)),
                   pl.BlockSpec((None, KV_W, m), lambda s, ci, pt: (s, 0, ci))],
        scratch_shapes=[pltpu.VMEM((2, pc + 1, N_KV_HEADS, HEAD_DIM, PAGE_SIZE), F32),
                        pltpu.VMEM((N_KV_HEADS // 2, (pc + 1) * PAGE_SIZE, LANES), F32),
                        pltpu.VMEM((N_KV_HEADS // 2, m + SUB_PER_PAGE, w1_big.shape[2]), F32),
                        pltpu.SemaphoreType.DMA((2,))])
    return pl.pallas_call(
        functools.partial(_compress_kernel, pc=pc, n_pages=n_pages, paged=paged),
        grid_spec=grid_spec,
        out_shape=[jax.ShapeDtypeStruct((n_seq, N_KV_HEADS, n_sub, LANES), MXU_DTYPE),
                   jax.ShapeDtypeStruct((n_seq, KV_W, n_sub), MXU_DTYPE)],
        compiler_params=_params("arbitrary", "arbitrary"), name="compress")(
            page_table.reshape(-1), x, w1_big, pe8, w1_flat, b1.reshape(1, hidden), w2_big, w2_heads)


def _compress_weights(pe, w1, b1, w2):
    hidden = w1.shape[-1]
    eye = jnp.eye(N_KV_HEADS, dtype=F32)
    both = jnp.concatenate([w1[:CMP_STRIDE], w1[CMP_STRIDE:]], axis=-1)
    both = both.reshape(CMP_STRIDE // 2, 2, HEAD_DIM, 2 * hidden)
    w1_big = jnp.einsum('pidc,hg->pihdgc', both, jnp.eye(2, dtype=F32)).reshape(
        CMP_STRIDE // 2, 4 * HEAD_DIM, 4 * hidden)
    w2_big = jnp.einsum('kd,hg->hkgd', w2, eye).reshape(N_KV_HEADS * hidden, KV_W)
    w2_pad = jnp.pad(w2, ((0, 0), (0, LANES - HEAD_DIM)))
    w2_heads = jnp.einsum('kd,hg->ghkd', w2_pad, eye).reshape(N_KV_HEADS, N_KV_HEADS * hidden, LANES)
    pe8 = jnp.zeros((SUBLANES, CMP_LEN * HEAD_DIM), F32).at[0].set(pe.reshape(-1))
    return (w1_big.astype(MXU_DTYPE), pe8, w1.reshape(CMP_LEN * HEAD_DIM, hidden).astype(MXU_DTYPE), b1,
            w2_big.astype(MXU_DTYPE), w2_heads.astype(MXU_DTYPE))


def _overlap(n_cmp_pad, n_blk_pad):
    ci = np.arange(n_cmp_pad)[:, None] * CMP_STRIDE
    bs = np.arange(n_blk_pad)[None, :] * SEL_BLOCK
    return ((ci < bs + SEL_BLOCK) & (ci + CMP_LEN > bs)).astype(np.float32)


def _attn_kernel(q_ref, g_ref, kc_ref, vct_ref, ov_ref, ks_ref, vst_ref, kw_ref, vwt_ref, o_ref,
                 bias_ref, sc_ref, imp_ref, sa_ref, sb_ref, *, n_cmp, n_blk):
    s0 = pl.program_id(2) * Q_TILE
    cols = GROUP * Q_TILE
    qt = jnp.concatenate([q_ref[g] for g in range(GROUP)], axis=1)
    q0 = jnp.concatenate([qt, jnp.zeros((LANES - HEAD_DIM, cols), qt.dtype)], axis=0)
    tq = s0 + (lax.broadcasted_iota(jnp.int32, (1, cols), 1) & (Q_TILE - 1))
    tq1 = s0 + lax.broadcasted_iota(jnp.int32, (1, Q_TILE), 1)

    ncp = kc_ref.shape[0]
    ch = min(CMP_CHUNK, ncp)
    last_c = lax.shift_right_logical(s0 + (Q_TILE - CMP_LEN), CMP_STRIDE.bit_length() - 1)
    n_cc = jnp.minimum(last_c // ch + 1, ncp // ch)
    ci0 = lax.broadcasted_iota(jnp.int32, (ch, 1), 0)

    def cmp_scores(cc, m):
        r0 = pl.multiple_of(cc * ch, ch)
        c = ci0 + r0
        s = jnp.where((c * CMP_STRIDE + (CMP_LEN - 1) <= tq) & (c < n_cmp), _mm(kc_ref[pl.ds(r0, ch), :], q0), NEG)
        sc_ref[pl.ds(r0, ch), :] = s
        return jnp.maximum(m, jnp.max(s, axis=0, keepdims=True))

    m_c = lax.fori_loop(0, n_cc, cmp_scores, jnp.full((1, cols), NEG, F32))
    imp_ref[...] = jnp.zeros(imp_ref.shape, F32)

    def cmp_weights(cc, carry):
        l, acc = carry
        r0 = pl.multiple_of(cc * ch, ch)
        p = jnp.exp2(sc_ref[pl.ds(r0, ch), :] - m_c)
        pb = p.astype(MXU_DTYPE)
        imp_ref[...] += _mm(ov_ref[:, pl.ds(r0, ch)], pb)
        return l + jnp.sum(p, axis=0, keepdims=True), acc + _mm(vct_ref[:, pl.ds(r0, ch)], pb)

    l_c, acc_c = lax.fori_loop(0, n_cc, cmp_weights,
                               (jnp.zeros((1, cols), F32), jnp.zeros((HEAD_DIM, cols), F32)))
    inv_c = jnp.where(m_c > 0.5 * NEG, 1.0 / jnp.maximum(l_c, 1e-30), 0.0)
    o_c = acc_c * inv_c
    impn = imp_ref[...] * inv_c
    imp = impn[:, 0:Q_TILE]
    for g in range(1, GROUP):
        imp = imp + impn[:, g * Q_TILE:(g + 1) * Q_TILE]

    wk = WINDOW + Q_TILE
    w0 = pl.multiple_of(jnp.maximum(s0 - WINDOW, 0), Q_TILE)
    wpos = w0 + lax.broadcasted_iota(jnp.int32, (wk, 1), 0)
    sw = jnp.where((wpos <= tq) & (wpos > tq - WINDOW), _mm(kw_ref[pl.ds(w0, wk), :], q0), NEG)
    acc_w = _mm(vwt_ref[:, pl.ds(w0, wk)], jnp.exp2(sw - jnp.max(sw, axis=0, keepdims=True)))
    o_w = acc_w[0:HEAD_DIM] / acc_w[HEAD_DIM:HEAD_DIM + 1]

    blk = lax.broadcasted_iota(jnp.int32, (n_blk, 1), 0)
    blk_f = blk.astype(F32)
    cur = lax.shift_right_logical(tq1, SEL_BLOCK.bit_length() - 1)
    forced = (blk == 0) | (blk == cur) | (blk == cur - 1)
    valid = blk * SEL_BLOCK <= tq1
    score = jnp.where(valid, imp + jnp.where(forced, BIG, 0.0), -BIG)
    for _ in range(min(N_SELECT, n_blk)):
        top = jnp.max(score, axis=0, keepdims=True)
        first = jnp.min(jnp.where(score == top, blk_f, float(n_blk)), axis=0, keepdims=True)
        score = jnp.where(blk_f == first, REMOVED, score)
    bias_ref[...] = jnp.where((score == REMOVED) & valid, 0.0, NEG)

    kpos0 = lax.broadcasted_iota(jnp.int32, (K_TILE, 1), 0)
    q_pad = jnp.zeros((LANES - HEAD_DIM - BLK_ROWS, cols), qt.dtype)
    b_pad = jnp.zeros((BLK_ROWS - BLK_PER_TILE, cols), F32)

    def sel_scores(j, s_ref):
        k0 = pl.multiple_of(j * K_TILE, K_TILE)
        b8 = bias_ref[pl.ds(pl.multiple_of(j * BLK_PER_TILE, BLK_PER_TILE), BLK_PER_TILE), :]
        rows = jnp.concatenate([jnp.concatenate([b8] * GROUP, axis=1), b_pad], axis=0).astype(qt.dtype)
        s_ref[...] = _mm(ks_ref[pl.ds(k0, K_TILE), :], jnp.concatenate([qt, rows, q_pad], axis=0))

    def sel_update(j, s, carry):
        m, acc = carry
        m_new = jnp.maximum(m, jnp.max(s, axis=0, keepdims=True))
        p = jnp.exp2(s - m_new)
        acc = jnp.exp2(m - m_new) * acc + _mm(vst_ref[:, pl.ds(pl.multiple_of(j * K_TILE, K_TILE), K_TILE)], p)
        return m_new, acc

    def sel_pair(i, carry):
        sel_scores(2 * i + 1, sb_ref)
        carry = sel_update(2 * i, sa_ref[...], carry)
        sel_scores(2 * i + 2, sa_ref)
        return sel_update(2 * i + 1, sb_ref[...], carry)

    last = (s0 + Q_TILE - 1) // K_TILE
    n_pairs = last // 2
    sel_scores(0, sa_ref)
    carry = lax.fori_loop(0, n_pairs, sel_pair, (jnp.full((1, cols), NEG, F32), jnp.zeros((V_ROWS, cols), F32)))
    ta = 2 * n_pairs

    def sel_tail(t, s_ref, c):
        return sel_update(t, jnp.where(kpos0 + t * K_TILE <= tq, s_ref[...], NEG), c)

    def two_left(c):
        sel_scores(last, sb_ref)
        return sel_tail(last, sb_ref, sel_tail(ta, sa_ref, c))

    _, acc_s = lax.cond(ta < last, two_left, lambda c: sel_tail(ta, sa_ref, c), carry)
    o_s = acc_s[0:HEAD_DIM] / acc_s[HEAD_DIM:HEAD_DIM + 1]

    gate = lambda br: jnp.concatenate([g_ref[g * 3 + br:g * 3 + br + 1, :] for g in range(GROUP)], axis=1)
    o = gate(0) * o_c + gate(1) * o_s + gate(2) * o_w
    for g in range(GROUP):
        o_ref[g] = o[:, g * Q_TILE:(g + 1) * Q_TILE]


def _attn_prompt(qt, gt, kc, vct, ks, vst, kw, vwt, n_cmp):
    b, h, grp, hd, s = qt.shape
    ncp = kc.shape[2]
    n_blk = s // SEL_BLOCK
    nt = s // Q_TILE
    cols = grp * Q_TILE
    ov = jnp.asarray(_overlap(ncp, n_blk).T, MXU_DTYPE)
    one = pl.Buffered(1)
    rows = lambda n: pl.BlockSpec((None, None, n, LANES), lambda bi, hi, t: (bi, hi, 0, 0), pipeline_mode=one)
    plane = lambda n: pl.BlockSpec((None, hd, n), lambda bi, hi, t: (bi, hi, 0), pipeline_mode=one)
    vplane = pl.BlockSpec((None, None, V_ROWS, s), lambda bi, hi, t: (bi, hi, 0, 0), pipeline_mode=one)
    qspec = pl.BlockSpec((None, None, grp, hd, Q_TILE), lambda bi, hi, t: (bi, hi, 0, 0, t))
    return pl.pallas_call(
        functools.partial(_attn_kernel, n_cmp=n_cmp, n_blk=n_blk),
        grid=(b, h, nt),
        in_specs=[qspec, pl.BlockSpec((None, None, 3 * grp, Q_TILE), lambda bi, hi, t: (bi, hi, 0, t)),
                  rows(ncp), plane(ncp), _const_spec(ov.shape), rows(s), vplane, rows(s), vplane],
        out_specs=qspec,
        out_shape=jax.ShapeDtypeStruct((b, h, grp, hd, s), F32),
        scratch_shapes=[pltpu.VMEM((n_blk, Q_TILE), F32), pltpu.VMEM((ncp, cols), F32),
                        pltpu.VMEM((n_blk, cols), F32), pltpu.VMEM((K_TILE, cols), F32),
                        pltpu.VMEM((K_TILE, cols), F32)],
        compiler_params=_params("parallel", "parallel", "arbitrary"), name="attn_prompt")(
            qt, gt, kc, vct, ov, ks, vst, kw, vwt)


def _sample_cmp_kernel(q_ref, kct_ref, vc_ref, ov_ref, oc_ref, idx_ref, *, n_cmp, n_blk, t):
    nb = q_ref.shape[0]
    ncp = kct_ref.shape[2]
    nbp = ov_ref.shape[1]
    ci = lax.broadcasted_iota(jnp.int32, (1, ncp), 1)
    mask = (ci * CMP_STRIDE + (CMP_LEN - 1) <= t) & (ci < n_cmp)
    sums = []
    for b in range(nb):
        for h in range(N_KV_HEADS):
            s = jnp.where(mask, _mm(q_ref[b, h], kct_ref[b, h * HEAD_DIM:(h + 1) * HEAD_DIM, :]), NEG)
            m = jnp.max(s, axis=1, keepdims=True)
            p = jnp.where(mask, jnp.exp(s - m), 0.0)
            p = p / jnp.maximum(jnp.sum(p, axis=1, keepdims=True), 1e-30)
            oc_ref[b, h] = _mm(p, vc_ref[b, h])
            sums.append(jnp.sum(p, axis=0, keepdims=True))
    imp = _mm(jnp.concatenate(sums, axis=0), ov_ref[...])
    blk = lax.broadcasted_iota(jnp.int32, (1, nbp), 1)
    cur = t // SEL_BLOCK
    forced = (blk == 0) | (blk == cur) | (blk == cur - 1)
    score = jnp.where(blk * SEL_BLOCK <= t, imp + jnp.where(forced, BIG, 0.0), -BIG)
    score = jnp.where(blk < n_blk, score, REMOVED)
    blk_f = blk.astype(F32)
    lane = lax.broadcasted_iota(jnp.int32, (1, LANES), 1)
    out = jnp.zeros((nb * N_KV_HEADS, LANES), F32)
    for r in range(min(N_SELECT, n_blk)):
        top = jnp.max(score, axis=1, keepdims=True)
        first = jnp.min(jnp.where(score == top, blk_f, float(nbp)), axis=1, keepdims=True)
        out = jnp.where(lane == r, first, out)
        score = jnp.where(blk_f == first, REMOVED, score)
    idx_ref[...] = out.astype(jnp.int32)


def _sample_cmp(q4, kct, vc, n_cmp, n_blk, t):
    db, _, _, hd = q4.shape
    ncp = kct.shape[2]
    nbp = -(-n_blk // LANES) * LANES
    nb = SAMPLE_BATCH if db % SAMPLE_BATCH == 0 else 1
    ov = jnp.asarray(_overlap(ncp, nbp), MXU_DTYPE)
    per = lambda r, w: pl.BlockSpec((nb, N_KV_HEADS, r, w), lambda b: (b, 0, 0, 0))
    oc, idx = pl.pallas_call(
        functools.partial(_sample_cmp_kernel, n_cmp=n_cmp, n_blk=n_blk, t=t),
        grid=(db // nb,),
        in_specs=[per(GROUP, hd), pl.BlockSpec((nb, KV_W, ncp), lambda b: (b, 0, 0)), per(ncp, LANES),
                  _const_spec(ov.shape)],
        out_specs=[per(GROUP, LANES), pl.BlockSpec((None, nb * N_KV_HEADS, LANES), lambda b: (b, 0, 0))],
        out_shape=[jax.ShapeDtypeStruct((db, N_KV_HEADS, GROUP, LANES), F32),
                   jax.ShapeDtypeStruct((db // nb, nb * N_KV_HEADS, LANES), jnp.int32)],
        compiler_params=_params("parallel"), name="sample_cmp")(q4, kct, vc, ov)
    return oc, idx.reshape(db, N_KV_HEADS, LANES)


def _sample_sel_kernel(idx_ref, pt_ref, q_ref, oc_ref, g_ref, ksn_ref, vsn_ref, kwn_ref, vwn_ref, wk_ref, wv_ref,
                       pk_hbm, pv_hbm, o_ref, kbuf, vbuf, sem, *, n_past_blk, n_pages, nsel, t):
    b = pl.program_id(0)
    sub = PAGE_SIZE // SEL_BLOCK

    def sel_block(h, n):
        return idx_ref[(b * N_KV_HEADS + h) * nsel + n]

    def page_copies(h, n):
        pg = pt_ref[b * n_pages + jnp.minimum(sel_block(h, n), n_past_blk - 1) // sub]
        dst = pl.ds(n * PAGE_SIZE, PAGE_SIZE)
        return (pltpu.make_async_copy(pk_hbm.at[pg, h], kbuf.at[h, :, dst], sem.at[0]),
                pltpu.make_async_copy(pv_hbm.at[pg, h], vbuf.at[h, :, dst], sem.at[1]))

    for h in range(N_KV_HEADS):
        for n in range(nsel):
            for cp in page_copies(h, n):
                cp.start()
    for h in range(N_KV_HEADS):
        for n in range(nsel):
            for cp in page_copies(h, n):
                cp.wait()

    lane = lax.broadcasted_iota(jnp.int32, (1, nsel * PAGE_SIZE), 1)
    seg = lax.shift_right_logical(lane, PAGE_SIZE.bit_length() - 1)
    half = lax.shift_right_logical(lane, SEL_BLOCK.bit_length() - 1) & (sub - 1)
    wl = wk_ref.shape[2]
    wpos = t - wl + lax.broadcasted_iota(jnp.int32, (1, wl), 1)
    okw = (wpos > t - WINDOW) & (wpos >= 0)
    for h in range(N_KV_HEADS):
        q = q_ref[h]
        want = jnp.full(lane.shape, -1, jnp.int32)
        for n in range(nsel):
            blk = sel_block(h, n)
            want = jnp.where(seg == n, jnp.where(blk < n_past_blk, blk % sub, -1), want)
        ok = half == want
        s = jnp.where(ok, _mm(q, kbuf[h]), NEG)
        s_new = jnp.sum(q * ksn_ref[h:h + 1, :], axis=1, keepdims=True)
        m = jnp.maximum(jnp.max(s, axis=1, keepdims=True), s_new)
        p = jnp.where(ok, jnp.exp(s - m), 0.0)
        p_new = jnp.exp(s_new - m)
        o_s = (_mm_nt(p, vbuf[h]) + p_new * vsn_ref[h:h + 1, :]) / (jnp.sum(p, axis=1, keepdims=True) + p_new)
        sw = jnp.where(okw, _mm(q, wk_ref[h]), NEG)
        sw_new = jnp.sum(q * kwn_ref[h:h + 1, :], axis=1, keepdims=True)
        mw = jnp.maximum(jnp.max(sw, axis=1, keepdims=True), sw_new)
        pw = jnp.where(okw, jnp.exp(sw - mw), 0.0)
        pw_new = jnp.exp(sw_new - mw)
        o_w = (_mm_nt(pw, wv_ref[h]) + pw_new * vwn_ref[h:h + 1, :]) / (jnp.sum(pw, axis=1, keepdims=True) + pw_new)
        g = g_ref[h]
        o_ref[h] = g[:, 0:1] * oc_ref[h][:, 0:HEAD_DIM] + g[:, 1:2] * o_s + g[:, 2:3] * o_w


def _sample_sel(idx, page_table, q4, oc, gates, ks_new, vs_new, kw_new, vw_new, win_k, win_v, pool_k, pool_v, t):
    db, _, grp, hd = q4.shape
    nsel = idx.shape[2]
    n_pages = page_table.shape[1]
    n_past_blk = n_pages * (PAGE_SIZE // SEL_BLOCK)
    wl = win_k.shape[3]
    per = lambda *shape: pl.BlockSpec((None,) + shape, lambda b, *_: (b,) + (0,) * len(shape))
    hbm = pl.BlockSpec(memory_space=pl.ANY)
    grid_spec = pltpu.PrefetchScalarGridSpec(
        num_scalar_prefetch=2,
        grid=(db,),
        in_specs=[per(N_KV_HEADS, grp, hd), per(N_KV_HEADS, grp, LANES), per(N_KV_HEADS, grp, LANES)]
                 + [per(N_KV_HEADS, hd)] * 4 + [per(N_KV_HEADS, hd, wl)] * 2 + [hbm, hbm],
        out_specs=per(N_KV_HEADS, grp, hd),
        scratch_shapes=[pltpu.VMEM((N_KV_HEADS, hd, nsel * PAGE_SIZE), F32),
                        pltpu.VMEM((N_KV_HEADS, hd, nsel * PAGE_SIZE), F32),
                        pltpu.SemaphoreType.DMA((2,))])
    return pl.pallas_call(
        functools.partial(_sample_sel_kernel, n_past_blk=n_past_blk, n_pages=n_pages, nsel=nsel, t=t),
        grid_spec=grid_spec,
        out_shape=jax.ShapeDtypeStruct((db, N_KV_HEADS, grp, hd), F32),
        compiler_params=_params("arbitrary"), name="sample_sel")(
            idx.reshape(-1), page_table.reshape(-1), q4, oc, gates, ks_new, vs_new, kw_new, vw_new,
            win_k, win_v, pool_k, pool_v)


def _merge_kernel(h_ref, yc_ref, ya_ref, mg_ref, wa_ref, wo_ref, o_ref, *, planes):
    d = h_ref.shape[1]
    ya = ya_ref[...].T if planes else ya_ref[...]
    m = mg_ref[:, :d] * yc_ref[...] + mg_ref[:, d:] * _mm(ya, wa_ref[...])
    o_ref[...] = h_ref[...] + _mm(m, wo_ref[...])


def _merge(h, y_conv, y_attn, mg, wa, wo, rows_per_seq, planes, tm=512):
    r, d = h.shape
    tm = min(tm, rows_per_seq)
    nper = rows_per_seq // tm
    row = lambda w: pl.BlockSpec((tm, w), lambda i: (i, 0))
    da = wa.shape[0]
    ya_spec = pl.BlockSpec((None, da, tm), lambda i: (i // nper, 0, i % nper)) if planes else row(da)
    return pl.pallas_call(
        functools.partial(_merge_kernel, planes=planes), grid=(r // tm,),
        in_specs=[row(d), row(d), ya_spec, row(2 * d), _const_spec(wa.shape), _const_spec(wo.shape)],
        out_specs=row(d), out_shape=jax.ShapeDtypeStruct((r, d), F32),
        compiler_params=_params("parallel"), name="merge")(h, y_conv, y_attn, mg, wa, wo)


def kernel(x_prompt, x_sample, cache_cmp_k, cache_cmp_v, cache_sel_k, cache_sel_v, state_win_k, state_win_v, state_conv, page_table, norm_ffn1, ffn1_gate, ffn1_up, ffn1_down, norm_mix, w_in, conv_w, conv_b, conv_ln_g, conv_ln_b, w_conv_out, cmp_pe_k, cmp_w1_k, cmp_b1_k, cmp_w2_k, cmp_pe_v, cmp_w1_v, cmp_b1_v, cmp_w2_v, w_attn_out, w_out, norm_ffn2, ffn2_gate, ffn2_up, ffn2_down, norm_final):
    bsz, seq, d = x_prompt.shape
    db, dseq, _ = x_sample.shape
    depth = norm_ffn1.shape[0]
    assert depth == 1 and dseq == 1 and d == N_HEADS * HEAD_DIM
    assert seq % K_TILE == 0 and seq >= WINDOW + Q_TILE
    assert state_conv.shape[2] == CONV_WIDTH - 1
    n_pages = page_table.shape[1]
    past = n_pages * PAGE_SIZE
    d_conv = d // 2
    scale = HEAD_DIM ** -0.5
    cast = lambda w: w.astype(MXU_DTYPE)

    col_sizes = (2 * d_conv, N_HEADS * HEAD_DIM) + (KV_W,) * 6 + (3 * N_HEADS, 2 * d)
    starts = np.concatenate([[0], np.cumsum(col_sizes)])
    offs, pieces, pos = [0], [], 0
    for i, n in enumerate(col_sizes):
        pad = -n % LANES
        pieces.append(w_in[0][:, starts[i]:starts[i + 1]])
        if pad:
            pieces.append(jnp.zeros((d, pad), F32))
        pos += n + pad
        offs.append(pos)
    w_in_pad = cast(jnp.concatenate(pieces, axis=1))
    offs = tuple(offs)
    ffn1 = (norm_ffn1[0], cast(ffn1_gate[0]), cast(ffn1_up[0]), cast(ffn1_down[0]))
    ffn2 = (norm_ffn2[0], cast(ffn2_gate[0]), cast(ffn2_up[0]), cast(ffn2_down[0]))
    wa, wo, wco = cast(w_attn_out[0]), cast(w_out[0]), cast(w_conv_out[0])
    convp = (conv_w[0], conv_b[0], conv_ln_g[0], conv_ln_b[0], wco)
    cmpk = _compress_weights(cmp_pe_k[0], cmp_w1_k[0], cmp_b1_k[0], cmp_w2_k[0])
    cmpv = _compress_weights(cmp_pe_v[0], cmp_w1_v[0], cmp_b1_v[0], cmp_w2_v[0])
    planes4 = lambda x: x.reshape(x.shape[0], N_KV_HEADS, HEAD_DIM, x.shape[2])
    state5 = lambda x: jnp.transpose(planes4(x), (0, 3, 1, 2))[None]
    to_planes = lambda x: jnp.transpose(x, (0, 2, 3, 1))

    rp = bsz * seq
    h_p = _ffn(x_prompt.reshape(rp, d), *ffn1)
    (v_p, qt_p, kct_p, vct_p, kst_p, vst_p, kwt_p, vwt_p, ksa_p, kwa_p, vstb_p, vwtb_p, ngt_p, mg_p) = _inproj(
        h_p, norm_mix[0], w_in_pad, offs, _rope_tables(jnp.arange(seq, dtype=jnp.int32)), bsz, seq,
        sample=False, qscale=scale * LOG2E)
    y_conv_p = _conv(v_p.reshape(bsz, seq, d_conv), *convp).reshape(rp, d)
    pages_p = seq // PAGE_SIZE
    pc_p = min(32, pages_p)
    ident = jnp.tile(jnp.arange(pages_p, dtype=jnp.int32)[None], (bsz, 1))
    kc_rows, _ = _compress(planes4(kct_p), ident, cmpk, pc_p, paged=False)
    _, vc_planes = _compress(planes4(vct_p), ident, cmpv, pc_p, paged=False)
    ot = _attn_prompt(qt_p.reshape(bsz, N_KV_HEADS, GROUP, HEAD_DIM, seq),
                      ngt_p.reshape(bsz, N_KV_HEADS, 3 * GROUP, seq),
                      kc_rows, vc_planes, ksa_p, vstb_p, kwa_p, vwtb_p, seq // CMP_STRIDE - 1)
    h2_p = _merge(h_p, y_conv_p, ot.reshape(bsz, d, seq), mg_p, wa, wo, seq, planes=True)
    y_prompt = _ffn(h2_p, *ffn2, gfin=norm_final).reshape(bsz, seq, d)

    t = past
    h_s = _ffn(x_sample.reshape(db, d), *ffn1)
    (v_s, q_s, kct_s, vct_s, kst_s, vst_s, kwt_s, vwt_s, ks_s, vs_s, kw_s, vw_s, ng_s, mg_s) = _inproj(
        h_s, norm_mix[0], w_in_pad, offs, _rope_tables(jnp.full((db,), t, jnp.int32)), 1, db,
        sample=True, qscale=scale)
    y_conv_s = _conv_sample(jnp.transpose(state_conv[0], (1, 0, 2)), v_s, *convp)
    pc_s = min(32, n_pages)
    n_sub_s = (past + dseq) // CMP_STRIDE
    assert n_sub_s == n_pages * SUB_PER_PAGE
    _, kc_planes_s = _compress(to_planes(cache_cmp_k[0]), page_table, cmpk, pc_s, paged=True)
    vc_rows_s, _ = _compress(to_planes(cache_cmp_v[0]), page_table, cmpv, pc_s, paged=True)
    n_blk_s = -(-(past + dseq) // SEL_BLOCK)
    q4 = q_s.reshape(db, N_KV_HEADS, GROUP, HEAD_DIM)
    oc_s, idx_s = _sample_cmp(q4, kc_planes_s, vc_rows_s, n_sub_s - 1, n_blk_s, t)
    nsel = min(N_SELECT, n_blk_s)
    gates_s = jnp.pad(ng_s[:, :3 * N_HEADS].reshape(db, N_KV_HEADS, GROUP, 3),
                      ((0, 0), (0, 0), (0, 0), (0, LANES - 3)))
    new3 = lambda x: x.reshape(db, N_KV_HEADS, HEAD_DIM)
    y_attn_s = _sample_sel(idx_s[:, :, :nsel], page_table, q4, oc_s, gates_s,
                           new3(ks_s), new3(vs_s), new3(kw_s), new3(vw_s),
                           to_planes(state_win_k[0]), to_planes(state_win_v[0]),
                           to_planes(cache_sel_k[0]), to_planes(cache_sel_v[0]), t).reshape(db, d)
    h2_s = _merge(h_s, y_conv_s, y_attn_s, mg_s, wa, wo, db, planes=False)
    y_sample = _ffn(h2_s, *ffn2, gfin=norm_final).reshape(db, dseq, d)

    wl_p = min(WINDOW, seq)
    p_conv = v_p.reshape(bsz, seq, d_conv)[:, seq - (CONV_WIDTH - 1):][None]
    new5 = lambda x: jnp.transpose(x.reshape(N_KV_HEADS, HEAD_DIM, db), (2, 0, 1)).reshape(1, db, 1, N_KV_HEADS, HEAD_DIM)
    wl_s = state_win_k.shape[2]
    row4 = lambda x: x.reshape(db, 1, N_KV_HEADS, HEAD_DIM)
    s_win_k = jnp.concatenate([state_win_k[0], row4(kw_s)], axis=1)[:, -wl_s:][None]
    s_win_v = jnp.concatenate([state_win_v[0], row4(vw_s)], axis=1)[:, -wl_s:][None]
    s_conv = jnp.concatenate([state_conv[0], v_s[:, None, :]], axis=1)[:, -(CONV_WIDTH - 1):][None]
    return (y_prompt, y_sample,
            state5(kct_p), state5(vct_p), state5(kst_p), state5(vst_p),
            state5(kwt_p[:, :, seq - wl_p:]), state5(vwt_p[:, :, seq - wl_p:]), p_conv,
            new5(kct_s), new5(vct_s), new5(kst_s), new5(vst_s),
            s_win_k, s_win_v, s_conv)
```

```python
import functools

import numpy as np
import jax
import jax.numpy as jnp
from jax import lax
from jax.experimental import pallas as pl
from jax.experimental.pallas import tpu as pltpu

N_HEADS = 16
N_KV_HEADS = 4
GROUP = N_HEADS // N_KV_HEADS
HEAD_DIM = 64
ROT_DIM = HEAD_DIM // 4
ROPE_THETA = 500000.0
CMP_LEN = 32
CMP_STRIDE = 16
SEL_BLOCK = 64
N_SELECT = 16
WINDOW = 512
Q_TILE = 256
K_TILE = 512
CMP_CHUNK = 512
PAGE_SIZE = 128
CONV_WIDTH = 31
CONV_HALO = 32
CONV_ROWS = 128
SAMPLE_BATCH = 8
EPS = 1e-6
NEG = -1e30
BIG = 1e6
REMOVED = -3e38
LOG2E = 1.4426950408889634
KV_W = N_KV_HEADS * HEAD_DIM
LANES = 128
SUBLANES = 8
SUB_PER_PAGE = PAGE_SIZE // CMP_STRIDE
CMP_ROW = CMP_STRIDE * KV_W
BLK_PER_TILE = K_TILE // SEL_BLOCK
BLK_ROWS = 16
V_ROWS = HEAD_DIM + 16

MXU_DTYPE = jnp.bfloat16
VMEM_LIMIT = 56 * 1024 * 1024
F32 = jnp.float32


def _mm(a, b):
    return jnp.dot(a.astype(MXU_DTYPE), b.astype(MXU_DTYPE), preferred_element_type=F32)


def _mm_nt(a, b):
    return lax.dot_general(a.astype(MXU_DTYPE), b.astype(MXU_DTYPE), (((1,), (1,)), ((), ())),
                           preferred_element_type=F32)


def _sigmoid(x):
    return 1.0 / (1.0 + jnp.exp(-x))


def _const_spec(shape):
    nd = len(shape)
    return pl.BlockSpec(shape, lambda *_: (0,) * nd, pipeline_mode=pl.Buffered(1))


def _params(*sem):
    return pltpu.CompilerParams(dimension_semantics=sem, vmem_limit_bytes=VMEM_LIMIT)


def _ff_chunk(f):
    best = LANES
    for k in range(1, f // LANES + 1):
        c = k * LANES
        if f % c == 0 and c <= 1408:
            best = c
    return best


def _ffn_kernel(*refs, fc, final):
    if final:
        x_ref, g_ref, wg_ref, wu_ref, wd_ref, gf_ref, o_ref = refs
    else:
        x_ref, g_ref, wg_ref, wu_ref, wd_ref, o_ref = refs
    x = x_ref[...]
    xn = x * lax.rsqrt(jnp.mean(x * x, -1, keepdims=True) + EPS) * g_ref[...]
    xb = xn.astype(MXU_DTYPE)
    acc = None
    for c in range(wg_ref.shape[1] // fc):
        sl = slice(c * fc, (c + 1) * fc)
        gate = _mm(xb, wg_ref[:, sl])
        up = _mm(xb, wu_ref[:, sl])
        d = _mm(gate * _sigmoid(gate) * up, wd_ref[sl, :])
        acc = d if acc is None else acc + d
    h = x + 0.5 * acc
    if final:
        h = h * lax.rsqrt(jnp.mean(h * h, -1, keepdims=True) + EPS) * gf_ref[...]
    o_ref[...] = h


def _ffn(x, g, wg, wu, wd, gfin=None, tm=512):
    r, d = x.shape
    f = wg.shape[1]
    tm = min(tm, r)
    row = pl.BlockSpec((tm, d), lambda i: (i, 0))
    in_specs = [row, _const_spec((1, d)), _const_spec((d, f)), _const_spec((d, f)), _const_spec((f, d))]
    args = [x, g.reshape(1, d), wg, wu, wd]
    if gfin is not None:
        in_specs.append(_const_spec((1, d)))
        args.append(gfin.reshape(1, d))
    return pl.pallas_call(
        functools.partial(_ffn_kernel, fc=_ff_chunk(f), final=gfin is not None),
        grid=(r // tm,), in_specs=in_specs, out_specs=row,
        out_shape=jax.ShapeDtypeStruct((r, d), F32),
        compiler_params=_params("parallel"), name="ffn")(*args)


def _rope_tables(pos):
    half = ROT_DIM // 2
    inv = ROPE_THETA ** (-jnp.arange(half, dtype=F32) / half)
    ang = pos.astype(F32)[:, None] * inv[None, :]
    cos, sin = jnp.cos(ang), jnp.sin(ang)
    t = pos.shape[0]
    rest = HEAD_DIM - ROT_DIM
    c = jnp.concatenate([cos, cos, jnp.ones((t, rest), F32)], 1)
    s1 = jnp.concatenate([jnp.zeros((t, half), F32), sin, jnp.zeros((t, rest), F32)], 1)
    s2 = jnp.concatenate([-sin, jnp.zeros((t, half + rest), F32)], 1)
    rep = LANES // HEAD_DIM
    return jnp.tile(c, (1, rep)), jnp.tile(s1, (1, rep)), jnp.tile(s2, (1, rep))


def _rope(z, c, s1, s2):
    out = []
    for j in range(z.shape[1] // LANES):
        x = z[:, j * LANES:(j + 1) * LANES]
        out.append(x * c + pltpu.roll(x, ROT_DIM // 2, 1) * s1 + pltpu.roll(x, LANES - ROT_DIM // 2, 1) * s2)
    return jnp.concatenate(out, axis=1)


def _head_tiles(z, extra=None):
    low = lax.broadcasted_iota(jnp.int32, (1, LANES), 1) < HEAD_DIM
    tiles = []
    for h in range(N_KV_HEADS):
        pair = z[:, (h // 2) * LANES:(h // 2 + 1) * LANES]
        if h % 2:
            pair = pltpu.roll(pair, HEAD_DIM, 1)
        tiles.append(jnp.where(low, pair, 0.0 if extra is None else extra))
    return tiles


def _inproj_kernel(h_ref, g_ref, w_ref, c_ref, s1_ref, s2_ref, *outs, offs, sample, qscale, nper):
    x = h_ref[...]
    tm = x.shape[0]
    xb = (x * lax.rsqrt(jnp.mean(x * x, -1, keepdims=True) + EPS) * g_ref[...]).astype(MXU_DTYPE)
    c, s1, s2 = c_ref[...], s1_ref[...], s2_ref[...]

    def seg(i):
        return _mm(xb, w_ref[:, offs[i]:offs[i + 1]])

    glu = seg(0)
    dc = glu.shape[1] // 2
    v = glu[:, :dc] * _sigmoid(glu[:, dc:])
    q = _rope(seg(1), c, s1, s2) * qscale
    kc, vc = seg(2), seg(3)
    ks, vs = _rope(seg(4), c, s1, s2), seg(5)
    kw, vw = _rope(seg(6), c, s1, s2), seg(7)
    ng = _sigmoid(seg(8))
    mg = _sigmoid(seg(9))
    if sample:
        (v_ref, q_ref, kct_ref, vct_ref, kst_ref, vst_ref, kwt_ref, vwt_ref,
         ks_ref, vs_ref, kw_ref, vw_ref, ng_ref, mg_ref) = outs
        q_ref[...] = q
        for ref, val in ((ks_ref, ks), (vs_ref, vs), (kw_ref, kw), (vw_ref, vw)):
            ref[...] = val
        ng_ref[...] = ng
    else:
        (v_ref, qt_ref, kct_ref, vct_ref, kst_ref, vst_ref, kwt_ref, vwt_ref,
         ksa_ref, kwa_ref, vstb_ref, vwtb_ref, ngt_ref, mg_ref) = outs
        qt_ref[...] = q.T.astype(MXU_DTYPE)
        pos = (pl.program_id(0) % nper) * tm + lax.broadcasted_iota(jnp.int32, (tm, 1), 0)
        blk_in_tile = lax.shift_right_logical(pos, SEL_BLOCK.bit_length() - 1) & (BLK_PER_TILE - 1)
        lane = lax.broadcasted_iota(jnp.int32, (1, LANES), 1)
        onehot = jnp.where(lane - HEAD_DIM == blk_in_tile, 1.0, 0.0)
        for h, tile in enumerate(_head_tiles(ks, onehot)):
            ksa_ref[h] = tile.astype(MXU_DTYPE)
        for h, tile in enumerate(_head_tiles(kw)):
            kwa_ref[h] = tile.astype(MXU_DTYPE)
        ngt_ref[...] = ng.T[0:3 * N_HEADS, :]
    v_ref[...] = v
    mg_ref[...] = mg
    kct_ref[...] = kc.T
    vct_ref[...] = vc.T
    kst_ref[...] = ks.T
    kwt_ref[...] = kw.T
    vst = vs.T
    vwt = vw.T
    vst_ref[...] = vst
    vwt_ref[...] = vwt
    if not sample:
        ones = jnp.where(lax.broadcasted_iota(jnp.int32, (V_ROWS - HEAD_DIM, tm), 0) == 0, 1.0, 0.0)
        for h in range(N_KV_HEADS):
            rs = slice(h * HEAD_DIM, (h + 1) * HEAD_DIM)
            vstb_ref[h] = jnp.concatenate([vst[rs], ones], axis=0).astype(MXU_DTYPE)
            vwtb_ref[h] = jnp.concatenate([vwt[rs], ones], axis=0).astype(MXU_DTYPE)


def _inproj(h, g, w_pad, offs, tables, nseq, rows_per_seq, sample, qscale, tm=512):
    r, d = h.shape
    tm = min(tm, rows_per_seq)
    nper = rows_per_seq // tm
    nat = lambda w, dt=F32: (pl.BlockSpec((tm, w), lambda i: (i, 0)), jax.ShapeDtypeStruct((r, w), dt))
    tr = lambda w, dt=F32: (pl.BlockSpec((None, w, tm), lambda i: (i // nper, 0, i % nper)),
                            jax.ShapeDtypeStruct((nseq, w, rows_per_seq), dt))
    aug = (pl.BlockSpec((None, N_KV_HEADS, tm, LANES), lambda i: (i // nper, 0, i % nper, 0)),
           jax.ShapeDtypeStruct((nseq, N_KV_HEADS, rows_per_seq, LANES), MXU_DTYPE))
    tab = pl.BlockSpec((tm, LANES), lambda i: (i % nper, 0))
    dc = (offs[1] - offs[0]) // 2
    dq = offs[2] - offs[1]
    planes = [tr(KV_W)] * 6
    if sample:
        outs = [nat(dc), nat(dq)] + planes + [nat(KV_W)] * 4 + [nat(LANES), nat(offs[10] - offs[9])]
    else:
        vplane = (pl.BlockSpec((None, N_KV_HEADS, V_ROWS, tm), lambda i: (i // nper, 0, 0, i % nper)),
                  jax.ShapeDtypeStruct((nseq, N_KV_HEADS, V_ROWS, rows_per_seq), MXU_DTYPE))
        outs = ([nat(dc), tr(dq, MXU_DTYPE)] + planes + [aug, aug, vplane, vplane,
                tr(3 * N_HEADS), nat(offs[10] - offs[9])])
    return pl.pallas_call(
        functools.partial(_inproj_kernel, offs=offs, sample=sample, qscale=qscale, nper=nper),
        grid=(r // tm,),
        in_specs=[pl.BlockSpec((tm, d), lambda i: (i, 0)), _const_spec((1, d)), _const_spec(w_pad.shape),
                  tab, tab, tab],
        out_specs=[o[0] for o in outs], out_shape=[o[1] for o in outs],
        compiler_params=_params("parallel"), name="inproj")(h, g.reshape(1, d), w_pad, *tables)


def _conv_tail(y, b_ref, lg_ref, lb_ref, wo_ref):
    y = y + b_ref[...]
    mu = jnp.mean(y, -1, keepdims=True)
    yc = y - mu
    var = jnp.mean(yc * yc, -1, keepdims=True)
    yn = yc * lax.rsqrt(var + EPS) * lg_ref[...] + lb_ref[...]
    return _mm(yn * _sigmoid(yn), wo_ref[...])


def _conv_kernel(main_ref, halo_ref, w_ref, b_ref, lg_ref, lb_ref, wo_ref, o_ref, ext_ref, sh_ref, y_ref):
    ts, c = main_ref.shape
    ext_ref[0:CONV_HALO, :] = jnp.where(pl.program_id(1) == 0, 0.0, halo_ref[...])
    ext_ref[CONV_HALO:CONV_HALO + ts, :] = main_ref[...]
    lead = CONV_HALO - (CONV_WIDTH - 1)
    n_sh = ts + CONV_HALO - SUBLANES
    for r in range(1, SUBLANES):
        sh_ref[r - 1] = ext_ref[r:r + n_sh, :]

    def tap_rows(k, r0, cs):
        off = lead + k
        r = off % SUBLANES
        base = r0 + off - r
        src = ext_ref if r == 0 else sh_ref.at[r - 1]
        return src[base:base + CONV_ROWS, cs]

    for cb in range(c // LANES):
        cs = slice(cb * LANES, (cb + 1) * LANES)
        taps = [w_ref[k:k + 1, cs] for k in range(CONV_WIDTH)]
        for r0 in range(0, ts, CONV_ROWS):
            acc = tap_rows(0, r0, cs) * taps[0]
            for k in range(1, CONV_WIDTH):
                acc = acc + tap_rows(k, r0, cs) * taps[k]
            y_ref[r0:r0 + CONV_ROWS, cs] = acc
    o_ref[...] = _conv_tail(y_ref[...], b_ref, lg_ref, lb_ref, wo_ref)


def _conv(v, w, b, lg, lb, wo, ts=512):
    bsz, l, c = v.shape
    d = wo.shape[1]
    step = ts // CONV_HALO
    vec = lambda a: a.reshape(1, c)
    return pl.pallas_call(
        _conv_kernel,
        grid=(bsz, l // ts),
        in_specs=[pl.BlockSpec((None, ts, c), lambda bi, i: (bi, i, 0)),
                  pl.BlockSpec((None, CONV_HALO, c), lambda bi, i: (bi, jnp.maximum(i * step - 1, 0), 0)),
                  _const_spec((CONV_WIDTH, c)), _const_spec((1, c)), _const_spec((1, c)), _const_spec((1, c)),
                  _const_spec((c, d))],
        out_specs=pl.BlockSpec((None, ts, d), lambda bi, i: (bi, i, 0)),
        out_shape=jax.ShapeDtypeStruct((bsz, l, d), F32),
        scratch_shapes=[pltpu.VMEM((ts + CONV_HALO, c), F32),
                        pltpu.VMEM((SUBLANES - 1, ts + CONV_HALO - SUBLANES, c), F32), pltpu.VMEM((ts, c), F32)],
        compiler_params=_params("parallel", "parallel"), name="conv")(v, v, w, vec(b), vec(lg), vec(lb), wo)


def _conv_sample_kernel(st_ref, v_ref, w_ref, b_ref, lg_ref, lb_ref, wo_ref, o_ref):
    acc = v_ref[...] * w_ref[CONV_WIDTH - 1:CONV_WIDTH, :]
    for k in range(CONV_WIDTH - 1):
        acc = acc + st_ref[k] * w_ref[k:k + 1, :]
    o_ref[...] = _conv_tail(acc, b_ref, lg_ref, lb_ref, wo_ref)


def _conv_sample(st, v, w, b, lg, lb, wo):
    rows, c = v.shape
    d = wo.shape[1]
    vec = lambda a: a.reshape(1, c)
    args = (st, v, w, vec(b), vec(lg), vec(lb), wo)
    return pl.pallas_call(
        _conv_sample_kernel, grid=(1,),
        in_specs=[_const_spec(a.shape) for a in args],
        out_specs=pl.BlockSpec((rows, d), lambda i: (0, 0)),
        out_shape=jax.ShapeDtypeStruct((rows, d), F32),
        compiler_params=_params("arbitrary"), name="conv_sample")(*args)


def _compress_kernel(pt_ref, x_hbm, w1_ref, pe_ref, w1f_ref, b1_ref, w2_ref, w2h_ref, nat_ref, tr_ref,
                     buf, tbuf, fsbuf, sem, *, pc, n_pages, paged):
    s = pl.program_id(0)
    ci = pl.program_id(1)
    n_chunks = pl.num_programs(1)
    step = s * n_chunks + ci
    slot = step % 2
    m = pc * SUB_PER_PAGE
    nrow = m + SUB_PER_PAGE
    pairs = N_KV_HEADS // 2

    def page_copy(s_, ci_, slot_, p):
        pg = pt_ref[s_ * n_pages + jnp.minimum(ci_ * pc + p, n_pages - 1)]
        if paged:
            src = x_hbm.at[pg]
        else:
            src = x_hbm.at[s_, :, :, pl.ds(pl.multiple_of(pg * PAGE_SIZE, PAGE_SIZE), PAGE_SIZE)]
        return pltpu.make_async_copy(src, buf.at[slot_, p], sem.at[slot_])

    @pl.when(step == 0)
    def _():
        for p in range(pc + 1):
            page_copy(s, ci, slot, p).start()

    @pl.when(step + 1 < pl.num_programs(0) * n_chunks)
    def _():
        wrap = ci + 1 == n_chunks
        for p in range(pc + 1):
            page_copy(jnp.where(wrap, s + 1, s), jnp.where(wrap, 0, ci + 1), 1 - slot, p).start()

    for p in range(pc + 1):
        page_copy(s, ci, slot, p).wait()

    for p in range(pc + 1):
        for j in range(pairs):
            plane = buf[slot, p, 2 * j:2 * j + 2].reshape(2 * HEAD_DIM, PAGE_SIZE)
            tbuf[j, p * PAGE_SIZE:(p + 1) * PAGE_SIZE, :] = plane.T
    for j in range(pairs):
        acc = None
        for lp in range(CMP_STRIDE // 2):
            lhs = jnp.concatenate([tbuf[j, pl.ds(2 * lp + i, nrow, stride=CMP_STRIDE), :] for i in range(2)], axis=1)
            d = _mm(lhs, w1_ref[lp])
            acc = d if acc is None else acc + d
        fsbuf[j] = acc
    c = _mm(pe_ref[...], w1f_ref[...])[0:1, :] + b1_ref[...]
    nh = c.shape[1]
    parts = []
    for h in range(N_KV_HEADS):
        col = (h % 2) * 2 * nh
        parts.append(fsbuf[h // 2, 0:m, col:col + nh] + fsbuf[h // 2, pl.ds(1, m), col + nh:col + 2 * nh] + c)
    hid = jnp.concatenate(parts, axis=1)
    act = (hid * _sigmoid(hid)).astype(MXU_DTYPE)
    tr_ref[...] = _mm(act, w2_ref[...]).T.astype(MXU_DTYPE)
    for h in range(N_KV_HEADS):
        nat_ref[h] = _mm(act, w2h_ref[h]).astype(MXU_DTYPE)


def _compress(x, page_table, weights, pc, paged):
    w1_big, pe8, w1_flat, b1, w2_big, w2_heads = weights
    n_seq, n_pages = page_table.shape
    m = pc * SUB_PER_PAGE
    n_sub = n_pages * SUB_PER_PAGE
    hidden = b1.shape[0]
    grid_spec = pltpu.PrefetchScalarGridSpec(
        num_scalar_prefetch=1,
        grid=(n_seq, n_pages // pc),
        in_specs=[pl.BlockSpec(memory_space=pl.ANY),
                  _const_spec(w1_big.shape), _const_spec(pe8.shape), _const_spec(w1_flat.shape),
                  _const_spec((1, hidden)), _const_spec(w2_big.shape), _const_spec(w2_heads.shape)],
        out_specs=[pl.BlockSpec((None, N_KV_HEADS, m, LANES), lambda s, ci, pt: (s, 0, ci, 0)),
                   pl.BlockSpec((None, KV_W, m), lambda s, ci, pt: (s, 0, ci))],
        scratch_shapes=[pltpu.VMEM((2, pc + 1, N_KV_HEADS, HEAD_DIM, PAGE_SIZE), F32),
                        pltpu.VMEM((N_KV_HEADS // 2, (pc + 1) * PAGE_SIZE, LANES), F32),
                        pltpu.VMEM((N_KV_HEADS // 2, m + SUB_PER_PAGE, w1_big.shape[2]), F32),
                        pltpu.SemaphoreType.DMA((2,))])
    return pl.pallas_call(
        functools.partial(_compress_kernel, pc=pc, n_pages=n_pages, paged=paged),
        grid_spec=grid_spec,
        out_shape=[jax.ShapeDtypeStruct((n_seq, N_KV_HEADS, n_sub, LANES), MXU_DTYPE),
                   jax.ShapeDtypeStruct((n_seq, KV_W, n_sub), MXU_DTYPE)],
        compiler_params=_params("arbitrary", "arbitrary"), name="compress")(
            page_table.reshape(-1), x, w1_big, pe8, w1_flat, b1.reshape(1, hidden), w2_big, w2_heads)


def _compress_weights(pe, w1, b1, w2):
    hidden = w1.shape[-1]
    eye = jnp.eye(N_KV_HEADS, dtype=F32)
    both = jnp.concatenate([w1[:CMP_STRIDE], w1[CMP_STRIDE:]], axis=-1)
    both = both.reshape(CMP_STRIDE // 2, 2, HEAD_DIM, 2 * hidden)
    w1_big = jnp.einsum('pidc,hg->pihdgc', both, jnp.eye(2, dtype=F32)).reshape(
        CMP_STRIDE // 2, 4 * HEAD_DIM, 4 * hidden)
    w2_big = jnp.einsum('kd,hg->hkgd', w2, eye).reshape(N_KV_HEADS * hidden, KV_W)
    w2_pad = jnp.pad(w2, ((0, 0), (0, LANES - HEAD_DIM)))
    w2_heads = jnp.einsum('kd,hg->ghkd', w2_pad, eye).reshape(N_KV_HEADS, N_KV_HEADS * hidden, LANES)
    pe8 = jnp.zeros((SUBLANES, CMP_LEN * HEAD_DIM), F32).at[0].set(pe.reshape(-1))
    return (w1_big.astype(MXU_DTYPE), pe8, w1.reshape(CMP_LEN * HEAD_DIM, hidden).astype(MXU_DTYPE), b1,
            w2_big.astype(MXU_DTYPE), w2_heads.astype(MXU_DTYPE))


def _overlap(n_cmp_pad, n_blk_pad):
    ci = np.arange(n_cmp_pad)[:, None] * CMP_STRIDE
    bs = np.arange(n_blk_pad)[None, :] * SEL_BLOCK
    return ((ci < bs + SEL_BLOCK) & (ci + CMP_LEN > bs)).astype(np.float32)


def _attn_kernel(q_ref, g_ref, kc_ref, vct_ref, ov_ref, ks_ref, vst_ref, kw_ref, vwt_ref, o_ref,
                 bias_ref, sc_ref, imp_ref, sa_ref, sb_ref, *, n_cmp, n_blk):
    s0 = pl.program_id(2) * Q_TILE
    cols = GROUP * Q_TILE
    qt = jnp.concatenate([q_ref[g] for g in range(GROUP)], axis=1)
    q0 = jnp.concatenate([qt, jnp.zeros((LANES - HEAD_DIM, cols), qt.dtype)], axis=0)
    tq = s0 + (lax.broadcasted_iota(jnp.int32, (1, cols), 1) & (Q_TILE - 1))
    tq1 = s0 + lax.broadcasted_iota(jnp.int32, (1, Q_TILE), 1)

    ncp = kc_ref.shape[0]
    ch = min(CMP_CHUNK, ncp)
    last_c = lax.shift_right_logical(s0 + (Q_TILE - CMP_LEN), CMP_STRIDE.bit_length() - 1)
    n_cc = jnp.minimum(last_c // ch + 1, ncp // ch)
    ci0 = lax.broadcasted_iota(jnp.int32, (ch, 1), 0)

    def cmp_scores(cc, m):
        r0 = pl.multiple_of(cc * ch, ch)
        c = ci0 + r0
        s = jnp.where((c * CMP_STRIDE + (CMP_LEN - 1) <= tq) & (c < n_cmp), _mm(kc_ref[pl.ds(r0, ch), :], q0), NEG)
        sc_ref[pl.ds(r0, ch), :] = s
        return jnp.maximum(m, jnp.max(s, axis=0, keepdims=True))

    m_c = lax.fori_loop(0, n_cc, cmp_scores, jnp.full((1, cols), NEG, F32))
    imp_ref[...] = jnp.zeros(imp_ref.shape, F32)

    def cmp_weights(cc, carry):
        l, acc = carry
        r0 = pl.multiple_of(cc * ch, ch)
        p = jnp.exp2(sc_ref[pl.ds(r0, ch), :] - m_c)
        pb = p.astype(MXU_DTYPE)
        imp_ref[...] += _mm(ov_ref[:, pl.ds(r0, ch)], pb)
        return l + jnp.sum(p, axis=0, keepdims=True), acc + _mm(vct_ref[:, pl.ds(r0, ch)], pb)

    l_c, acc_c = lax.fori_loop(0, n_cc, cmp_weights,
                               (jnp.zeros((1, cols), F32), jnp.zeros((HEAD_DIM, cols), F32)))
    inv_c = jnp.where(m_c > 0.5 * NEG, 1.0 / jnp.maximum(l_c, 1e-30), 0.0)
    o_c = acc_c * inv_c
    impn = imp_ref[...] * inv_c
    imp = impn[:, 0:Q_TILE]
    for g in range(1, GROUP):
        imp = imp + impn[:, g * Q_TILE:(g + 1) * Q_TILE]

    wk = WINDOW + Q_TILE
    w0 = pl.multiple_of(jnp.maximum(s0 - WINDOW, 0), Q_TILE)
    wpos = w0 + lax.broadcasted_iota(jnp.int32, (wk, 1), 0)
    sw = jnp.where((wpos <= tq) & (wpos > tq - WINDOW), _mm(kw_ref[pl.ds(w0, wk), :], q0), NEG)
    acc_w = _mm(vwt_ref[:, pl.ds(w0, wk)], jnp.exp2(sw - jnp.max(sw, axis=0, keepdims=True)))
    o_w = acc_w[0:HEAD_DIM] / acc_w[HEAD_DIM:HEAD_DIM + 1]

    blk = lax.broadcasted_iota(jnp.int32, (n_blk, 1), 0)
    blk_f = blk.astype(F32)
    cur = lax.shift_right_logical(tq1, SEL_BLOCK.bit_length() - 1)
    forced = (blk == 0) | (blk == cur) | (blk == cur - 1)
    valid = blk * SEL_BLOCK <= tq1
    n_top = min(N_SELECT, n_blk)

    def ranked_selection():
        score = jnp.where(valid, imp + jnp.where(forced, BIG, 0.0), -BIG)
        for _ in range(n_top):
            top = jnp.max(score, axis=0, keepdims=True)
            first = jnp.min(jnp.where(score == top, blk_f, float(n_blk)), axis=0, keepdims=True)
            score = jnp.where(blk_f == first, REMOVED, score)
        return score == REMOVED

    free = valid & jnp.logical_not(forced)
    cand = jnp.where(free, imp, REMOVED)
    n_forced = 1 + (cur >= 1).astype(jnp.int32) + (cur >= 2).astype(jnp.int32)
    n_free = jnp.sum(jnp.where(free, 1.0, 0.0), axis=0, keepdims=True)
    want = jnp.minimum((n_top - n_forced).astype(F32), n_free)
    thr = jnp.full((1, Q_TILE), jnp.inf, F32)
    kth = {}
    for r in range(1, n_top):
        thr = jnp.max(jnp.where(cand < thr, cand, REMOVED), axis=0, keepdims=True)
        kth[r] = thr
    thr = jnp.where(n_forced == 3, kth[max(n_top - 3, 1)], jnp.where(n_forced == 2, kth[max(n_top - 2, 1)],
                                                                      kth[n_top - 1]))
    picked = free & (cand >= thr)
    got = jnp.sum(jnp.where(picked, 1.0, 0.0), axis=0, keepdims=True)
    exact = jnp.max(jnp.abs(got - want)) == 0.0
    as_bias = lambda selected: jnp.where(selected & valid, 0.0, NEG)
    bias_ref[...] = lax.cond(exact, lambda: as_bias(picked | forced), lambda: as_bias(ranked_selection()))

    kpos0 = lax.broadcasted_iota(jnp.int32, (K_TILE, 1), 0)
    q_pad = jnp.zeros((LANES - HEAD_DIM - BLK_ROWS, cols), qt.dtype)
    b_pad = jnp.zeros((BLK_ROWS - BLK_PER_TILE, cols), F32)

    def sel_scores(j, s_ref):
        k0 = pl.multiple_of(j * K_TILE, K_TILE)
        b8 = bias_ref[pl.ds(pl.multiple_of(j * BLK_PER_TILE, BLK_PER_TILE), BLK_PER_TILE), :]
        rows = jnp.concatenate([jnp.concatenate([b8] * GROUP, axis=1), b_pad], axis=0).astype(qt.dtype)
        s_ref[...] = _mm(ks_ref[pl.ds(k0, K_TILE), :], jnp.concatenate([qt, rows, q_pad], axis=0))

    def sel_update(j, s, carry):
        m, acc = carry
        m_new = jnp.maximum(m, jnp.max(s, axis=0, keepdims=True))
        p = jnp.exp2(s - m_new)
        acc = jnp.exp2(m - m_new) * acc + _mm(vst_ref[:, pl.ds(pl.multiple_of(j * K_TILE, K_TILE), K_TILE)], p)
        return m_new, acc

    def sel_pair(i, carry):
        sel_scores(2 * i + 1, sb_ref)
        carry = sel_update(2 * i, sa_ref[...], carry)
        sel_scores(2 * i + 2, sa_ref)
        return sel_update(2 * i + 1, sb_ref[...], carry)

    last = (s0 + Q_TILE - 1) // K_TILE
    n_pairs = last // 2
    sel_scores(0, sa_ref)
    carry = lax.fori_loop(0, n_pairs, sel_pair, (jnp.full((1, cols), NEG, F32), jnp.zeros((V_ROWS, cols), F32)))
    ta = 2 * n_pairs

    def sel_tail(t, s_ref, c):
        return sel_update(t, jnp.where(kpos0 + t * K_TILE <= tq, s_ref[...], NEG), c)

    def two_left(c):
        sel_scores(last, sb_ref)
        return sel_tail(last, sb_ref, sel_tail(ta, sa_ref, c))

    _, acc_s = lax.cond(ta < last, two_left, lambda c: sel_tail(ta, sa_ref, c), carry)
    o_s = acc_s[0:HEAD_DIM] / acc_s[HEAD_DIM:HEAD_DIM + 1]

    gate = lambda br: jnp.concatenate([g_ref[g * 3 + br:g * 3 + br + 1, :] for g in range(GROUP)], axis=1)
    o = gate(0) * o_c + gate(1) * o_s + gate(2) * o_w
    for g in range(GROUP):
        o_ref[g] = o[:, g * Q_TILE:(g + 1) * Q_TILE]


def _attn_prompt(qt, gt, kc, vct, ks, vst, kw, vwt, n_cmp):
    b, h, grp, hd, s = qt.shape
    ncp = kc.shape[2]
    n_blk = s // SEL_BLOCK
    nt = s // Q_TILE
    cols = grp * Q_TILE
    ov = jnp.asarray(_overlap(ncp, n_blk).T, MXU_DTYPE)
    one = pl.Buffered(1)
    rows = lambda n: pl.BlockSpec((None, None, n, LANES), lambda bi, hi, t: (bi, hi, 0, 0), pipeline_mode=one)
    plane = lambda n: pl.BlockSpec((None, hd, n), lambda bi, hi, t: (bi, hi, 0), pipeline_mode=one)
    vplane = pl.BlockSpec((None, None, V_ROWS, s), lambda bi, hi, t: (bi, hi, 0, 0), pipeline_mode=one)
    qspec = pl.BlockSpec((None, None, grp, hd, Q_TILE), lambda bi, hi, t: (bi, hi, 0, 0, t))
    return pl.pallas_call(
        functools.partial(_attn_kernel, n_cmp=n_cmp, n_blk=n_blk),
        grid=(b, h, nt),
        in_specs=[qspec, pl.BlockSpec((None, None, 3 * grp, Q_TILE), lambda bi, hi, t: (bi, hi, 0, t)),
                  rows(ncp), plane(ncp), _const_spec(ov.shape), rows(s), vplane, rows(s), vplane],
        out_specs=qspec,
        out_shape=jax.ShapeDtypeStruct((b, h, grp, hd, s), F32),
        scratch_shapes=[pltpu.VMEM((n_blk, Q_TILE), F32), pltpu.VMEM((ncp, cols), F32),
                        pltpu.VMEM((n_blk, cols), F32), pltpu.VMEM((K_TILE, cols), F32),
                        pltpu.VMEM((K_TILE, cols), F32)],
        compiler_params=_params("parallel", "parallel", "arbitrary"), name="attn_prompt")(
            qt, gt, kc, vct, ov, ks, vst, kw, vwt)


def _sample_cmp_kernel(q_ref, kct_ref, vc_ref, ov_ref, oc_ref, idx_ref, *, n_cmp, n_blk, t):
    nb = q_ref.shape[0]
    ncp = kct_ref.shape[2]
    nbp = ov_ref.shape[1]
    ci = lax.broadcasted_iota(jnp.int32, (1, ncp), 1)
    mask = (ci * CMP_STRIDE + (CMP_LEN - 1) <= t) & (ci < n_cmp)
    sums = []
    for b in range(nb):
        for h in range(N_KV_HEADS):
            s = jnp.where(mask, _mm(q_ref[b, h], kct_ref[b, h * HEAD_DIM:(h + 1) * HEAD_DIM, :]), NEG)
            m = jnp.max(s, axis=1, keepdims=True)
            p = jnp.where(mask, jnp.exp(s - m), 0.0)
            p = p / jnp.maximum(jnp.sum(p, axis=1, keepdims=True), 1e-30)
            oc_ref[b, h] = _mm(p, vc_ref[b, h])
            sums.append(jnp.sum(p, axis=0, keepdims=True))
    imp = _mm(jnp.concatenate(sums, axis=0), ov_ref[...])
    blk = lax.broadcasted_iota(jnp.int32, (1, nbp), 1)
    cur = t // SEL_BLOCK
    forced = (blk == 0) | (blk == cur) | (blk == cur - 1)
    score = jnp.where(blk * SEL_BLOCK <= t, imp + jnp.where(forced, BIG, 0.0), -BIG)
    score = jnp.where(blk < n_blk, score, REMOVED)
    blk_f = blk.astype(F32)
    lane = lax.broadcasted_iota(jnp.int32, (1, LANES), 1)
    out = jnp.zeros((nb * N_KV_HEADS, LANES), F32)
    for r in range(min(N_SELECT, n_blk)):
        top = jnp.max(score, axis=1, keepdims=True)
        first = jnp.min(jnp.where(score == top, blk_f, float(nbp)), axis=1, keepdims=True)
        out = jnp.where(lane == r, first, out)
        score = jnp.where(blk_f == first, REMOVED, score)
    idx_ref[...] = out.astype(jnp.int32)


def _sample_cmp(q4, kct, vc, n_cmp, n_blk, t):
    db, _, _, hd = q4.shape
    ncp = kct.shape[2]
    nbp = -(-n_blk // LANES) * LANES
    nb = SAMPLE_BATCH if db % SAMPLE_BATCH == 0 else 1
    ov = jnp.asarray(_overlap(ncp, nbp), MXU_DTYPE)
    per = lambda r, w: pl.BlockSpec((nb, N_KV_HEADS, r, w), lambda b: (b, 0, 0, 0))
    oc, idx = pl.pallas_call(
        functools.partial(_sample_cmp_kernel, n_cmp=n_cmp, n_blk=n_blk, t=t),
        grid=(db // nb,),
        in_specs=[per(GROUP, hd), pl.BlockSpec((nb, KV_W, ncp), lambda b: (b, 0, 0)), per(ncp, LANES),
                  _const_spec(ov.shape)],
        out_specs=[per(GROUP, LANES), pl.BlockSpec((None, nb * N_KV_HEADS, LANES), lambda b: (b, 0, 0))],
        out_shape=[jax.ShapeDtypeStruct((db, N_KV_HEADS, GROUP, LANES), F32),
                   jax.ShapeDtypeStruct((db // nb, nb * N_KV_HEADS, LANES), jnp.int32)],
        compiler_params=_params("parallel"), name="sample_cmp")(q4, kct, vc, ov)
    return oc, idx.reshape(db, N_KV_HEADS, LANES)


def _sample_sel_kernel(idx_ref, pt_ref, q_ref, oc_ref, g_ref, ksn_ref, vsn_ref, kwn_ref, vwn_ref, wk_ref, wv_ref,
                       pk_hbm, pv_hbm, o_ref, kbuf, vbuf, sem, *, n_past_blk, n_pages, nsel, t):
    b = pl.program_id(0)
    sub = PAGE_SIZE // SEL_BLOCK

    slot = b % 2

    def sel_block(b_, h, n):
        return idx_ref[(b_ * N_KV_HEADS + h) * nsel + n]

    def page_copies(b_, slot_, h, n):
        pg = pt_ref[b_ * n_pages + jnp.minimum(sel_block(b_, h, n), n_past_blk - 1) // sub]
        dst = pl.ds(n * PAGE_SIZE, PAGE_SIZE)
        return (pltpu.make_async_copy(pk_hbm.at[pg, h], kbuf.at[slot_, h, :, dst], sem.at[0, slot_]),
                pltpu.make_async_copy(pv_hbm.at[pg, h], vbuf.at[slot_, h, :, dst], sem.at[1, slot_]))

    def all_copies(b_, slot_, go):
        for h in range(N_KV_HEADS):
            for n in range(nsel):
                for cp in page_copies(b_, slot_, h, n):
                    go(cp)

    @pl.when(b == 0)
    def _():
        all_copies(b, slot, lambda cp: cp.start())

    @pl.when(b + 1 < pl.num_programs(0))
    def _():
        all_copies(b + 1, 1 - slot, lambda cp: cp.start())

    all_copies(b, slot, lambda cp: cp.wait())

    lane = lax.broadcasted_iota(jnp.int32, (1, nsel * PAGE_SIZE), 1)
    seg = lax.shift_right_logical(lane, PAGE_SIZE.bit_length() - 1)
    half = lax.shift_right_logical(lane, SEL_BLOCK.bit_length() - 1) & (sub - 1)
    wl = wk_ref.shape[2]
    wpos = t - wl + lax.broadcasted_iota(jnp.int32, (1, wl), 1)
    okw = (wpos > t - WINDOW) & (wpos >= 0)
    for h in range(N_KV_HEADS):
        q = q_ref[h]
        want = jnp.full(lane.shape, -1, jnp.int32)
        for n in range(nsel):
            blk = sel_block(b, h, n)
            want = jnp.where(seg == n, jnp.where(blk < n_past_blk, blk % sub, -1), want)
        ok = half == want
        s = jnp.where(ok, _mm(q, kbuf[slot, h]), NEG)
        s_new = jnp.sum(q * ksn_ref[h:h + 1, :], axis=1, keepdims=True)
        m = jnp.maximum(jnp.max(s, axis=1, keepdims=True), s_new)
        p = jnp.where(ok, jnp.exp(s - m), 0.0)
        p_new = jnp.exp(s_new - m)
        o_s = ((_mm_nt(p, vbuf[slot, h]) + p_new * vsn_ref[h:h + 1, :])
               / (jnp.sum(p, axis=1, keepdims=True) + p_new))
        sw = jnp.where(okw, _mm(q, wk_ref[h]), NEG)
        sw_new = jnp.sum(q * kwn_ref[h:h + 1, :], axis=1, keepdims=True)
        mw = jnp.maximum(jnp.max(sw, axis=1, keepdims=True), sw_new)
        pw = jnp.where(okw, jnp.exp(sw - mw), 0.0)
        pw_new = jnp.exp(sw_new - mw)
        o_w = (_mm_nt(pw, wv_ref[h]) + pw_new * vwn_ref[h:h + 1, :]) / (jnp.sum(pw, axis=1, keepdims=True) + pw_new)
        g = g_ref[h]
        o_ref[h] = g[:, 0:1] * oc_ref[h][:, 0:HEAD_DIM] + g[:, 1:2] * o_s + g[:, 2:3] * o_w


def _sample_sel(idx, page_table, q4, oc, gates, ks_new, vs_new, kw_new, vw_new, win_k, win_v, pool_k, pool_v, t):
    db, _, grp, hd = q4.shape
    nsel = idx.shape[2]
    n_pages = page_table.shape[1]
    n_past_blk = n_pages * (PAGE_SIZE // SEL_BLOCK)
    wl = win_k.shape[3]
    per = lambda *shape: pl.BlockSpec((None,) + shape, lambda b, *_: (b,) + (0,) * len(shape))
    hbm = pl.BlockSpec(memory_space=pl.ANY)
    grid_spec = pltpu.PrefetchScalarGridSpec(
        num_scalar_prefetch=2,
        grid=(db,),
        in_specs=[per(N_KV_HEADS, grp, hd), per(N_KV_HEADS, grp, LANES), per(N_KV_HEADS, grp, LANES)]
                 + [per(N_KV_HEADS, hd)] * 4 + [per(N_KV_HEADS, hd, wl)] * 2 + [hbm, hbm],
        out_specs=per(N_KV_HEADS, grp, hd),
        scratch_shapes=[pltpu.VMEM((2, N_KV_HEADS, hd, nsel * PAGE_SIZE), F32),
                        pltpu.VMEM((2, N_KV_HEADS, hd, nsel * PAGE_SIZE), F32),
                        pltpu.SemaphoreType.DMA((2, 2))])
    return pl.pallas_call(
        functools.partial(_sample_sel_kernel, n_past_blk=n_past_blk, n_pages=n_pages, nsel=nsel, t=t),
        grid_spec=grid_spec,
        out_shape=jax.ShapeDtypeStruct((db, N_KV_HEADS, grp, hd), F32),
        compiler_params=_params("arbitrary"), name="sample_sel")(
            idx.reshape(-1), page_table.reshape(-1), q4, oc, gates, ks_new, vs_new, kw_new, vw_new,
            win_k, win_v, pool_k, pool_v)


def _merge_kernel(h_ref, yc_ref, ya_ref, mg_ref, wa_ref, wo_ref, o_ref, *, planes):
    d = h_ref.shape[1]
    ya = ya_ref[...].T if planes else ya_ref[...]
    m = mg_ref[:, :d] * yc_ref[...] + mg_ref[:, d:] * _mm(ya, wa_ref[...])
    o_ref[...] = h_ref[...] + _mm(m, wo_ref[...])


def _merge(h, y_conv, y_attn, mg, wa, wo, rows_per_seq, planes, tm=512):
    r, d = h.shape
    tm = min(tm, rows_per_seq)
    nper = rows_per_seq // tm
    row = lambda w: pl.BlockSpec((tm, w), lambda i: (i, 0))
    da = wa.shape[0]
    ya_spec = pl.BlockSpec((None, da, tm), lambda i: (i // nper, 0, i % nper)) if planes else row(da)
    return pl.pallas_call(
        functools.partial(_merge_kernel, planes=planes), grid=(r // tm,),
        in_specs=[row(d), row(d), ya_spec, row(2 * d), _const_spec(wa.shape), _const_spec(wo.shape)],
        out_specs=row(d), out_shape=jax.ShapeDtypeStruct((r, d), F32),
        compiler_params=_params("parallel"), name="merge")(h, y_conv, y_attn, mg, wa, wo)


def kernel(x_prompt, x_sample, cache_cmp_k, cache_cmp_v, cache_sel_k, cache_sel_v, state_win_k, state_win_v, state_conv, page_table, norm_ffn1, ffn1_gate, ffn1_up, ffn1_down, norm_mix, w_in, conv_w, conv_b, conv_ln_g, conv_ln_b, w_conv_out, cmp_pe_k, cmp_w1_k, cmp_b1_k, cmp_w2_k, cmp_pe_v, cmp_w1_v, cmp_b1_v, cmp_w2_v, w_attn_out, w_out, norm_ffn2, ffn2_gate, ffn2_up, ffn2_down, norm_final):
    bsz, seq, d = x_prompt.shape
    db, dseq, _ = x_sample.shape
    depth = norm_ffn1.shape[0]
    assert depth == 1 and dseq == 1 and d == N_HEADS * HEAD_DIM
    assert seq % K_TILE == 0 and seq >= WINDOW + Q_TILE
    assert state_conv.shape[2] == CONV_WIDTH - 1
    n_pages = page_table.shape[1]
    past = n_pages * PAGE_SIZE
    d_conv = d // 2
    scale = HEAD_DIM ** -0.5
    cast = lambda w: w.astype(MXU_DTYPE)

    col_sizes = (2 * d_conv, N_HEADS * HEAD_DIM) + (KV_W,) * 6 + (3 * N_HEADS, 2 * d)
    starts = np.concatenate([[0], np.cumsum(col_sizes)])
    offs, pieces, pos = [0], [], 0
    for i, n in enumerate(col_sizes):
        pad = -n % LANES
        pieces.append(w_in[0][:, starts[i]:starts[i + 1]])
        if pad:
            pieces.append(jnp.zeros((d, pad), F32))
        pos += n + pad
        offs.append(pos)
    w_in_pad = cast(jnp.concatenate(pieces, axis=1))
    offs = tuple(offs)
    ffn1 = (norm_ffn1[0], cast(ffn1_gate[0]), cast(ffn1_up[0]), cast(ffn1_down[0]))
    ffn2 = (norm_ffn2[0], cast(ffn2_gate[0]), cast(ffn2_up[0]), cast(ffn2_down[0]))
    wa, wo, wco = cast(w_attn_out[0]), cast(w_out[0]), cast(w_conv_out[0])
    convp = (conv_w[0], conv_b[0], conv_ln_g[0], conv_ln_b[0], wco)
    cmpk = _compress_weights(cmp_pe_k[0], cmp_w1_k[0], cmp_b1_k[0], cmp_w2_k[0])
    cmpv = _compress_weights(cmp_pe_v[0], cmp_w1_v[0], cmp_b1_v[0], cmp_w2_v[0])
    planes4 = lambda x: x.reshape(x.shape[0], N_KV_HEADS, HEAD_DIM, x.shape[2])
    state5 = lambda x: jnp.transpose(planes4(x), (0, 3, 1, 2))[None]
    to_planes = lambda x: jnp.transpose(x, (0, 2, 3, 1))

    rp = bsz * seq
    h_p = _ffn(x_prompt.reshape(rp, d), *ffn1)
    (v_p, qt_p, kct_p, vct_p, kst_p, vst_p, kwt_p, vwt_p, ksa_p, kwa_p, vstb_p, vwtb_p, ngt_p, mg_p) = _inproj(
        h_p, norm_mix[0], w_in_pad, offs, _rope_tables(jnp.arange(seq, dtype=jnp.int32)), bsz, seq,
        sample=False, qscale=scale * LOG2E)
    y_conv_p = _conv(v_p.reshape(bsz, seq, d_conv), *convp).reshape(rp, d)
    pages_p = seq // PAGE_SIZE
    pc_p = min(32, pages_p)
    ident = jnp.tile(jnp.arange(pages_p, dtype=jnp.int32)[None], (bsz, 1))
    kc_rows, _ = _compress(planes4(kct_p), ident, cmpk, pc_p, paged=False)
    _, vc_planes = _compress(planes4(vct_p), ident, cmpv, pc_p, paged=False)
    ot = _attn_prompt(qt_p.reshape(bsz, N_KV_HEADS, GROUP, HEAD_DIM, seq),
                      ngt_p.reshape(bsz, N_KV_HEADS, 3 * GROUP, seq),
                      kc_rows, vc_planes, ksa_p, vstb_p, kwa_p, vwtb_p, seq // CMP_STRIDE - 1)
    h2_p = _merge(h_p, y_conv_p, ot.reshape(bsz, d, seq), mg_p, wa, wo, seq, planes=True)
    y_prompt = _ffn(h2_p, *ffn2, gfin=norm_final).reshape(bsz, seq, d)

    t = past
    h_s = _ffn(x_sample.reshape(db, d), *ffn1)
    (v_s, q_s, kct_s, vct_s, kst_s, vst_s, kwt_s, vwt_s, ks_s, vs_s, kw_s, vw_s, ng_s, mg_s) = _inproj(
        h_s, norm_mix[0], w_in_pad, offs, _rope_tables(jnp.full((db,), t, jnp.int32)), 1, db,
        sample=True, qscale=scale)
    y_conv_s = _conv_sample(jnp.transpose(state_conv[0], (1, 0, 2)), v_s, *convp)
    pc_s = min(32, n_pages)
    n_sub_s = (past + dseq) // CMP_STRIDE
    assert n_sub_s == n_pages * SUB_PER_PAGE
    _, kc_planes_s = _compress(to_planes(cache_cmp_k[0]), page_table, cmpk, pc_s, paged=True)
    vc_rows_s, _ = _compress(to_planes(cache_cmp_v[0]), page_table, cmpv, pc_s, paged=True)
    n_blk_s = -(-(past + dseq) // SEL_BLOCK)
    q4 = q_s.reshape(db, N_KV_HEADS, GROUP, HEAD_DIM)
    oc_s, idx_s = _sample_cmp(q4, kc_planes_s, vc_rows_s, n_sub_s - 1, n_blk_s, t)
    nsel = min(N_SELECT, n_blk_s)
    gates_s = jnp.pad(ng_s[:, :3 * N_HEADS].reshape(db, N_KV_HEADS, GROUP, 3),
                      ((0, 0), (0, 0), (0, 0), (0, LANES - 3)))
    new3 = lambda x: x.reshape(db, N_KV_HEADS, HEAD_DIM)
    y_attn_s = _sample_sel(idx_s[:, :, :nsel], page_table, q4, oc_s, gates_s,
                           new3(ks_s), new3(vs_s), new3(kw_s), new3(vw_s),
                           to_planes(state_win_k[0]), to_planes(state_win_v[0]),
                           to_planes(cache_sel_k[0]), to_planes(cache_sel_v[0]), t).reshape(db, d)
    h2_s = _merge(h_s, y_conv_s, y_attn_s, mg_s, wa, wo, db, planes=False)
    y_sample = _ffn(h2_s, *ffn2, gfin=norm_final).reshape(db, dseq, d)

    wl_p = min(WINDOW, seq)
    p_conv = v_p.reshape(bsz, seq, d_conv)[:, seq - (CONV_WIDTH - 1):][None]
    new5 = lambda x: jnp.transpose(x.reshape(N_KV_HEADS, HEAD_DIM, db), (2, 0, 1)).reshape(1, db, 1, N_KV_HEADS, HEAD_DIM)
    wl_s = state_win_k.shape[2]
    row4 = lambda x: x.reshape(db, 1, N_KV_HEADS, HEAD_DIM)
    s_win_k = jnp.concatenate([state_win_k[0], row4(kw_s)], axis=1)[:, -wl_s:][None]
    s_win_v = jnp.concatenate([state_win_v[0], row4(vw_s)], axis=1)[:, -wl_s:][None]
    s_conv = jnp.concatenate([state_conv[0], v_s[:, None, :]], axis=1)[:, -(CONV_WIDTH - 1):][None]
    return (y_prompt, y_sample,
            state5(kct_p), state5(vct_p), state5(kst_p), state5(vst_p),
            state5(kwt_p[:, :, seq - wl_p:]), state5(vwt_p[:, :, seq - wl_p:]), p_conv,
            new5(kct_s), new5(vct_s), new5(kst_s), new5(vst_s),
            s_win_k, s_win_v, s_conv)
```

```python
import functools

import numpy as np
import jax
import jax.numpy as jnp
from jax import lax
from jax.experimental import pallas as pl
from jax.experimental.pallas import tpu as pltpu

N_HEADS = 16
N_KV_HEADS = 4
GROUP = N_HEADS // N_KV_HEADS
HEAD_DIM = 64
ROT_DIM = HEAD_DIM // 4
ROPE_THETA = 500000.0
CMP_LEN = 32
CMP_STRIDE = 16
SEL_BLOCK = 64
N_SELECT = 16
WINDOW = 512
Q_TILE = 256
K_TILE = 512
CMP_CHUNK = 512
PAGE_SIZE = 128
CONV_WIDTH = 31
CONV_HALO = 32
CONV_ROWS = 128
SAMPLE_BATCH = 8
EPS = 1e-6
NEG = -1e30
BIG = 1e6
REMOVED = -3e38
LOG2E = 1.4426950408889634
KV_W = N_KV_HEADS * HEAD_DIM
LANES = 128
SUBLANES = 8
SUB_PER_PAGE = PAGE_SIZE // CMP_STRIDE
CMP_ROW = CMP_STRIDE * KV_W
BLK_PER_TILE = K_TILE // SEL_BLOCK
BLK_ROWS = 16
V_ROWS = HEAD_DIM + 16

MXU_DTYPE = jnp.bfloat16
VMEM_LIMIT = 56 * 1024 * 1024
F32 = jnp.float32


def _mm(a, b):
    return jnp.dot(a.astype(MXU_DTYPE), b.astype(MXU_DTYPE), preferred_element_type=F32)


def _mm_nt(a, b):
    return lax.dot_general(a.astype(MXU_DTYPE), b.astype(MXU_DTYPE), (((1,), (1,)), ((), ())),
                           preferred_element_type=F32)


def _sigmoid(x):
    return 1.0 / (1.0 + jnp.exp(-x))


def _const_spec(shape):
    nd = len(shape)
    return pl.BlockSpec(shape, lambda *_: (0,) * nd, pipeline_mode=pl.Buffered(1))


def _params(*sem):
    return pltpu.CompilerParams(dimension_semantics=sem, vmem_limit_bytes=VMEM_LIMIT)


def _ff_chunk(f):
    best = LANES
    for k in range(1, f // LANES + 1):
        c = k * LANES
        if f % c == 0 and c <= 1408:
            best = c
    return best


def _ffn_kernel(*refs, fc, final):
    if final:
        x_ref, g_ref, wg_ref, wu_ref, wd_ref, gf_ref, o_ref = refs
    else:
        x_ref, g_ref, wg_ref, wu_ref, wd_ref, o_ref = refs
    x = x_ref[...]
    xn = x * lax.rsqrt(jnp.mean(x * x, -1, keepdims=True) + EPS) * g_ref[...]
    xb = xn.astype(MXU_DTYPE)
    acc = None
    for c in range(wg_ref.shape[1] // fc):
        sl = slice(c * fc, (c + 1) * fc)
        gate = _mm(xb, wg_ref[:, sl])
        up = _mm(xb, wu_ref[:, sl])
        d = _mm(gate * _sigmoid(gate) * up, wd_ref[sl, :])
        acc = d if acc is None else acc + d
    h = x + 0.5 * acc
    if final:
        h = h * lax.rsqrt(jnp.mean(h * h, -1, keepdims=True) + EPS) * gf_ref[...]
    o_ref[...] = h


def _ffn(x, g, wg, wu, wd, gfin=None, tm=512):
    r, d = x.shape
    f = wg.shape[1]
    tm = min(tm, r)
    row = pl.BlockSpec((tm, d), lambda i: (i, 0))
    in_specs = [row, _const_spec((1, d)), _const_spec((d, f)), _const_spec((d, f)), _const_spec((f, d))]
    args = [x, g.reshape(1, d), wg, wu, wd]
    if gfin is not None:
        in_specs.append(_const_spec((1, d)))
        args.append(gfin.reshape(1, d))
    return pl.pallas_call(
        functools.partial(_ffn_kernel, fc=_ff_chunk(f), final=gfin is not None),
        grid=(r // tm,), in_specs=in_specs, out_specs=row,
        out_shape=jax.ShapeDtypeStruct((r, d), F32),
        compiler_params=_params("parallel"), name="ffn")(*args)


def _rope_tables(pos):
    half = ROT_DIM // 2
    inv = ROPE_THETA ** (-jnp.arange(half, dtype=F32) / half)
    ang = pos.astype(F32)[:, None] * inv[None, :]
    cos, sin = jnp.cos(ang), jnp.sin(ang)
    t = pos.shape[0]
    rest = HEAD_DIM - ROT_DIM
    c = jnp.concatenate([cos, cos, jnp.ones((t, rest), F32)], 1)
    s1 = jnp.concatenate([jnp.zeros((t, half), F32), sin, jnp.zeros((t, rest), F32)], 1)
    s2 = jnp.concatenate([-sin, jnp.zeros((t, half + rest), F32)], 1)
    rep = LANES // HEAD_DIM
    return jnp.tile(c, (1, rep)), jnp.tile(s1, (1, rep)), jnp.tile(s2, (1, rep))


def _rope(z, c, s1, s2):
    out = []
    for j in range(z.shape[1] // LANES):
        x = z[:, j * LANES:(j + 1) * LANES]
        out.append(x * c + pltpu.roll(x, ROT_DIM // 2, 1) * s1 + pltpu.roll(x, LANES - ROT_DIM // 2, 1) * s2)
    return jnp.concatenate(out, axis=1)


def _head_tiles(z, extra=None):
    low = lax.broadcasted_iota(jnp.int32, (1, LANES), 1) < HEAD_DIM
    tiles = []
    for h in range(N_KV_HEADS):
        pair = z[:, (h // 2) * LANES:(h // 2 + 1) * LANES]
        if h % 2:
            pair = pltpu.roll(pair, HEAD_DIM, 1)
        tiles.append(jnp.where(low, pair, 0.0 if extra is None else extra))
    return tiles


def _inproj_kernel(h_ref, g_ref, w_ref, c_ref, s1_ref, s2_ref, *outs, offs, sample, qscale, nper):
    x = h_ref[...]
    tm = x.shape[0]
    xb = (x * lax.rsqrt(jnp.mean(x * x, -1, keepdims=True) + EPS) * g_ref[...]).astype(MXU_DTYPE)
    c, s1, s2 = c_ref[...], s1_ref[...], s2_ref[...]

    def seg(i):
        return _mm(xb, w_ref[:, offs[i]:offs[i + 1]])

    glu = seg(0)
    dc = glu.shape[1] // 2
    v = glu[:, :dc] * _sigmoid(glu[:, dc:])
    q = _rope(seg(1), c, s1, s2) * qscale
    kc, vc = seg(2), seg(3)
    ks, vs = _rope(seg(4), c, s1, s2), seg(5)
    kw, vw = _rope(seg(6), c, s1, s2), seg(7)
    ng = _sigmoid(seg(8))
    mg = _sigmoid(seg(9))
    if sample:
        (v_ref, q_ref, kct_ref, vct_ref, kst_ref, vst_ref, kwt_ref, vwt_ref,
         ks_ref, vs_ref, kw_ref, vw_ref, ng_ref, mg_ref) = outs
        q_ref[...] = q
        for ref, val in ((ks_ref, ks), (vs_ref, vs), (kw_ref, kw), (vw_ref, vw)):
            ref[...] = val
        ng_ref[...] = ng
    else:
        (v_ref, qt_ref, kct_ref, vct_ref, kst_ref, vst_ref, kwt_ref, vwt_ref,
         ksa_ref, kwa_ref, vstb_ref, vwtb_ref, ngt_ref, mg_ref) = outs
        qt_ref[...] = q.T.astype(MXU_DTYPE)
        pos = (pl.program_id(0) % nper) * tm + lax.broadcasted_iota(jnp.int32, (tm, 1), 0)
        blk_in_tile = lax.shift_right_logical(pos, SEL_BLOCK.bit_length() - 1) & (BLK_PER_TILE - 1)
        lane = lax.broadcasted_iota(jnp.int32, (1, LANES), 1)
        onehot = jnp.where(lane - HEAD_DIM == blk_in_tile, 1.0, 0.0)
        for h, tile in enumerate(_head_tiles(ks, onehot)):
            ksa_ref[h] = tile.astype(MXU_DTYPE)
        for h, tile in enumerate(_head_tiles(kw)):
            kwa_ref[h] = tile.astype(MXU_DTYPE)
        ngt_ref[...] = ng.T[0:3 * N_HEADS, :]
    v_ref[...] = v
    mg_ref[...] = mg
    kct_ref[...] = kc.T
    vct_ref[...] = vc.T
    kst_ref[...] = ks.T
    kwt_ref[...] = kw.T
    vst = vs.T
    vwt = vw.T
    vst_ref[...] = vst
    vwt_ref[...] = vwt
    if not sample:
        ones = jnp.where(lax.broadcasted_iota(jnp.int32, (V_ROWS - HEAD_DIM, tm), 0) == 0, 1.0, 0.0)
        for h in range(N_KV_HEADS):
            rs = slice(h * HEAD_DIM, (h + 1) * HEAD_DIM)
            vstb_ref[h] = jnp.concatenate([vst[rs], ones], axis=0).astype(MXU_DTYPE)
            vwtb_ref[h] = jnp.concatenate([vwt[rs], ones], axis=0).astype(MXU_DTYPE)


def _inproj(h, g, w_pad, offs, tables, nseq, rows_per_seq, sample, qscale, tm=512):
    r, d = h.shape
    tm = min(tm, rows_per_seq)
    nper = rows_per_seq // tm
    nat = lambda w, dt=F32: (pl.BlockSpec((tm, w), lambda i: (i, 0)), jax.ShapeDtypeStruct((r, w), dt))
    tr = lambda w, dt=F32: (pl.BlockSpec((None, w, tm), lambda i: (i // nper, 0, i % nper)),
                            jax.ShapeDtypeStruct((nseq, w, rows_per_seq), dt))
    aug = (pl.BlockSpec((None, N_KV_HEADS, tm, LANES), lambda i: (i // nper, 0, i % nper, 0)),
           jax.ShapeDtypeStruct((nseq, N_KV_HEADS, rows_per_seq, LANES), MXU_DTYPE))
    tab = pl.BlockSpec((tm, LANES), lambda i: (i % nper, 0))
    dc = (offs[1] - offs[0]) // 2
    dq = offs[2] - offs[1]
    planes = [tr(KV_W)] * 6
    if sample:
        outs = [nat(dc), nat(dq)] + planes + [nat(KV_W)] * 4 + [nat(LANES), nat(offs[10] - offs[9])]
    else:
        vplane = (pl.BlockSpec((None, N_KV_HEADS, V_ROWS, tm), lambda i: (i // nper, 0, 0, i % nper)),
                  jax.ShapeDtypeStruct((nseq, N_KV_HEADS, V_ROWS, rows_per_seq), MXU_DTYPE))
        outs = ([nat(dc), tr(dq, MXU_DTYPE)] + planes + [aug, aug, vplane, vplane,
                tr(3 * N_HEADS), nat(offs[10] - offs[9])])
    return pl.pallas_call(
        functools.partial(_inproj_kernel, offs=offs, sample=sample, qscale=qscale, nper=nper),
        grid=(r // tm,),
        in_specs=[pl.BlockSpec((tm, d), lambda i: (i, 0)), _const_spec((1, d)), _const_spec(w_pad.shape),
                  tab, tab, tab],
        out_specs=[o[0] for o in outs], out_shape=[o[1] for o in outs],
        compiler_params=_params("parallel"), name="inproj")(h, g.reshape(1, d), w_pad, *tables)


def _conv_tail(y, b_ref, lg_ref, lb_ref, wo_ref):
    y = y + b_ref[...]
    mu = jnp.mean(y, -1, keepdims=True)
    yc = y - mu
    var = jnp.mean(yc * yc, -1, keepdims=True)
    yn = yc * lax.rsqrt(var + EPS) * lg_ref[...] + lb_ref[...]
    return _mm(yn * _sigmoid(yn), wo_ref[...])


def _conv_kernel(main_ref, halo_ref, w_ref, b_ref, lg_ref, lb_ref, wo_ref, o_ref, ext_ref, sh_ref, y_ref):
    ts, c = main_ref.shape
    ext_ref[0:CONV_HALO, :] = jnp.where(pl.program_id(1) == 0, 0.0, halo_ref[...])
    ext_ref[CONV_HALO:CONV_HALO + ts, :] = main_ref[...]
    lead = CONV_HALO - (CONV_WIDTH - 1)
    n_sh = ts + CONV_HALO - SUBLANES
    for r in range(1, SUBLANES):
        sh_ref[r - 1] = ext_ref[r:r + n_sh, :]

    def tap_rows(k, r0, cs):
        off = lead + k
        r = off % SUBLANES
        base = r0 + off - r
        src = ext_ref if r == 0 else sh_ref.at[r - 1]
        return src[base:base + CONV_ROWS, cs]

    for cb in range(c // LANES):
        cs = slice(cb * LANES, (cb + 1) * LANES)
        taps = [w_ref[k:k + 1, cs] for k in range(CONV_WIDTH)]
        for r0 in range(0, ts, CONV_ROWS):
            acc = tap_rows(0, r0, cs) * taps[0]
            for k in range(1, CONV_WIDTH):
                acc = acc + tap_rows(k, r0, cs) * taps[k]
            y_ref[r0:r0 + CONV_ROWS, cs] = acc
    o_ref[...] = _conv_tail(y_ref[...], b_ref, lg_ref, lb_ref, wo_ref)


def _conv(v, w, b, lg, lb, wo, ts=512):
    bsz, l, c = v.shape
    d = wo.shape[1]
    step = ts // CONV_HALO
    vec = lambda a: a.reshape(1, c)
    return pl.pallas_call(
        _conv_kernel,
        grid=(bsz, l // ts),
        in_specs=[pl.BlockSpec((None, ts, c), lambda bi, i: (bi, i, 0)),
                  pl.BlockSpec((None, CONV_HALO, c), lambda bi, i: (bi, jnp.maximum(i * step - 1, 0), 0)),
                  _const_spec((CONV_WIDTH, c)), _const_spec((1, c)), _const_spec((1, c)), _const_spec((1, c)),
                  _const_spec((c, d))],
        out_specs=pl.BlockSpec((None, ts, d), lambda bi, i: (bi, i, 0)),
        out_shape=jax.ShapeDtypeStruct((bsz, l, d), F32),
        scratch_shapes=[pltpu.VMEM((ts + CONV_HALO, c), F32),
                        pltpu.VMEM((SUBLANES - 1, ts + CONV_HALO - SUBLANES, c), F32), pltpu.VMEM((ts, c), F32)],
        compiler_params=_params("parallel", "parallel"), name="conv")(v, v, w, vec(b), vec(lg), vec(lb), wo)


def _conv_sample_kernel(st_ref, v_ref, w_ref, b_ref, lg_ref, lb_ref, wo_ref, o_ref):
    acc = v_ref[...] * w_ref[CONV_WIDTH - 1:CONV_WIDTH, :]
    for k in range(CONV_WIDTH - 1):
        acc = acc + st_ref[k] * w_ref[k:k + 1, :]
    o_ref[...] = _conv_tail(acc, b_ref, lg_ref, lb_ref, wo_ref)


def _conv_sample(st, v, w, b, lg, lb, wo):
    rows, c = v.shape
    d = wo.shape[1]
    vec = lambda a: a.reshape(1, c)
    args = (st, v, w, vec(b), vec(lg), vec(lb), wo)
    return pl.pallas_call(
        _conv_sample_kernel, grid=(1,),
        in_specs=[_const_spec(a.shape) for a in args],
        out_specs=pl.BlockSpec((rows, d), lambda i: (0, 0)),
        out_shape=jax.ShapeDtypeStruct((rows, d), F32),
        compiler_params=_params("arbitrary"), name="conv_sample")(*args)


def _compress_kernel(pt_ref, x_hbm, w1_ref, pe_ref, w1f_ref, b1_ref, w2_ref, w2h_ref, nat_ref, tr_ref,
                     buf, tbuf, fsbuf, sem, *, pc, n_pages, paged):
    s = pl.program_id(0)
    ci = pl.program_id(1)
    n_chunks = pl.num_programs(1)
    step = s * n_chunks + ci
    slot = step % 2
    m = pc * SUB_PER_PAGE
    nrow = m + SUB_PER_PAGE
    pairs = N_KV_HEADS // 2

    def page_copy(s_, ci_, slot_, p):
        pg = pt_ref[s_ * n_pages + jnp.minimum(ci_ * pc + p, n_pages - 1)]
        if paged:
            src = x_hbm.at[pg]
        else:
            src = x_hbm.at[s_, :, :, pl.ds(pl.multiple_of(pg * PAGE_SIZE, PAGE_SIZE), PAGE_SIZE)]
        return pltpu.make_async_copy(src, buf.at[slot_, p], sem.at[slot_])

    @pl.when(step == 0)
    def _():
        for p in range(pc + 1):
            page_copy(s, ci, slot, p).start()

    @pl.when(step + 1 < pl.num_programs(0) * n_chunks)
    def _():
        wrap = ci + 1 == n_chunks
        for p in range(pc + 1):
            page_copy(jnp.where(wrap, s + 1, s), jnp.where(wrap, 0, ci + 1), 1 - slot, p).start()

    for p in range(pc + 1):
        page_copy(s, ci, slot, p).wait()

    for p in range(pc + 1):
        for j in range(pairs):
            plane = buf[slot, p, 2 * j:2 * j + 2].reshape(2 * HEAD_DIM, PAGE_SIZE)
            tbuf[j, p * PAGE_SIZE:(p + 1) * PAGE_SIZE, :] = plane.T
    for j in range(pairs):
        acc = None
        for lp in range(CMP_STRIDE // 2):
            lhs = jnp.concatenate([tbuf[j, pl.ds(2 * lp + i, nrow, stride=CMP_STRIDE), :] for i in range(2)], axis=1)
            d = _mm(lhs, w1_ref[lp])
            acc = d if acc is None else acc + d
        fsbuf[j] = acc
    c = _mm(pe_ref[...], w1f_ref[...])[0:1, :] + b1_ref[...]
    nh = c.shape[1]
    parts = []
    for h in range(N_KV_HEADS):
        col = (h % 2) * 2 * nh
        parts.append(fsbuf[h // 2, 0:m, col:col + nh] + fsbuf[h // 2, pl.ds(1, m), col + nh:col + 2 * nh] + c)
    hid = jnp.concatenate(parts, axis=1)
    act = (hid * _sigmoid(hid)).astype(MXU_DTYPE)
    tr_ref[...] = _mm(act, w2_ref[...]).T.astype(MXU_DTYPE)
    for h in range(N_KV_HEADS):
        nat_ref[h] = _mm(act, w2h_ref[h]).astype(MXU_DTYPE)


def _compress(x, page_table, weights, pc, paged):
    w1_big, pe8, w1_flat, b1, w2_big, w2_heads = weights
    n_seq, n_pages = page_table.shape
    m = pc * SUB_PER_PAGE
    n_sub = n_pages * SUB_PER_PAGE
    hidden = b1.shape[0]
    grid_spec = pltpu.PrefetchScalarGridSpec(
        num_scalar_prefetch=1,
        grid=(n_seq, n_pages // pc),
        in_specs=[pl.BlockSpec(memory_space=pl.ANY),
                  _const_spec(w1_big.shape), _const_spec(pe8.shape), _const_spec(w1_flat.shape),
                  _const_spec((1, hidden)), _const_spec(w2_big.shape), _const_spec(w2_heads.shape)],
        out_specs=[pl.BlockSpec((None, N_KV_HEADS, m, LANES), lambda s, ci, pt: (s, 0, ci, 0)),
                   pl.BlockSpec((None, KV_W, m), lambda s, ci, pt: (s, 0, ci))],
        scratch_shapes=[pltpu.VMEM((2, pc + 1, N_KV_HEADS, HEAD_DIM, PAGE_SIZE), F32),
                        pltpu.VMEM((N_KV_HEADS // 2, (pc + 1) * PAGE_SIZE, LANES), F32),
                        pltpu.VMEM((N_KV_HEADS // 2, m + SUB_PER_PAGE, w1_big.shape[2]), F32),
                        pltpu.SemaphoreType.DMA((2,))])
    return pl.pallas_call(
        functools.partial(_compress_kernel, pc=pc, n_pages=n_pages, paged=paged),
        grid_spec=grid_spec,
        out_shape=[jax.ShapeDtypeStruct((n_seq, N_KV_HEADS, n_sub, LANES), MXU_DTYPE),
                   jax.ShapeDtypeStruct((n_seq, KV_W, n_sub), MXU_DTYPE)],
        compiler_params=_params("arbitrary", "arbitrary"), name="compress")(
            page_table.reshape(-1), x, w1_big, pe8, w1_flat, b1.reshape(1, hidden), w2_big, w2_heads)


def _compress_weights(pe, w1, b1, w2):
    hidden = w1.shape[-1]
    eye = jnp.eye(N_KV_HEADS, dtype=F32)
    both = jnp.concatenate([w1[:CMP_STRIDE], w1[CMP_STRIDE:]], axis=-1)
    both = both.reshape(CMP_STRIDE // 2, 2, HEAD_DIM, 2 * hidden)
    w1_big = jnp.einsum('pidc,hg->pihdgc', both, jnp.eye(2, dtype=F32)).reshape(
        CMP_STRIDE // 2, 4 * HEAD_DIM, 4 * hidden)
    w2_big = jnp.einsum('kd,hg->hkgd', w2, eye).reshape(N_KV_HEADS * hidden, KV_W)
    w2_pad = jnp.pad(w2, ((0, 0), (0, LANES - HEAD_DIM)))
    w2_heads = jnp.einsum('kd,hg->ghkd', w2_pad, eye).reshape(N_KV_HEADS, N_KV_HEADS * hidden, LANES)
    pe8 = jnp.zeros((SUBLANES, CMP_LEN * HEAD_DIM), F32).at[0].set(pe.reshape(-1))
    return (w1_big.astype(MXU_DTYPE), pe8, w1.reshape(CMP_LEN * HEAD_DIM, hidden).astype(MXU_DTYPE), b1,
            w2_big.astype(MXU_DTYPE), w2_heads.astype(MXU_DTYPE))


def _overlap(n_cmp_pad, n_blk_pad):
    ci = np.arange(n_cmp_pad)[:, None] * CMP_STRIDE
    bs = np.arange(n_blk_pad)[None, :] * SEL_BLOCK
    return ((ci < bs + SEL_BLOCK) & (ci + CMP_LEN > bs)).astype(np.float32)


def _attn_kernel(q_ref, g_ref, kc_ref, vct_ref, ov_ref, ks_ref, vst_ref, kw_ref, vwt_ref, o_ref,
                 bias_ref, sc_ref, imp_ref, sa_ref, sb_ref, *, n_cmp, n_blk):
    s0 = pl.program_id(2) * Q_TILE
    cols = GROUP * Q_TILE
    qt = jnp.concatenate([q_ref[g] for g in range(GROUP)], axis=1)
    q0 = jnp.concatenate([qt, jnp.zeros((LANES - HEAD_DIM, cols), qt.dtype)], axis=0)
    tq = s0 + (lax.broadcasted_iota(jnp.int32, (1, cols), 1) & (Q_TILE - 1))
    tq1 = s0 + lax.broadcasted_iota(jnp.int32, (1, Q_TILE), 1)

    ncp = kc_ref.shape[0]
    ch = min(CMP_CHUNK, ncp)
    last_c = lax.shift_right_logical(s0 + (Q_TILE - CMP_LEN), CMP_STRIDE.bit_length() - 1)
    n_cc = jnp.minimum(last_c // ch + 1, ncp // ch)
    ci0 = lax.broadcasted_iota(jnp.int32, (ch, 1), 0)

    def cmp_scores(cc, m):
        r0 = pl.multiple_of(cc * ch, ch)
        c = ci0 + r0
        s = jnp.where((c * CMP_STRIDE + (CMP_LEN - 1) <= tq) & (c < n_cmp), _mm(kc_ref[pl.ds(r0, ch), :], q0), NEG)
        sc_ref[pl.ds(r0, ch), :] = s
        return jnp.maximum(m, jnp.max(s, axis=0, keepdims=True))

    m_c = lax.fori_loop(0, n_cc, cmp_scores, jnp.full((1, cols), NEG, F32))
    imp_ref[...] = jnp.zeros(imp_ref.shape, F32)

    def cmp_weights(cc, carry):
        l, acc = carry
        r0 = pl.multiple_of(cc * ch, ch)
        p = jnp.exp2(sc_ref[pl.ds(r0, ch), :] - m_c)
        pb = p.astype(MXU_DTYPE)
        imp_ref[...] += _mm(ov_ref[:, pl.ds(r0, ch)], pb)
        return l + jnp.sum(p, axis=0, keepdims=True), acc + _mm(vct_ref[:, pl.ds(r0, ch)], pb)

    l_c, acc_c = lax.fori_loop(0, n_cc, cmp_weights,
                               (jnp.zeros((1, cols), F32), jnp.zeros((HEAD_DIM, cols), F32)))
    inv_c = jnp.where(m_c > 0.5 * NEG, 1.0 / jnp.maximum(l_c, 1e-30), 0.0)
    o_c = acc_c * inv_c
    impn = imp_ref[...] * inv_c
    imp = impn[:, 0:Q_TILE]
    for g in range(1, GROUP):
        imp = imp + impn[:, g * Q_TILE:(g + 1) * Q_TILE]

    wk = WINDOW + Q_TILE
    w0 = pl.multiple_of(jnp.maximum(s0 - WINDOW, 0), Q_TILE)
    wpos = w0 + lax.broadcasted_iota(jnp.int32, (wk, 1), 0)
    sw = jnp.where((wpos <= tq) & (wpos > tq - WINDOW), _mm(kw_ref[pl.ds(w0, wk), :], q0), NEG)
    acc_w = _mm(vwt_ref[:, pl.ds(w0, wk)], jnp.exp2(sw - jnp.max(sw, axis=0, keepdims=True)))
    o_w = acc_w[0:HEAD_DIM] / acc_w[HEAD_DIM:HEAD_DIM + 1]

    blk = lax.broadcasted_iota(jnp.int32, (n_blk, 1), 0)
    blk_f = blk.astype(F32)
    cur = lax.shift_right_logical(tq1, SEL_BLOCK.bit_length() - 1)
    forced = (blk == 0) | (blk == cur) | (blk == cur - 1)
    valid = blk * SEL_BLOCK <= tq1
    n_top = min(N_SELECT, n_blk)

    def ranked_selection():
        score = jnp.where(valid, imp + jnp.where(forced, BIG, 0.0), -BIG)
        for _ in range(n_top):
            top = jnp.max(score, axis=0, keepdims=True)
            first = jnp.min(jnp.where(score == top, blk_f, float(n_blk)), axis=0, keepdims=True)
            score = jnp.where(blk_f == first, REMOVED, score)
        return score == REMOVED

    free = valid & jnp.logical_not(forced)
    cand = jnp.where(free, imp, REMOVED)
    n_forced = 1 + (cur >= 1).astype(jnp.int32) + (cur >= 2).astype(jnp.int32)
    n_free = jnp.sum(jnp.where(free, 1.0, 0.0), axis=0, keepdims=True)
    want = jnp.minimum((n_top - n_forced).astype(F32), n_free)
    thr = jnp.full((1, Q_TILE), jnp.inf, F32)
    kth = {}
    for r in range(1, n_top):
        thr = jnp.max(jnp.where(cand < thr, cand, REMOVED), axis=0, keepdims=True)
        kth[r] = thr
    thr = jnp.where(n_forced == 3, kth[max(n_top - 3, 1)], jnp.where(n_forced == 2, kth[max(n_top - 2, 1)],
                                                                      kth[n_top - 1]))
    picked = free & (cand >= thr)
    got = jnp.sum(jnp.where(picked, 1.0, 0.0), axis=0, keepdims=True)
    exact = jnp.max(jnp.abs(got - want)) == 0.0
    as_bias = lambda selected: jnp.where(selected & valid, 0.0, NEG)
    bias_ref[...] = lax.cond(exact, lambda: as_bias(picked | forced), lambda: as_bias(ranked_selection()))

    kpos0 = lax.broadcasted_iota(jnp.int32, (K_TILE, 1), 0)
    q_pad = jnp.zeros((LANES - HEAD_DIM - BLK_ROWS, cols), qt.dtype)
    b_pad = jnp.zeros((BLK_ROWS - BLK_PER_TILE, cols), F32)

    def sel_scores(j, s_ref):
        k0 = pl.multiple_of(j * K_TILE, K_TILE)
        b8 = bias_ref[pl.ds(pl.multiple_of(j * BLK_PER_TILE, BLK_PER_TILE), BLK_PER_TILE), :]
        rows = jnp.concatenate([b8] * GROUP, axis=1)
        if BLK_ROWS > BLK_PER_TILE:
            rows = jnp.concatenate([rows, b_pad], axis=0)
        rows = rows.astype(qt.dtype)
        s_ref[...] = _mm(ks_ref[pl.ds(k0, K_TILE), :], jnp.concatenate([qt, rows, q_pad], axis=0))

    def sel_update(j, s_ref, carry, causal=False):
        v = vst_ref[:, pl.ds(pl.multiple_of(j * K_TILE, K_TILE), K_TILE)]
        out = []
        for g in range(GROUP):
            m, acc = carry[g]
            s = s_ref[:, g * Q_TILE:(g + 1) * Q_TILE]
            if causal:
                s = jnp.where(kpos0 + j * K_TILE <= tq1, s, NEG)
            m_new = jnp.maximum(m, jnp.max(s, axis=0, keepdims=True))
            out.append((m_new, jnp.exp2(m - m_new) * acc + _mm(v, jnp.exp2(s - m_new))))
        return tuple(out)

    def sel_pair(i, carry):
        sel_scores(2 * i + 1, sb_ref)
        carry = sel_update(2 * i, sa_ref, carry)
        sel_scores(2 * i + 2, sa_ref)
        return sel_update(2 * i + 1, sb_ref, carry)

    last = (s0 + Q_TILE - 1) // K_TILE
    n_pairs = last // 2
    sel_scores(0, sa_ref)
    init = tuple((jnp.full((1, Q_TILE), NEG, F32), jnp.zeros((V_ROWS, Q_TILE), F32)) for _ in range(GROUP))
    carry = lax.fori_loop(0, n_pairs, sel_pair, init)
    ta = 2 * n_pairs

    def two_left(c):
        sel_scores(last, sb_ref)
        return sel_update(last, sb_ref, sel_update(ta, sa_ref, c, causal=True), causal=True)

    carry = lax.cond(ta < last, two_left, lambda c: sel_update(ta, sa_ref, c, causal=True), carry)
    acc_s = jnp.concatenate([acc for _, acc in carry], axis=1)
    o_s = acc_s[0:HEAD_DIM] / acc_s[HEAD_DIM:HEAD_DIM + 1]

    gate = lambda br: jnp.concatenate([g_ref[g * 3 + br:g * 3 + br + 1, :] for g in range(GROUP)], axis=1)
    o = gate(0) * o_c + gate(1) * o_s + gate(2) * o_w
    for g in range(GROUP):
        o_ref[g] = o[:, g * Q_TILE:(g + 1) * Q_TILE]


def _attn_prompt(qt, gt, kc, vct, ks, vst, kw, vwt, n_cmp):
    b, h, grp, hd, s = qt.shape
    ncp = kc.shape[2]
    n_blk = s // SEL_BLOCK
    nt = s // Q_TILE
    cols = grp * Q_TILE
    ov = jnp.asarray(_overlap(ncp, n_blk).T, MXU_DTYPE)
    one = pl.Buffered(1)
    rows = lambda n: pl.BlockSpec((None, None, n, LANES), lambda bi, hi, t: (bi, hi, 0, 0), pipeline_mode=one)
    plane = lambda n: pl.BlockSpec((None, hd, n), lambda bi, hi, t: (bi, hi, 0), pipeline_mode=one)
    vplane = pl.BlockSpec((None, None, V_ROWS, s), lambda bi, hi, t: (bi, hi, 0, 0), pipeline_mode=one)
    qspec = pl.BlockSpec((None, None, grp, hd, Q_TILE), lambda bi, hi, t: (bi, hi, 0, 0, t))
    return pl.pallas_call(
        functools.partial(_attn_kernel, n_cmp=n_cmp, n_blk=n_blk),
        grid=(b, h, nt),
        in_specs=[qspec, pl.BlockSpec((None, None, 3 * grp, Q_TILE), lambda bi, hi, t: (bi, hi, 0, t)),
                  rows(ncp), plane(ncp), _const_spec(ov.shape), rows(s), vplane, rows(s), vplane],
        out_specs=qspec,
        out_shape=jax.ShapeDtypeStruct((b, h, grp, hd, s), F32),
        scratch_shapes=[pltpu.VMEM((n_blk, Q_TILE), F32), pltpu.VMEM((ncp, cols), F32),
                        pltpu.VMEM((n_blk, cols), F32), pltpu.VMEM((K_TILE, cols), F32),
                        pltpu.VMEM((K_TILE, cols), F32)],
        compiler_params=_params("parallel", "parallel", "arbitrary"), name="attn_prompt")(
            qt, gt, kc, vct, ov, ks, vst, kw, vwt)


def _sample_cmp_kernel(q_ref, kct_ref, vc_ref, ov_ref, oc_ref, idx_ref, *, n_cmp, n_blk, t):
    nb = q_ref.shape[0]
    ncp = kct_ref.shape[2]
    nbp = ov_ref.shape[1]
    ci = lax.broadcasted_iota(jnp.int32, (1, ncp), 1)
    mask = (ci * CMP_STRIDE + (CMP_LEN - 1) <= t) & (ci < n_cmp)
    sums = []
    for b in range(nb):
        for h in range(N_KV_HEADS):
            s = jnp.where(mask, _mm(q_ref[b, h], kct_ref[b, h * HEAD_DIM:(h + 1) * HEAD_DIM, :]), NEG)
            m = jnp.max(s, axis=1, keepdims=True)
            p = jnp.where(mask, jnp.exp(s - m), 0.0)
            p = p / jnp.maximum(jnp.sum(p, axis=1, keepdims=True), 1e-30)
            oc_ref[b, h] = _mm(p, vc_ref[b, h])
            sums.append(jnp.sum(p, axis=0, keepdims=True))
    imp = _mm(jnp.concatenate(sums, axis=0), ov_ref[...])
    blk = lax.broadcasted_iota(jnp.int32, (1, nbp), 1)
    cur = t // SEL_BLOCK
    forced = (blk == 0) | (blk == cur) | (blk == cur - 1)
    score = jnp.where(blk * SEL_BLOCK <= t, imp + jnp.where(forced, BIG, 0.0), -BIG)
    score = jnp.where(blk < n_blk, score, REMOVED)
    blk_f = blk.astype(F32)
    lane = lax.broadcasted_iota(jnp.int32, (1, LANES), 1)
    out = jnp.zeros((nb * N_KV_HEADS, LANES), F32)
    for r in range(min(N_SELECT, n_blk)):
        top = jnp.max(score, axis=1, keepdims=True)
        first = jnp.min(jnp.where(score == top, blk_f, float(nbp)), axis=1, keepdims=True)
        out = jnp.where(lane == r, first, out)
        score = jnp.where(blk_f == first, REMOVED, score)
    idx_ref[...] = out.astype(jnp.int32)


def _sample_cmp(q4, kct, vc, n_cmp, n_blk, t):
    db, _, _, hd = q4.shape
    ncp = kct.shape[2]
    nbp = -(-n_blk // LANES) * LANES
    nb = SAMPLE_BATCH if db % SAMPLE_BATCH == 0 else 1
    ov = jnp.asarray(_overlap(ncp, nbp), MXU_DTYPE)
    per = lambda r, w: pl.BlockSpec((nb, N_KV_HEADS, r, w), lambda b: (b, 0, 0, 0))
    oc, idx = pl.pallas_call(
        functools.partial(_sample_cmp_kernel, n_cmp=n_cmp, n_blk=n_blk, t=t),
        grid=(db // nb,),
        in_specs=[per(GROUP, hd), pl.BlockSpec((nb, KV_W, ncp), lambda b: (b, 0, 0)), per(ncp, LANES),
                  _const_spec(ov.shape)],
        out_specs=[per(GROUP, LANES), pl.BlockSpec((None, nb * N_KV_HEADS, LANES), lambda b: (b, 0, 0))],
        out_shape=[jax.ShapeDtypeStruct((db, N_KV_HEADS, GROUP, LANES), F32),
                   jax.ShapeDtypeStruct((db // nb, nb * N_KV_HEADS, LANES), jnp.int32)],
        compiler_params=_params("parallel"), name="sample_cmp")(q4, kct, vc, ov)
    return oc, idx.reshape(db, N_KV_HEADS, LANES)


def _sample_sel_kernel(idx_ref, pt_ref, q_ref, oc_ref, g_ref, ksn_ref, vsn_ref, kwn_ref, vwn_ref, wk_ref, wv_ref,
                       pk_hbm, pv_hbm, o_ref, kbuf, vbuf, sem, *, n_past_blk, n_pages, nsel, t):
    b = pl.program_id(0)
    sub = PAGE_SIZE // SEL_BLOCK

    slot = b % 2

    def sel_block(b_, h, n):
        return idx_ref[(b_ * N_KV_HEADS + h) * nsel + n]

    def page_copies(b_, slot_, h, n):
        pg = pt_ref[b_ * n_pages + jnp.minimum(sel_block(b_, h, n), n_past_blk - 1) // sub]
        dst = pl.ds(n * PAGE_SIZE, PAGE_SIZE)
        return (pltpu.make_async_copy(pk_hbm.at[pg, h], kbuf.at[slot_, h, :, dst], sem.at[0, slot_]),
                pltpu.make_async_copy(pv_hbm.at[pg, h], vbuf.at[slot_, h, :, dst], sem.at[1, slot_]))

    def all_copies(b_, slot_, go):
        for h in range(N_KV_HEADS):
            for n in range(nsel):
                for cp in page_copies(b_, slot_, h, n):
                    go(cp)

    @pl.when(b == 0)
    def _():
        all_copies(b, slot, lambda cp: cp.start())

    @pl.when(b + 1 < pl.num_programs(0))
    def _():
        all_copies(b + 1, 1 - slot, lambda cp: cp.start())

    all_copies(b, slot, lambda cp: cp.wait())

    lane = lax.broadcasted_iota(jnp.int32, (1, nsel * PAGE_SIZE), 1)
    seg = lax.shift_right_logical(lane, PAGE_SIZE.bit_length() - 1)
    half = lax.shift_right_logical(lane, SEL_BLOCK.bit_length() - 1) & (sub - 1)
    wl = wk_ref.shape[2]
    wpos = t - wl + lax.broadcasted_iota(jnp.int32, (1, wl), 1)
    okw = (wpos > t - WINDOW) & (wpos >= 0)
    for h in range(N_KV_HEADS):
        q = q_ref[h]
        want = jnp.full(lane.shape, -1, jnp.int32)
        for n in range(nsel):
            blk = sel_block(b, h, n)
            want = jnp.where(seg == n, jnp.where(blk < n_past_blk, blk % sub, -1), want)
        ok = half == want
        s = jnp.where(ok, _mm(q, kbuf[slot, h]), NEG)
        s_new = jnp.sum(q * ksn_ref[h:h + 1, :], axis=1, keepdims=True)
        m = jnp.maximum(jnp.max(s, axis=1, keepdims=True), s_new)
        p = jnp.where(ok, jnp.exp(s - m), 0.0)
        p_new = jnp.exp(s_new - m)
        o_s = ((_mm_nt(p, vbuf[slot, h]) + p_new * vsn_ref[h:h + 1, :])
               / (jnp.sum(p, axis=1, keepdims=True) + p_new))
        sw = jnp.where(okw, _mm(q, wk_ref[h]), NEG)
        sw_new = jnp.sum(q * kwn_ref[h:h + 1, :], axis=1, keepdims=True)
        mw = jnp.maximum(jnp.max(sw, axis=1, keepdims=True), sw_new)
        pw = jnp.where(okw, jnp.exp(sw - mw), 0.0)
        pw_new = jnp.exp(sw_new - mw)
        o_w = (_mm_nt(pw, wv_ref[h]) + pw_new * vwn_ref[h:h + 1, :]) / (jnp.sum(pw, axis=1, keepdims=True) + pw_new)
        g = g_ref[h]
        o_ref[h] = g[:, 0:1] * oc_ref[h][:, 0:HEAD_DIM] + g[:, 1:2] * o_s + g[:, 2:3] * o_w


def _sample_sel(idx, page_table, q4, oc, gates, ks_new, vs_new, kw_new, vw_new, win_k, win_v, pool_k, pool_v, t):
    db, _, grp, hd = q4.shape
    nsel = idx.shape[2]
    n_pages = page_table.shape[1]
    n_past_blk = n_pages * (PAGE_SIZE // SEL_BLOCK)
    wl = win_k.shape[3]
    per = lambda *shape: pl.BlockSpec((None,) + shape, lambda b, *_: (b,) + (0,) * len(shape))
    hbm = pl.BlockSpec(memory_space=pl.ANY)
    grid_spec = pltpu.PrefetchScalarGridSpec(
        num_scalar_prefetch=2,
        grid=(db,),
        in_specs=[per(N_KV_HEADS, grp, hd), per(N_KV_HEADS, grp, LANES), per(N_KV_HEADS, grp, LANES)]
                 + [per(N_KV_HEADS, hd)] * 4 + [per(N_KV_HEADS, hd, wl)] * 2 + [hbm, hbm],
        out_specs=per(N_KV_HEADS, grp, hd),
        scratch_shapes=[pltpu.VMEM((2, N_KV_HEADS, hd, nsel * PAGE_SIZE), F32),
                        pltpu.VMEM((2, N_KV_HEADS, hd, nsel * PAGE_SIZE), F32),
                        pltpu.SemaphoreType.DMA((2, 2))])
    return pl.pallas_call(
        functools.partial(_sample_sel_kernel, n_past_blk=n_past_blk, n_pages=n_pages, nsel=nsel, t=t),
        grid_spec=grid_spec,
        out_shape=jax.ShapeDtypeStruct((db, N_KV_HEADS, grp, hd), F32),
        compiler_params=_params("arbitrary"), name="sample_sel")(
            idx.reshape(-1), page_table.reshape(-1), q4, oc, gates, ks_new, vs_new, kw_new, vw_new,
            win_k, win_v, pool_k, pool_v)


def _merge_kernel(h_ref, yc_ref, ya_ref, mg_ref, wa_ref, wo_ref, o_ref, *, planes):
    d = h_ref.shape[1]
    ya = ya_ref[...].T if planes else ya_ref[...]
    m = mg_ref[:, :d] * yc_ref[...] + mg_ref[:, d:] * _mm(ya, wa_ref[...])
    o_ref[...] = h_ref[...] + _mm(m, wo_ref[...])


def _merge(h, y_conv, y_attn, mg, wa, wo, rows_per_seq, planes, tm=512):
    r, d = h.shape
    tm = min(tm, rows_per_seq)
    nper = rows_per_seq // tm
    row = lambda w: pl.BlockSpec((tm, w), lambda i: (i, 0))
    da = wa.shape[0]
    ya_spec = pl.BlockSpec((None, da, tm), lambda i: (i // nper, 0, i % nper)) if planes else row(da)
    return pl.pallas_call(
        functools.partial(_merge_kernel, planes=planes), grid=(r // tm,),
        in_specs=[row(d), row(d), ya_spec, row(2 * d), _const_spec(wa.shape), _const_spec(wo.shape)],
        out_specs=row(d), out_shape=jax.ShapeDtypeStruct((r, d), F32),
        compiler_params=_params("parallel"), name="merge")(h, y_conv, y_attn, mg, wa, wo)


def kernel(x_prompt, x_sample, cache_cmp_k, cache_cmp_v, cache_sel_k, cache_sel_v, state_win_k, state_win_v, state_conv, page_table, norm_ffn1, ffn1_gate, ffn1_up, ffn1_down, norm_mix, w_in, conv_w, conv_b, conv_ln_g, conv_ln_b, w_conv_out, cmp_pe_k, cmp_w1_k, cmp_b1_k, cmp_w2_k, cmp_pe_v, cmp_w1_v, cmp_b1_v, cmp_w2_v, w_attn_out, w_out, norm_ffn2, ffn2_gate, ffn2_up, ffn2_down, norm_final):
    bsz, seq, d = x_prompt.shape
    db, dseq, _ = x_sample.shape
    depth = norm_ffn1.shape[0]
    assert depth == 1 and dseq == 1 and d == N_HEADS * HEAD_DIM
    assert seq % K_TILE == 0 and seq >= WINDOW + Q_TILE
    assert state_conv.shape[2] == CONV_WIDTH - 1
    n_pages = page_table.shape[1]
    past = n_pages * PAGE_SIZE
    d_conv = d // 2
    scale = HEAD_DIM ** -0.5
    cast = lambda w: w.astype(MXU_DTYPE)

    col_sizes = (2 * d_conv, N_HEADS * HEAD_DIM) + (KV_W,) * 6 + (3 * N_HEADS, 2 * d)
    starts = np.concatenate([[0], np.cumsum(col_sizes)])
    offs, pieces, pos = [0], [], 0
    for i, n in enumerate(col_sizes):
        pad = -n % LANES
        pieces.append(w_in[0][:, starts[i]:starts[i + 1]])
        if pad:
            pieces.append(jnp.zeros((d, pad), F32))
        pos += n + pad
        offs.append(pos)
    w_in_pad = cast(jnp.concatenate(pieces, axis=1))
    offs = tuple(offs)
    ffn1 = (norm_ffn1[0], cast(ffn1_gate[0]), cast(ffn1_up[0]), cast(ffn1_down[0]))
    ffn2 = (norm_ffn2[0], cast(ffn2_gate[0]), cast(ffn2_up[0]), cast(ffn2_down[0]))
    wa, wo, wco = cast(w_attn_out[0]), cast(w_out[0]), cast(w_conv_out[0])
    convp = (conv_w[0], conv_b[0], conv_ln_g[0], conv_ln_b[0], wco)
    cmpk = _compress_weights(cmp_pe_k[0], cmp_w1_k[0], cmp_b1_k[0], cmp_w2_k[0])
    cmpv = _compress_weights(cmp_pe_v[0], cmp_w1_v[0], cmp_b1_v[0], cmp_w2_v[0])
    planes4 = lambda x: x.reshape(x.shape[0], N_KV_HEADS, HEAD_DIM, x.shape[2])
    state5 = lambda x: jnp.transpose(planes4(x), (0, 3, 1, 2))[None]
    to_planes = lambda x: jnp.transpose(x, (0, 2, 3, 1))

    rp = bsz * seq
    h_p = _ffn(x_prompt.reshape(rp, d), *ffn1)
    (v_p, qt_p, kct_p, vct_p, kst_p, vst_p, kwt_p, vwt_p, ksa_p, kwa_p, vstb_p, vwtb_p, ngt_p, mg_p) = _inproj(
        h_p, norm_mix[0], w_in_pad, offs, _rope_tables(jnp.arange(seq, dtype=jnp.int32)), bsz, seq,
        sample=False, qscale=scale * LOG2E)
    y_conv_p = _conv(v_p.reshape(bsz, seq, d_conv), *convp).reshape(rp, d)
    pages_p = seq // PAGE_SIZE
    pc_p = min(32, pages_p)
    ident = jnp.tile(jnp.arange(pages_p, dtype=jnp.int32)[None], (bsz, 1))
    kc_rows, _ = _compress(planes4(kct_p), ident, cmpk, pc_p, paged=False)
    _, vc_planes = _compress(planes4(vct_p), ident, cmpv, pc_p, paged=False)
    ot = _attn_prompt(qt_p.reshape(bsz, N_KV_HEADS, GROUP, HEAD_DIM, seq),
                      ngt_p.reshape(bsz, N_KV_HEADS, 3 * GROUP, seq),
                      kc_rows, vc_planes, ksa_p, vstb_p, kwa_p, vwtb_p, seq // CMP_STRIDE - 1)
    h2_p = _merge(h_p, y_conv_p, ot.reshape(bsz, d, seq), mg_p, wa, wo, seq, planes=True)
    y_prompt = _ffn(h2_p, *ffn2, gfin=norm_final).reshape(bsz, seq, d)

    t = past
    h_s = _ffn(x_sample.reshape(db, d), *ffn1)
    (v_s, q_s, kct_s, vct_s, kst_s, vst_s, kwt_s, vwt_s, ks_s, vs_s, kw_s, vw_s, ng_s, mg_s) = _inproj(
        h_s, norm_mix[0], w_in_pad, offs, _rope_tables(jnp.full((db,), t, jnp.int32)), 1, db,
        sample=True, qscale=scale)
    y_conv_s = _conv_sample(jnp.transpose(state_conv[0], (1, 0, 2)), v_s, *convp)
    pc_s = min(32, n_pages)
    n_sub_s = (past + dseq) // CMP_STRIDE
    assert n_sub_s == n_pages * SUB_PER_PAGE
    _, kc_planes_s = _compress(to_planes(cache_cmp_k[0]), page_table, cmpk, pc_s, paged=True)
    vc_rows_s, _ = _compress(to_planes(cache_cmp_v[0]), page_table, cmpv, pc_s, paged=True)
    n_blk_s = -(-(past + dseq) // SEL_BLOCK)
    q4 = q_s.reshape(db, N_KV_HEADS, GROUP, HEAD_DIM)
    oc_s, idx_s = _sample_cmp(q4, kc_planes_s, vc_rows_s, n_sub_s - 1, n_blk_s, t)
    nsel = min(N_SELECT, n_blk_s)
    gates_s = jnp.pad(ng_s[:, :3 * N_HEADS].reshape(db, N_KV_HEADS, GROUP, 3),
                      ((0, 0), (0, 0), (0, 0), (0, LANES - 3)))
    new3 = lambda x: x.reshape(db, N_KV_HEADS, HEAD_DIM)
    y_attn_s = _sample_sel(idx_s[:, :, :nsel], page_table, q4, oc_s, gates_s,
                           new3(ks_s), new3(vs_s), new3(kw_s), new3(vw_s),
                           to_planes(state_win_k[0]), to_planes(state_win_v[0]),
                           to_planes(cache_sel_k[0]), to_planes(cache_sel_v[0]), t).reshape(db, d)
    h2_s = _merge(h_s, y_conv_s, y_attn_s, mg_s, wa, wo, db, planes=False)
    y_sample = _ffn(h2_s, *ffn2, gfin=norm_final).reshape(db, dseq, d)

    wl_p = min(WINDOW, seq)
    p_conv = v_p.reshape(bsz, seq, d_conv)[:, seq - (CONV_WIDTH - 1):][None]
    new5 = lambda x: jnp.transpose(x.reshape(N_KV_HEADS, HEAD_DIM, db), (2, 0, 1)).reshape(1, db, 1, N_KV_HEADS, HEAD_DIM)
    wl_s = state_win_k.shape[2]
    row4 = lambda x: x.reshape(db, 1, N_KV_HEADS, HEAD_DIM)
    s_win_k = jnp.concatenate([state_win_k[0], row4(kw_s)], axis=1)[:, -wl_s:][None]
    s_win_v = jnp.concatenate([state_win_v[0], row4(vw_s)], axis=1)[:, -wl_s:][None]
    s_conv = jnp.concatenate([state_conv[0], v_s[:, None, :]], axis=1)[:, -(CONV_WIDTH - 1):][None]
    return (y_prompt, y_sample,
            state5(kct_p), state5(vct_p), state5(kst_p), state5(vst_p),
            state5(kwt_p[:, :, seq - wl_p:]), state5(vwt_p[:, :, seq - wl_p:]), p_conv,
            new5(kct_s), new5(vct_s), new5(kst_s), new5(vst_s),
            s_win_k, s_win_v, s_conv)
```

```python
import functools

import numpy as np
import jax
import jax.numpy as jnp
from jax import lax
from jax.experimental import pallas as pl
from jax.experimental.pallas import tpu as pltpu

N_HEADS = 16
N_KV_HEADS = 4
GROUP = N_HEADS // N_KV_HEADS
HEAD_DIM = 64
ROT_DIM = HEAD_DIM // 4
ROPE_THETA = 500000.0
CMP_LEN = 32
CMP_STRIDE = 16
SEL_BLOCK = 64
N_SELECT = 16
WINDOW = 512
Q_TILE = 256
K_TILE = 512
CMP_CHUNK = 512
PAGE_SIZE = 128
CONV_WIDTH = 31
CONV_HALO = 32
CONV_ROWS = 128
SAMPLE_BATCH = 8
EPS = 1e-6
NEG = -1e30
BIG = 1e6
REMOVED = -3e38
LOG2E = 1.4426950408889634
KV_W = N_KV_HEADS * HEAD_DIM
LANES = 128
SUBLANES = 8
SUB_PER_PAGE = PAGE_SIZE // CMP_STRIDE
BLK_PER_TILE = K_TILE // SEL_BLOCK
BLK_ROWS = 16
V_ROWS = HEAD_DIM + 16

MXU_DTYPE = jnp.bfloat16
VMEM_LIMIT = 56 * 1024 * 1024
F32 = jnp.float32


def _mm(a, b):
    return jnp.dot(a.astype(MXU_DTYPE), b.astype(MXU_DTYPE), preferred_element_type=F32)


def _mm_nt(a, b):
    return lax.dot_general(a.astype(MXU_DTYPE), b.astype(MXU_DTYPE), (((1,), (1,)), ((), ())),
                           preferred_element_type=F32)


def _sigmoid(x):
    return 1.0 / (1.0 + jnp.exp(-x))


def _const_spec(shape):
    nd = len(shape)
    return pl.BlockSpec(shape, lambda *_: (0,) * nd, pipeline_mode=pl.Buffered(1))


def _params(*sem):
    return pltpu.CompilerParams(dimension_semantics=sem, vmem_limit_bytes=VMEM_LIMIT)


def _ff_chunk(f):
    best = LANES
    for k in range(1, f // LANES + 1):
        c = k * LANES
        if f % c == 0 and c <= 1408:
            best = c
    return best


def _ffn_kernel(*refs, fc, final):
    if final:
        x_ref, g_ref, wg_ref, wu_ref, wd_ref, gf_ref, o_ref = refs
    else:
        x_ref, g_ref, wg_ref, wu_ref, wd_ref, o_ref = refs
    x = x_ref[...]
    xn = x * lax.rsqrt(jnp.mean(x * x, -1, keepdims=True) + EPS) * g_ref[...]
    xb = xn.astype(MXU_DTYPE)
    acc = None
    for c in range(wg_ref.shape[1] // fc):
        sl = slice(c * fc, (c + 1) * fc)
        gate = _mm(xb, wg_ref[:, sl])
        up = _mm(xb, wu_ref[:, sl])
        d = _mm(gate * _sigmoid(gate) * up, wd_ref[sl, :])
        acc = d if acc is None else acc + d
    h = x + 0.5 * acc
    if final:
        h = h * lax.rsqrt(jnp.mean(h * h, -1, keepdims=True) + EPS) * gf_ref[...]
    o_ref[...] = h


def _ffn(x, g, wg, wu, wd, gfin=None, tm=512):
    r, d = x.shape
    f = wg.shape[1]
    tm = min(tm, r)
    row = pl.BlockSpec((tm, d), lambda i: (i, 0))
    in_specs = [row, _const_spec((1, d)), _const_spec((d, f)), _const_spec((d, f)), _const_spec((f, d))]
    args = [x, g.reshape(1, d), wg, wu, wd]
    if gfin is not None:
        in_specs.append(_const_spec((1, d)))
        args.append(gfin.reshape(1, d))
    return pl.pallas_call(
        functools.partial(_ffn_kernel, fc=_ff_chunk(f), final=gfin is not None),
        grid=(r // tm,), in_specs=in_specs, out_specs=row,
        out_shape=jax.ShapeDtypeStruct((r, d), F32),
        compiler_params=_params("parallel"), name="ffn")(*args)


def _rope_tables(pos):
    half = ROT_DIM // 2
    inv = ROPE_THETA ** (-jnp.arange(half, dtype=F32) / half)
    ang = pos.astype(F32)[:, None] * inv[None, :]
    cos, sin = jnp.cos(ang), jnp.sin(ang)
    t = pos.shape[0]
    rest = HEAD_DIM - ROT_DIM
    c = jnp.concatenate([cos, cos, jnp.ones((t, rest), F32)], 1)
    s1 = jnp.concatenate([jnp.zeros((t, half), F32), sin, jnp.zeros((t, rest), F32)], 1)
    s2 = jnp.concatenate([-sin, jnp.zeros((t, half + rest), F32)], 1)
    rep = LANES // HEAD_DIM
    return jnp.tile(c, (1, rep)), jnp.tile(s1, (1, rep)), jnp.tile(s2, (1, rep))


def _rope(z, c, s1, s2):
    out = []
    for j in range(z.shape[1] // LANES):
        x = z[:, j * LANES:(j + 1) * LANES]
        out.append(x * c + pltpu.roll(x, ROT_DIM // 2, 1) * s1 + pltpu.roll(x, LANES - ROT_DIM // 2, 1) * s2)
    return jnp.concatenate(out, axis=1)


def _head_tiles(z, extra=None):
    low = lax.broadcasted_iota(jnp.int32, (1, LANES), 1) < HEAD_DIM
    tiles = []
    for h in range(N_KV_HEADS):
        pair = z[:, (h // 2) * LANES:(h // 2 + 1) * LANES]
        if h % 2:
            pair = pltpu.roll(pair, HEAD_DIM, 1)
        tiles.append(jnp.where(low, pair, 0.0 if extra is None else extra))
    return tiles


def _inproj_kernel(h_ref, g_ref, w_ref, c_ref, s1_ref, s2_ref, *outs, offs, sample, qscale, nper):
    x = h_ref[...]
    tm = x.shape[0]
    xb = (x * lax.rsqrt(jnp.mean(x * x, -1, keepdims=True) + EPS) * g_ref[...]).astype(MXU_DTYPE)
    c, s1, s2 = c_ref[...], s1_ref[...], s2_ref[...]

    def seg(i):
        return _mm(xb, w_ref[:, offs[i]:offs[i + 1]])

    glu = seg(0)
    dc = glu.shape[1] // 2
    v = glu[:, :dc] * _sigmoid(glu[:, dc:])
    q = _rope(seg(1), c, s1, s2) * qscale
    kc, vc = seg(2), seg(3)
    ks, vs = _rope(seg(4), c, s1, s2), seg(5)
    kw, vw = _rope(seg(6), c, s1, s2), seg(7)
    ng = _sigmoid(seg(8))
    mg = _sigmoid(seg(9))
    if sample:
        (v_ref, q_ref, kct_ref, vct_ref, kst_ref, vst_ref, kwt_ref, vwt_ref,
         ks_ref, vs_ref, kw_ref, vw_ref, ng_ref, mg_ref) = outs
        q_ref[...] = q
        for ref, val in ((ks_ref, ks), (vs_ref, vs), (kw_ref, kw), (vw_ref, vw)):
            ref[...] = val
        ng_ref[...] = ng
    else:
        (v_ref, qt_ref, kct_ref, vct_ref, kst_ref, vst_ref, kwt_ref, vwt_ref,
         ksa_ref, kwa_ref, vstb_ref, vwtb_ref, ngt_ref, mg_ref) = outs
        qt_ref[...] = q.T.astype(MXU_DTYPE)
        pos = (pl.program_id(0) % nper) * tm + lax.broadcasted_iota(jnp.int32, (tm, 1), 0)
        blk_in_tile = lax.shift_right_logical(pos, SEL_BLOCK.bit_length() - 1) & (BLK_PER_TILE - 1)
        lane = lax.broadcasted_iota(jnp.int32, (1, LANES), 1)
        onehot = jnp.where(lane - HEAD_DIM == blk_in_tile, 1.0, 0.0)
        for h, tile in enumerate(_head_tiles(ks, onehot)):
            ksa_ref[h] = tile.astype(MXU_DTYPE)
        for h, tile in enumerate(_head_tiles(kw)):
            kwa_ref[h] = tile.astype(MXU_DTYPE)
        ngt_ref[...] = ng.T[0:3 * N_HEADS, :]
    v_ref[...] = v
    mg_ref[...] = mg
    kct_ref[...] = kc.T
    vct_ref[...] = vc.T
    kst_ref[...] = ks.T
    kwt_ref[...] = kw.T
    vst = vs.T
    vwt = vw.T
    vst_ref[...] = vst
    vwt_ref[...] = vwt
    if not sample:
        ones = jnp.where(lax.broadcasted_iota(jnp.int32, (V_ROWS - HEAD_DIM, tm), 0) == 0, 1.0, 0.0)
        for h in range(N_KV_HEADS):
            rs = slice(h * HEAD_DIM, (h + 1) * HEAD_DIM)
            vstb_ref[h] = jnp.concatenate([vst[rs], ones], axis=0).astype(MXU_DTYPE)
            vwtb_ref[h] = jnp.concatenate([vwt[rs], ones], axis=0).astype(MXU_DTYPE)


def _inproj(h, g, w_pad, offs, tables, nseq, rows_per_seq, sample, qscale, tm=512):
    r, d = h.shape
    tm = min(tm, rows_per_seq)
    nper = rows_per_seq // tm
    nat = lambda w, dt=F32: (pl.BlockSpec((tm, w), lambda i: (i, 0)), jax.ShapeDtypeStruct((r, w), dt))
    tr = lambda w, dt=F32: (pl.BlockSpec((None, w, tm), lambda i: (i // nper, 0, i % nper)),
                            jax.ShapeDtypeStruct((nseq, w, rows_per_seq), dt))
    aug = (pl.BlockSpec((None, N_KV_HEADS, tm, LANES), lambda i: (i // nper, 0, i % nper, 0)),
           jax.ShapeDtypeStruct((nseq, N_KV_HEADS, rows_per_seq, LANES), MXU_DTYPE))
    tab = pl.BlockSpec((tm, LANES), lambda i: (i % nper, 0))
    dc = (offs[1] - offs[0]) // 2
    dq = offs[2] - offs[1]
    planes = [tr(KV_W)] * 6
    if sample:
        outs = [nat(dc), nat(dq)] + planes + [nat(KV_W)] * 4 + [nat(LANES), nat(offs[10] - offs[9])]
    else:
        vplane = (pl.BlockSpec((None, N_KV_HEADS, V_ROWS, tm), lambda i: (i // nper, 0, 0, i % nper)),
                  jax.ShapeDtypeStruct((nseq, N_KV_HEADS, V_ROWS, rows_per_seq), MXU_DTYPE))
        outs = ([nat(dc), tr(dq, MXU_DTYPE)] + planes + [aug, aug, vplane, vplane,
                tr(3 * N_HEADS), nat(offs[10] - offs[9])])
    return pl.pallas_call(
        functools.partial(_inproj_kernel, offs=offs, sample=sample, qscale=qscale, nper=nper),
        grid=(r // tm,),
        in_specs=[pl.BlockSpec((tm, d), lambda i: (i, 0)), _const_spec((1, d)), _const_spec(w_pad.shape),
                  tab, tab, tab],
        out_specs=[o[0] for o in outs], out_shape=[o[1] for o in outs],
        compiler_params=_params("parallel"), name="inproj")(h, g.reshape(1, d), w_pad, *tables)


def _conv_tail(y, b_ref, lg_ref, lb_ref, wo_ref):
    y = y + b_ref[...]
    mu = jnp.mean(y, -1, keepdims=True)
    yc = y - mu
    var = jnp.mean(yc * yc, -1, keepdims=True)
    yn = yc * lax.rsqrt(var + EPS) * lg_ref[...] + lb_ref[...]
    return _mm(yn * _sigmoid(yn), wo_ref[...])


def _conv_kernel(main_ref, halo_ref, w_ref, b_ref, lg_ref, lb_ref, wo_ref, o_ref, ext_ref, sh_ref, y_ref):
    ts, c = main_ref.shape
    ext_ref[0:CONV_HALO, :] = jnp.where(pl.program_id(1) == 0, 0.0, halo_ref[...])
    ext_ref[CONV_HALO:CONV_HALO + ts, :] = main_ref[...]
    lead = CONV_HALO - (CONV_WIDTH - 1)
    n_sh = ts + CONV_HALO - SUBLANES
    for r in range(1, SUBLANES):
        sh_ref[r - 1] = ext_ref[r:r + n_sh, :]

    def tap_rows(k, r0, cs):
        off = lead + k
        r = off % SUBLANES
        base = r0 + off - r
        src = ext_ref if r == 0 else sh_ref.at[r - 1]
        return src[base:base + CONV_ROWS, cs]

    for cb in range(c // LANES):
        cs = slice(cb * LANES, (cb + 1) * LANES)
        taps = [w_ref[k:k + 1, cs] for k in range(CONV_WIDTH)]
        for r0 in range(0, ts, CONV_ROWS):
            acc = tap_rows(0, r0, cs) * taps[0]
            for k in range(1, CONV_WIDTH):
                acc = acc + tap_rows(k, r0, cs) * taps[k]
            y_ref[r0:r0 + CONV_ROWS, cs] = acc
    o_ref[...] = _conv_tail(y_ref[...], b_ref, lg_ref, lb_ref, wo_ref)


def _conv(v, w, b, lg, lb, wo, ts=512):
    bsz, l, c = v.shape
    d = wo.shape[1]
    step = ts // CONV_HALO
    vec = lambda a: a.reshape(1, c)
    return pl.pallas_call(
        _conv_kernel,
        grid=(bsz, l // ts),
        in_specs=[pl.BlockSpec((None, ts, c), lambda bi, i: (bi, i, 0)),
                  pl.BlockSpec((None, CONV_HALO, c), lambda bi, i: (bi, jnp.maximum(i * step - 1, 0), 0)),
                  _const_spec((CONV_WIDTH, c)), _const_spec((1, c)), _const_spec((1, c)), _const_spec((1, c)),
                  _const_spec((c, d))],
        out_specs=pl.BlockSpec((None, ts, d), lambda bi, i: (bi, i, 0)),
        out_shape=jax.ShapeDtypeStruct((bsz, l, d), F32),
        scratch_shapes=[pltpu.VMEM((ts + CONV_HALO, c), F32),
                        pltpu.VMEM((SUBLANES - 1, ts + CONV_HALO - SUBLANES, c), F32), pltpu.VMEM((ts, c), F32)],
        compiler_params=_params("parallel", "parallel"), name="conv")(v, v, w, vec(b), vec(lg), vec(lb), wo)


def _conv_sample_kernel(st_ref, v_ref, w_ref, b_ref, lg_ref, lb_ref, wo_ref, o_ref):
    acc = v_ref[...] * w_ref[CONV_WIDTH - 1:CONV_WIDTH, :]
    for k in range(CONV_WIDTH - 1):
        acc = acc + st_ref[k] * w_ref[k:k + 1, :]
    o_ref[...] = _conv_tail(acc, b_ref, lg_ref, lb_ref, wo_ref)


def _conv_sample(st, v, w, b, lg, lb, wo):
    rows, c = v.shape
    d = wo.shape[1]
    vec = lambda a: a.reshape(1, c)
    args = (st, v, w, vec(b), vec(lg), vec(lb), wo)
    return pl.pallas_call(
        _conv_sample_kernel, grid=(1,),
        in_specs=[_const_spec(a.shape) for a in args],
        out_specs=pl.BlockSpec((rows, d), lambda i: (0, 0)),
        out_shape=jax.ShapeDtypeStruct((rows, d), F32),
        compiler_params=_params("arbitrary"), name="conv_sample")(*args)


def _compress_kernel(pt_ref, x_hbm, w1_ref, pe_ref, w1f_ref, b1_ref, w2_ref, o_ref,
                     buf, tbuf, fsbuf, sem, *, pc, n_pages, paged, planes):
    s = pl.program_id(0)
    ci = pl.program_id(1)
    n_chunks = pl.num_programs(1)
    step = s * n_chunks + ci
    slot = step % 2
    m = pc * SUB_PER_PAGE
    nrow = m + SUB_PER_PAGE
    pairs = N_KV_HEADS // 2

    def page_copy(s_, ci_, slot_, p):
        pg = pt_ref[s_ * n_pages + jnp.minimum(ci_ * pc + p, n_pages - 1)]
        if paged:
            src = x_hbm.at[pg]
        else:
            src = x_hbm.at[s_, :, :, pl.ds(pl.multiple_of(pg * PAGE_SIZE, PAGE_SIZE), PAGE_SIZE)]
        return pltpu.make_async_copy(src, buf.at[slot_, p], sem.at[slot_])

    @pl.when(step == 0)
    def _():
        for p in range(pc + 1):
            page_copy(s, ci, slot, p).start()

    @pl.when(step + 1 < pl.num_programs(0) * n_chunks)
    def _():
        wrap = ci + 1 == n_chunks
        for p in range(pc + 1):
            page_copy(jnp.where(wrap, s + 1, s), jnp.where(wrap, 0, ci + 1), 1 - slot, p).start()

    for p in range(pc + 1):
        page_copy(s, ci, slot, p).wait()

    for p in range(pc + 1):
        for j in range(pairs):
            plane = buf[slot, p, 2 * j:2 * j + 2].reshape(2 * HEAD_DIM, PAGE_SIZE)
            tbuf[j, p * PAGE_SIZE:(p + 1) * PAGE_SIZE, :] = plane.T
    for j in range(pairs):
        acc = None
        for lp in range(CMP_STRIDE // 2):
            lhs = jnp.concatenate([tbuf[j, pl.ds(2 * lp + i, nrow, stride=CMP_STRIDE), :] for i in range(2)], axis=1)
            d = _mm(lhs, w1_ref[lp])
            acc = d if acc is None else acc + d
        fsbuf[j] = acc
    c = _mm(pe_ref[...], w1f_ref[...])[0:1, :] + b1_ref[...]
    nh = c.shape[1]
    parts = []
    for h in range(N_KV_HEADS):
        col = (h % 2) * 2 * nh
        parts.append(fsbuf[h // 2, 0:m, col:col + nh] + fsbuf[h // 2, pl.ds(1, m), col + nh:col + 2 * nh] + c)
    hid = jnp.concatenate(parts, axis=1)
    out = _mm((hid * _sigmoid(hid)).astype(MXU_DTYPE), w2_ref[...])
    if planes:
        o_ref[...] = out.T.astype(MXU_DTYPE)
    else:
        for h, tile in enumerate(_head_tiles(out)):
            o_ref[h] = tile.astype(MXU_DTYPE)


def _compress(x, page_table, weights, pc, paged, planes):
    w1_big, pe8, w1_flat, b1, w2_big = weights
    n_seq, n_pages = page_table.shape
    m = pc * SUB_PER_PAGE
    n_sub = n_pages * SUB_PER_PAGE
    hidden = b1.shape[0]
    if planes:
        out_spec = pl.BlockSpec((None, KV_W, m), lambda s, ci, pt: (s, 0, ci))
        out_shape = jax.ShapeDtypeStruct((n_seq, KV_W, n_sub), MXU_DTYPE)
    else:
        out_spec = pl.BlockSpec((None, N_KV_HEADS, m, LANES), lambda s, ci, pt: (s, 0, ci, 0))
        out_shape = jax.ShapeDtypeStruct((n_seq, N_KV_HEADS, n_sub, LANES), MXU_DTYPE)
    grid_spec = pltpu.PrefetchScalarGridSpec(
        num_scalar_prefetch=1,
        grid=(n_seq, n_pages // pc),
        in_specs=[pl.BlockSpec(memory_space=pl.ANY),
                  _const_spec(w1_big.shape), _const_spec(pe8.shape), _const_spec(w1_flat.shape),
                  _const_spec((1, hidden)), _const_spec(w2_big.shape)],
        out_specs=out_spec,
        scratch_shapes=[pltpu.VMEM((2, pc + 1, N_KV_HEADS, HEAD_DIM, PAGE_SIZE), F32),
                        pltpu.VMEM((N_KV_HEADS // 2, (pc + 1) * PAGE_SIZE, LANES), F32),
                        pltpu.VMEM((N_KV_HEADS // 2, m + SUB_PER_PAGE, w1_big.shape[2]), F32),
                        pltpu.SemaphoreType.DMA((2,))])
    return pl.pallas_call(
        functools.partial(_compress_kernel, pc=pc, n_pages=n_pages, paged=paged, planes=planes),
        grid_spec=grid_spec, out_shape=out_shape,
        compiler_params=_params("arbitrary", "arbitrary"), name="compress")(
            page_table.reshape(-1), x, w1_big, pe8, w1_flat, b1.reshape(1, hidden), w2_big)


def _compress_weights(pe, w1, b1, w2):
    hidden = w1.shape[-1]
    eye = jnp.eye(N_KV_HEADS, dtype=F32)
    both = jnp.concatenate([w1[:CMP_STRIDE], w1[CMP_STRIDE:]], axis=-1)
    both = both.reshape(CMP_STRIDE // 2, 2, HEAD_DIM, 2 * hidden)
    w1_big = jnp.einsum('pidc,hg->pihdgc', both, jnp.eye(2, dtype=F32)).reshape(
        CMP_STRIDE // 2, 4 * HEAD_DIM, 4 * hidden)
    w2_big = jnp.einsum('kd,hg->hkgd', w2, eye).reshape(N_KV_HEADS * hidden, KV_W)
    pe8 = jnp.zeros((SUBLANES, CMP_LEN * HEAD_DIM), F32).at[0].set(pe.reshape(-1))
    return (w1_big.astype(MXU_DTYPE), pe8, w1.reshape(CMP_LEN * HEAD_DIM, hidden).astype(MXU_DTYPE), b1,
            w2_big.astype(MXU_DTYPE))


def _overlap(n_cmp_pad, n_blk_pad):
    ci = np.arange(n_cmp_pad)[:, None] * CMP_STRIDE
    bs = np.arange(n_blk_pad)[None, :] * SEL_BLOCK
    return ((ci < bs + SEL_BLOCK) & (ci + CMP_LEN > bs)).astype(np.float32)


def _attn_kernel(q_ref, g_ref, kc_ref, vct_ref, ov_ref, ks_ref, vst_ref, kw_ref, vwt_ref, o_ref,
                 bias_ref, sc_ref, imp_ref, sa_ref, sb_ref, *, n_cmp, n_blk):
    s0 = pl.program_id(2) * Q_TILE
    cols = GROUP * Q_TILE
    qt = jnp.concatenate([q_ref[g] for g in range(GROUP)], axis=1)
    q0 = jnp.concatenate([qt, jnp.zeros((LANES - HEAD_DIM, cols), qt.dtype)], axis=0)
    tq = s0 + (lax.broadcasted_iota(jnp.int32, (1, cols), 1) & (Q_TILE - 1))
    tq1 = s0 + lax.broadcasted_iota(jnp.int32, (1, Q_TILE), 1)

    ncp = kc_ref.shape[0]
    ch = min(CMP_CHUNK, ncp)
    last_c = lax.shift_right_logical(s0 + (Q_TILE - CMP_LEN), CMP_STRIDE.bit_length() - 1)
    n_cc = jnp.minimum(last_c // ch + 1, ncp // ch)
    ci0 = lax.broadcasted_iota(jnp.int32, (ch, 1), 0)

    def cmp_scores(cc, m):
        r0 = pl.multiple_of(cc * ch, ch)
        c = ci0 + r0
        s = jnp.where((c * CMP_STRIDE + (CMP_LEN - 1) <= tq) & (c < n_cmp), _mm(kc_ref[pl.ds(r0, ch), :], q0), NEG)
        sc_ref[pl.ds(r0, ch), :] = s
        return jnp.maximum(m, jnp.max(s, axis=0, keepdims=True))

    m_c = lax.fori_loop(0, n_cc, cmp_scores, jnp.full((1, cols), NEG, F32))
    imp_ref[...] = jnp.zeros(imp_ref.shape, F32)

    def cmp_weights(cc, carry):
        l, acc = carry
        r0 = pl.multiple_of(cc * ch, ch)
        p = jnp.exp2(sc_ref[pl.ds(r0, ch), :] - m_c)
        pb = p.astype(MXU_DTYPE)
        imp_ref[...] += _mm(ov_ref[:, pl.ds(r0, ch)], pb)
        return l + jnp.sum(p, axis=0, keepdims=True), acc + _mm(vct_ref[:, pl.ds(r0, ch)], pb)

    l_c, acc_c = lax.fori_loop(0, n_cc, cmp_weights,
                               (jnp.zeros((1, cols), F32), jnp.zeros((HEAD_DIM, cols), F32)))
    inv_c = jnp.where(m_c > 0.5 * NEG, 1.0 / jnp.maximum(l_c, 1e-30), 0.0)
    o_c = acc_c * inv_c
    impn = imp_ref[...] * inv_c
    imp = impn[:, 0:Q_TILE]
    for g in range(1, GROUP):
        imp = imp + impn[:, g * Q_TILE:(g + 1) * Q_TILE]

    wk = WINDOW + Q_TILE
    w0 = pl.multiple_of(jnp.maximum(s0 - WINDOW, 0), Q_TILE)
    wpos = w0 + lax.broadcasted_iota(jnp.int32, (wk, 1), 0)
    sw = jnp.where((wpos <= tq) & (wpos > tq - WINDOW), _mm(kw_ref[pl.ds(w0, wk), :], q0), NEG)
    acc_w = _mm(vwt_ref[:, pl.ds(w0, wk)], jnp.exp2(sw - jnp.max(sw, axis=0, keepdims=True)))
    o_w = acc_w[0:HEAD_DIM] / acc_w[HEAD_DIM:HEAD_DIM + 1]

    blk = lax.broadcasted_iota(jnp.int32, (n_blk, 1), 0)
    blk_f = blk.astype(F32)
    cur = lax.shift_right_logical(tq1, SEL_BLOCK.bit_length() - 1)
    forced = (blk == 0) | (blk == cur) | (blk == cur - 1)
    valid = blk * SEL_BLOCK <= tq1
    n_top = min(N_SELECT, n_blk)

    def ranked_selection():
        score = jnp.where(valid, imp + jnp.where(forced, BIG, 0.0), -BIG)
        for _ in range(n_top):
            top = jnp.max(score, axis=0, keepdims=True)
            first = jnp.min(jnp.where(score == top, blk_f, float(n_blk)), axis=0, keepdims=True)
            score = jnp.where(blk_f == first, REMOVED, score)
        return score == REMOVED

    free = valid & jnp.logical_not(forced)
    cand = jnp.where(free, imp, REMOVED)
    n_forced = 1 + (cur >= 1).astype(jnp.int32) + (cur >= 2).astype(jnp.int32)
    n_free = jnp.sum(jnp.where(free, 1.0, 0.0), axis=0, keepdims=True)
    want = jnp.minimum((n_top - n_forced).astype(F32), n_free)
    thr = jnp.full((1, Q_TILE), jnp.inf, F32)
    kth = {}
    for r in range(1, n_top):
        thr = jnp.max(jnp.where(cand < thr, cand, REMOVED), axis=0, keepdims=True)
        kth[r] = thr
    thr = jnp.where(n_forced == 3, kth[max(n_top - 3, 1)], jnp.where(n_forced == 2, kth[max(n_top - 2, 1)],
                                                                      kth[n_top - 1]))
    picked = free & (cand >= thr)
    got = jnp.sum(jnp.where(picked, 1.0, 0.0), axis=0, keepdims=True)
    exact = jnp.max(jnp.abs(got - want)) == 0.0
    as_bias = lambda selected: jnp.where(selected & valid, 0.0, NEG)
    bias_ref[...] = lax.cond(exact, lambda: as_bias(picked | forced), lambda: as_bias(ranked_selection()))

    kpos0 = lax.broadcasted_iota(jnp.int32, (K_TILE, 1), 0)
    q_pad = jnp.zeros((LANES - HEAD_DIM - BLK_ROWS, cols), qt.dtype)
    b_pad = jnp.zeros((BLK_ROWS - BLK_PER_TILE, cols), F32)

    def sel_scores(j, s_ref):
        k0 = pl.multiple_of(j * K_TILE, K_TILE)
        b8 = bias_ref[pl.ds(pl.multiple_of(j * BLK_PER_TILE, BLK_PER_TILE), BLK_PER_TILE), :]
        rows = jnp.concatenate([b8] * GROUP, axis=1)
        if BLK_ROWS > BLK_PER_TILE:
            rows = jnp.concatenate([rows, b_pad], axis=0)
        rows = rows.astype(qt.dtype)
        s_ref[...] = _mm(ks_ref[pl.ds(k0, K_TILE), :], jnp.concatenate([qt, rows, q_pad], axis=0))

    def sel_update(j, s_ref, carry, causal=False):
        v = vst_ref[:, pl.ds(pl.multiple_of(j * K_TILE, K_TILE), K_TILE)]
        out = []
        for g in range(GROUP):
            m, acc = carry[g]
            s = s_ref[:, g * Q_TILE:(g + 1) * Q_TILE]
            if causal:
                s = jnp.where(kpos0 + j * K_TILE <= tq1, s, NEG)
            m_new = jnp.maximum(m, jnp.max(s, axis=0, keepdims=True))
            out.append((m_new, jnp.exp2(m - m_new) * acc + _mm(v, jnp.exp2(s - m_new))))
        return tuple(out)

    def sel_pair(i, carry):
        sel_scores(2 * i + 1, sb_ref)
        carry = sel_update(2 * i, sa_ref, carry)
        sel_scores(2 * i + 2, sa_ref)
        return sel_update(2 * i + 1, sb_ref, carry)

    last = (s0 + Q_TILE - 1) // K_TILE
    n_pairs = last // 2
    sel_scores(0, sa_ref)
    init = tuple((jnp.full((1, Q_TILE), NEG, F32), jnp.zeros((V_ROWS, Q_TILE), F32)) for _ in range(GROUP))
    carry = lax.fori_loop(0, n_pairs, sel_pair, init)
    ta = 2 * n_pairs

    def two_left(c):
        sel_scores(last, sb_ref)
        return sel_update(last, sb_ref, sel_update(ta, sa_ref, c), causal=True)

    carry = lax.cond(ta < last, two_left, lambda c: sel_update(ta, sa_ref, c, causal=True), carry)
    acc_s = jnp.concatenate([acc for _, acc in carry], axis=1)
    o_s = acc_s[0:HEAD_DIM] / acc_s[HEAD_DIM:HEAD_DIM + 1]

    gate = lambda br: jnp.concatenate([g_ref[g * 3 + br:g * 3 + br + 1, :] for g in range(GROUP)], axis=1)
    o = gate(0) * o_c + gate(1) * o_s + gate(2) * o_w
    for g in range(GROUP):
        o_ref[g] = o[:, g * Q_TILE:(g + 1) * Q_TILE]


def _attn_prompt(qt, gt, kc, vct, ks, vst, kw, vwt, n_cmp):
    b, h, grp, hd, s = qt.shape
    ncp = kc.shape[2]
    assert K_TILE % Q_TILE == 0 and BLK_PER_TILE <= BLK_ROWS
    n_blk = s // SEL_BLOCK
    nt = s // Q_TILE
    cols = grp * Q_TILE
    ov = jnp.asarray(_overlap(ncp, n_blk).T, MXU_DTYPE)
    one = pl.Buffered(1)
    rows = lambda n: pl.BlockSpec((None, None, n, LANES), lambda bi, hi, t: (bi, hi, 0, 0), pipeline_mode=one)
    plane = lambda n: pl.BlockSpec((None, hd, n), lambda bi, hi, t: (bi, hi, 0), pipeline_mode=one)
    vplane = pl.BlockSpec((None, None, V_ROWS, s), lambda bi, hi, t: (bi, hi, 0, 0), pipeline_mode=one)
    qspec = pl.BlockSpec((None, None, grp, hd, Q_TILE), lambda bi, hi, t: (bi, hi, 0, 0, t))
    return pl.pallas_call(
        functools.partial(_attn_kernel, n_cmp=n_cmp, n_blk=n_blk),
        grid=(b, h, nt),
        in_specs=[qspec, pl.BlockSpec((None, None, 3 * grp, Q_TILE), lambda bi, hi, t: (bi, hi, 0, t)),
                  rows(ncp), plane(ncp), _const_spec(ov.shape), rows(s), vplane, rows(s), vplane],
        out_specs=qspec,
        out_shape=jax.ShapeDtypeStruct((b, h, grp, hd, s), F32),
        scratch_shapes=[pltpu.VMEM((n_blk, Q_TILE), F32), pltpu.VMEM((ncp, cols), F32),
                        pltpu.VMEM((n_blk, cols), F32), pltpu.VMEM((K_TILE, cols), F32),
                        pltpu.VMEM((K_TILE, cols), F32)],
        compiler_params=_params("parallel", "parallel", "arbitrary"), name="attn_prompt")(
            qt, gt, kc, vct, ov, ks, vst, kw, vwt)


def _sample_cmp_kernel(q_ref, kct_ref, vc_ref, ov_ref, oc_ref, idx_ref, *, n_cmp, n_blk, t):
    nb = q_ref.shape[0]
    ncp = kct_ref.shape[2]
    nbp = ov_ref.shape[1]
    ci = lax.broadcasted_iota(jnp.int32, (1, ncp), 1)
    mask = (ci * CMP_STRIDE + (CMP_LEN - 1) <= t) & (ci < n_cmp)
    sums = []
    for b in range(nb):
        for h in range(N_KV_HEADS):
            s = jnp.where(mask, _mm(q_ref[b, h], kct_ref[b, h * HEAD_DIM:(h + 1) * HEAD_DIM, :]), NEG)
            m = jnp.max(s, axis=1, keepdims=True)
            p = jnp.where(mask, jnp.exp(s - m), 0.0)
            p = p / jnp.maximum(jnp.sum(p, axis=1, keepdims=True), 1e-30)
            oc_ref[b, h] = _mm(p, vc_ref[b, h])
            sums.append(jnp.sum(p, axis=0, keepdims=True))
    imp = _mm(jnp.concatenate(sums, axis=0), ov_ref[...])
    blk = lax.broadcasted_iota(jnp.int32, (1, nbp), 1)
    cur = t // SEL_BLOCK
    forced = (blk == 0) | (blk == cur) | (blk == cur - 1)
    score = jnp.where(blk * SEL_BLOCK <= t, imp + jnp.where(forced, BIG, 0.0), -BIG)
    score = jnp.where(blk < n_blk, score, REMOVED)
    blk_f = blk.astype(F32)
    lane = lax.broadcasted_iota(jnp.int32, (1, LANES), 1)
    out = jnp.zeros((nb * N_KV_HEADS, LANES), F32)
    for r in range(min(N_SELECT, n_blk)):
        top = jnp.max(score, axis=1, keepdims=True)
        first = jnp.min(jnp.where(score == top, blk_f, float(nbp)), axis=1, keepdims=True)
        out = jnp.where(lane == r, first, out)
        score = jnp.where(blk_f == first, REMOVED, score)
    idx_ref[...] = out.astype(jnp.int32)


def _sample_cmp(q4, kct, vc, n_cmp, n_blk, t):
    db, _, _, hd = q4.shape
    ncp = kct.shape[2]
    nbp = -(-n_blk // LANES) * LANES
    nb = SAMPLE_BATCH if db % SAMPLE_BATCH == 0 else 1
    ov = jnp.asarray(_overlap(ncp, nbp), MXU_DTYPE)
    per = lambda r, w: pl.BlockSpec((nb, N_KV_HEADS, r, w), lambda b: (b, 0, 0, 0))
    oc, idx = pl.pallas_call(
        functools.partial(_sample_cmp_kernel, n_cmp=n_cmp, n_blk=n_blk, t=t),
        grid=(db // nb,),
        in_specs=[per(GROUP, hd), pl.BlockSpec((nb, KV_W, ncp), lambda b: (b, 0, 0)), per(ncp, LANES),
                  _const_spec(ov.shape)],
        out_specs=[per(GROUP, LANES), pl.BlockSpec((None, nb * N_KV_HEADS, LANES), lambda b: (b, 0, 0))],
        out_shape=[jax.ShapeDtypeStruct((db, N_KV_HEADS, GROUP, LANES), F32),
                   jax.ShapeDtypeStruct((db // nb, nb * N_KV_HEADS, LANES), jnp.int32)],
        compiler_params=_params("parallel"), name="sample_cmp")(q4, kct, vc, ov)
    return oc, idx.reshape(db, N_KV_HEADS, LANES)


def _sample_sel_kernel(idx_ref, pt_ref, q_ref, oc_ref, g_ref, ksn_ref, vsn_ref, kwn_ref, vwn_ref, wk_ref, wv_ref,
                       pk_hbm, pv_hbm, o_ref, kbuf, vbuf, sem, *, n_past_blk, n_pages, nsel, t):
    b = pl.program_id(0)
    sub = PAGE_SIZE // SEL_BLOCK

    slot = b % 2

    def sel_block(b_, h, n):
        return idx_ref[(b_ * N_KV_HEADS + h) * nsel + n]

    def page_copies(b_, slot_, h, n):
        pg = pt_ref[b_ * n_pages + jnp.minimum(sel_block(b_, h, n), n_past_blk - 1) // sub]
        dst = pl.ds(n * PAGE_SIZE, PAGE_SIZE)
        return (pltpu.make_async_copy(pk_hbm.at[pg, h], kbuf.at[slot_, h, :, dst], sem.at[0, slot_]),
                pltpu.make_async_copy(pv_hbm.at[pg, h], vbuf.at[slot_, h, :, dst], sem.at[1, slot_]))

    def all_copies(b_, slot_, go):
        for h in range(N_KV_HEADS):
            for n in range(nsel):
                for cp in page_copies(b_, slot_, h, n):
                    go(cp)

    @pl.when(b == 0)
    def _():
        all_copies(b, slot, lambda cp: cp.start())

    @pl.when(b + 1 < pl.num_programs(0))
    def _():
        all_copies(b + 1, 1 - slot, lambda cp: cp.start())

    all_copies(b, slot, lambda cp: cp.wait())

    lane = lax.broadcasted_iota(jnp.int32, (1, nsel * PAGE_SIZE), 1)
    seg = lax.shift_right_logical(lane, PAGE_SIZE.bit_length() - 1)
    half = lax.shift_right_logical(lane, SEL_BLOCK.bit_length() - 1) & (sub - 1)
    wl = wk_ref.shape[2]
    wpos = t - wl + lax.broadcasted_iota(jnp.int32, (1, wl), 1)
    okw = (wpos > t - WINDOW) & (wpos >= 0)
    for h in range(N_KV_HEADS):
        q = q_ref[h]
        want = jnp.full(lane.shape, -1, jnp.int32)
        for n in range(nsel):
            blk = sel_block(b, h, n)
            want = jnp.where(seg == n, jnp.where(blk < n_past_blk, blk % sub, -1), want)
        ok = half == want
        s = jnp.where(ok, _mm(q, kbuf[slot, h]), NEG)
        s_new = jnp.sum(q * ksn_ref[h:h + 1, :], axis=1, keepdims=True)
        m = jnp.maximum(jnp.max(s, axis=1, keepdims=True), s_new)
        p = jnp.where(ok, jnp.exp(s - m), 0.0)
        p_new = jnp.exp(s_new - m)
        o_s = ((_mm_nt(p, vbuf[slot, h]) + p_new * vsn_ref[h:h + 1, :])
               / (jnp.sum(p, axis=1, keepdims=True) + p_new))
        sw = jnp.where(okw, _mm(q, wk_ref[h]), NEG)
        sw_new = jnp.sum(q * kwn_ref[h:h + 1, :], axis=1, keepdims=True)
        mw = jnp.maximum(jnp.max(sw, axis=1, keepdims=True), sw_new)
        pw = jnp.where(okw, jnp.exp(sw - mw), 0.0)
        pw_new = jnp.exp(sw_new - mw)
        o_w = (_mm_nt(pw, wv_ref[h]) + pw_new * vwn_ref[h:h + 1, :]) / (jnp.sum(pw, axis=1, keepdims=True) + pw_new)
        g = g_ref[h]
        o_ref[h] = g[:, 0:1] * oc_ref[h][:, 0:HEAD_DIM] + g[:, 1:2] * o_s + g[:, 2:3] * o_w


def _sample_sel(idx, page_table, q4, oc, gates, ks_new, vs_new, kw_new, vw_new, win_k, win_v, pool_k, pool_v, t):
    db, _, grp, hd = q4.shape
    nsel = idx.shape[2]
    n_pages = page_table.shape[1]
    n_past_blk = n_pages * (PAGE_SIZE // SEL_BLOCK)
    wl = win_k.shape[3]
    per = lambda *shape: pl.BlockSpec((None,) + shape, lambda b, *_: (b,) + (0,) * len(shape))
    hbm = pl.BlockSpec(memory_space=pl.ANY)
    grid_spec = pltpu.PrefetchScalarGridSpec(
        num_scalar_prefetch=2,
        grid=(db,),
        in_specs=[per(N_KV_HEADS, grp, hd), per(N_KV_HEADS, grp, LANES), per(N_KV_HEADS, grp, LANES)]
                 + [per(N_KV_HEADS, hd)] * 4 + [per(N_KV_HEADS, hd, wl)] * 2 + [hbm, hbm],
        out_specs=per(N_KV_HEADS, grp, hd),
        scratch_shapes=[pltpu.VMEM((2, N_KV_HEADS, hd, nsel * PAGE_SIZE), F32),
                        pltpu.VMEM((2, N_KV_HEADS, hd, nsel * PAGE_SIZE), F32),
                        pltpu.SemaphoreType.DMA((2, 2))])
    return pl.pallas_call(
        functools.partial(_sample_sel_kernel, n_past_blk=n_past_blk, n_pages=n_pages, nsel=nsel, t=t),
        grid_spec=grid_spec,
        out_shape=jax.ShapeDtypeStruct((db, N_KV_HEADS, grp, hd), F32),
        compiler_params=_params("arbitrary"), name="sample_sel")(
            idx.reshape(-1), page_table.reshape(-1), q4, oc, gates, ks_new, vs_new, kw_new, vw_new,
            win_k, win_v, pool_k, pool_v)


def _merge_kernel(h_ref, yc_ref, ya_ref, mg_ref, wa_ref, wo_ref, o_ref, *, planes):
    d = h_ref.shape[1]
    ya = ya_ref[...].T if planes else ya_ref[...]
    m = mg_ref[:, :d] * yc_ref[...] + mg_ref[:, d:] * _mm(ya, wa_ref[...])
    o_ref[...] = h_ref[...] + _mm(m, wo_ref[...])


def _merge(h, y_conv, y_attn, mg, wa, wo, rows_per_seq, planes, tm=512):
    r, d = h.shape
    tm = min(tm, rows_per_seq)
    nper = rows_per_seq // tm
    row = lambda w: pl.BlockSpec((tm, w), lambda i: (i, 0))
    da = wa.shape[0]
    ya_spec = pl.BlockSpec((None, da, tm), lambda i: (i // nper, 0, i % nper)) if planes else row(da)
    return pl.pallas_call(
        functools.partial(_merge_kernel, planes=planes), grid=(r // tm,),
        in_specs=[row(d), row(d), ya_spec, row(2 * d), _const_spec(wa.shape), _const_spec(wo.shape)],
        out_specs=row(d), out_shape=jax.ShapeDtypeStruct((r, d), F32),
        compiler_params=_params("parallel"), name="merge")(h, y_conv, y_attn, mg, wa, wo)


def kernel(x_prompt, x_sample, cache_cmp_k, cache_cmp_v, cache_sel_k, cache_sel_v, state_win_k, state_win_v, state_conv, page_table, norm_ffn1, ffn1_gate, ffn1_up, ffn1_down, norm_mix, w_in, conv_w, conv_b, conv_ln_g, conv_ln_b, w_conv_out, cmp_pe_k, cmp_w1_k, cmp_b1_k, cmp_w2_k, cmp_pe_v, cmp_w1_v, cmp_b1_v, cmp_w2_v, w_attn_out, w_out, norm_ffn2, ffn2_gate, ffn2_up, ffn2_down, norm_final):
    bsz, seq, d = x_prompt.shape
    db, dseq, _ = x_sample.shape
    depth = norm_ffn1.shape[0]
    assert depth == 1 and dseq == 1 and d == N_HEADS * HEAD_DIM
    assert seq % K_TILE == 0 and seq >= WINDOW + Q_TILE
    assert state_conv.shape[2] == CONV_WIDTH - 1
    n_pages = page_table.shape[1]
    past = n_pages * PAGE_SIZE
    d_conv = d // 2
    scale = HEAD_DIM ** -0.5
    cast = lambda w: w.astype(MXU_DTYPE)

    col_sizes = (2 * d_conv, N_HEADS * HEAD_DIM) + (KV_W,) * 6 + (3 * N_HEADS, 2 * d)
    starts = np.concatenate([[0], np.cumsum(col_sizes)])
    offs, pieces, pos = [0], [], 0
    for i, n in enumerate(col_sizes):
        pad = -n % LANES
        pieces.append(w_in[0][:, starts[i]:starts[i + 1]])
        if pad:
            pieces.append(jnp.zeros((d, pad), F32))
        pos += n + pad
        offs.append(pos)
    w_in_pad = cast(jnp.concatenate(pieces, axis=1))
    offs = tuple(offs)
    ffn1 = (norm_ffn1[0], cast(ffn1_gate[0]), cast(ffn1_up[0]), cast(ffn1_down[0]))
    ffn2 = (norm_ffn2[0], cast(ffn2_gate[0]), cast(ffn2_up[0]), cast(ffn2_down[0]))
    wa, wo, wco = cast(w_attn_out[0]), cast(w_out[0]), cast(w_conv_out[0])
    convp = (conv_w[0], conv_b[0], conv_ln_g[0], conv_ln_b[0], wco)
    cmpk = _compress_weights(cmp_pe_k[0], cmp_w1_k[0], cmp_b1_k[0], cmp_w2_k[0])
    cmpv = _compress_weights(cmp_pe_v[0], cmp_w1_v[0], cmp_b1_v[0], cmp_w2_v[0])
    planes4 = lambda x: x.reshape(x.shape[0], N_KV_HEADS, HEAD_DIM, x.shape[2])
    state5 = lambda x: jnp.transpose(planes4(x), (0, 3, 1, 2))[None]
    to_planes = lambda x: jnp.transpose(x, (0, 2, 3, 1))

    rp = bsz * seq
    h_p = _ffn(x_prompt.reshape(rp, d), *ffn1)
    (v_p, qt_p, kct_p, vct_p, kst_p, vst_p, kwt_p, vwt_p, ksa_p, kwa_p, vstb_p, vwtb_p, ngt_p, mg_p) = _inproj(
        h_p, norm_mix[0], w_in_pad, offs, _rope_tables(jnp.arange(seq, dtype=jnp.int32)), bsz, seq,
        sample=False, qscale=scale * LOG2E)
    y_conv_p = _conv(v_p.reshape(bsz, seq, d_conv), *convp).reshape(rp, d)
    pages_p = seq // PAGE_SIZE
    pc_p = min(32, pages_p)
    ident = jnp.tile(jnp.arange(pages_p, dtype=jnp.int32)[None], (bsz, 1))
    kc_rows = _compress(planes4(kct_p), ident, cmpk, pc_p, paged=False, planes=False)
    vc_planes = _compress(planes4(vct_p), ident, cmpv, pc_p, paged=False, planes=True)
    ot = _attn_prompt(qt_p.reshape(bsz, N_KV_HEADS, GROUP, HEAD_DIM, seq),
                      ngt_p.reshape(bsz, N_KV_HEADS, 3 * GROUP, seq),
                      kc_rows, vc_planes, ksa_p, vstb_p, kwa_p, vwtb_p, seq // CMP_STRIDE - 1)
    h2_p = _merge(h_p, y_conv_p, ot.reshape(bsz, d, seq), mg_p, wa, wo, seq, planes=True)
    y_prompt = _ffn(h2_p, *ffn2, gfin=norm_final).reshape(bsz, seq, d)

    t = past
    h_s = _ffn(x_sample.reshape(db, d), *ffn1)
    (v_s, q_s, kct_s, vct_s, kst_s, vst_s, kwt_s, vwt_s, ks_s, vs_s, kw_s, vw_s, ng_s, mg_s) = _inproj(
        h_s, norm_mix[0], w_in_pad, offs, _rope_tables(jnp.full((db,), t, jnp.int32)), 1, db,
        sample=True, qscale=scale)
    y_conv_s = _conv_sample(jnp.transpose(state_conv[0], (1, 0, 2)), v_s, *convp)
    pc_s = min(32, n_pages)
    n_sub_s = (past + dseq) // CMP_STRIDE
    assert n_sub_s == n_pages * SUB_PER_PAGE
    kc_planes_s = _compress(to_planes(cache_cmp_k[0]), page_table, cmpk, pc_s, paged=True, planes=True)
    vc_rows_s = _compress(to_planes(cache_cmp_v[0]), page_table, cmpv, pc_s, paged=True, planes=False)
    n_blk_s = -(-(past + dseq) // SEL_BLOCK)
    q4 = q_s.reshape(db, N_KV_HEADS, GROUP, HEAD_DIM)
    oc_s, idx_s = _sample_cmp(q4, kc_planes_s, vc_rows_s, n_sub_s - 1, n_blk_s, t)
    nsel = min(N_SELECT, n_blk_s)
    gates_s = jnp.pad(ng_s[:, :3 * N_HEADS].reshape(db, N_KV_HEADS, GROUP, 3),
                      ((0, 0), (0, 0), (0, 0), (0, LANES - 3)))
    new3 = lambda x: x.reshape(db, N_KV_HEADS, HEAD_DIM)
    y_attn_s = _sample_sel(idx_s[:, :, :nsel], page_table, q4, oc_s, gates_s,
                           new3(ks_s), new3(vs_s), new3(kw_s), new3(vw_s),
                           to_planes(state_win_k[0]), to_planes(state_win_v[0]),
                           to_planes(cache_sel_k[0]), to_planes(cache_sel_v[0]), t).reshape(db, d)
    h2_s = _merge(h_s, y_conv_s, y_attn_s, mg_s, wa, wo, db, planes=False)
    y_sample = _ffn(h2_s, *ffn2, gfin=norm_final).reshape(db, dseq, d)

    wl_p = min(WINDOW, seq)
    p_conv = v_p.reshape(bsz, seq, d_conv)[:, seq - (CONV_WIDTH - 1):][None]
    new5 = lambda x: jnp.transpose(x.reshape(N_KV_HEADS, HEAD_DIM, db), (2, 0, 1)).reshape(1, db, 1, N_KV_HEADS, HEAD_DIM)
    wl_s = state_win_k.shape[2]
    row4 = lambda x: x.reshape(db, 1, N_KV_HEADS, HEAD_DIM)
    s_win_k = jnp.concatenate([state_win_k[0], row4(kw_s)], axis=1)[:, -wl_s:][None]
    s_win_v = jnp.concatenate([state_win_v[0], row4(vw_s)], axis=1)[:, -wl_s:][None]
    s_conv = jnp.concatenate([state_conv[0], v_s[:, None, :]], axis=1)[:, -(CONV_WIDTH - 1):][None]
    return (y_prompt, y_sample,
            state5(kct_p), state5(vct_p), state5(kst_p), state5(vst_p),
            state5(kwt_p[:, :, seq - wl_p:]), state5(vwt_p[:, :, seq - wl_p:]), p_conv,
            new5(kct_s), new5(vct_s), new5(kst_s), new5(vst_s),
            s_win_k, s_win_v, s_conv)
```

```python
import functools

import numpy as np
import jax
import jax.numpy as jnp
from jax import lax
from jax.experimental import pallas as pl
from jax.experimental.pallas import tpu as pltpu

N_HEADS = 16
N_KV_HEADS = 4
GROUP = N_HEADS // N_KV_HEADS
HEAD_DIM = 64
ROT_DIM = HEAD_DIM // 4
ROPE_THETA = 500000.0
CMP_LEN = 32
CMP_STRIDE = 16
SEL_BLOCK = 64
N_SELECT = 16
WINDOW = 512
Q_TILE = 256
K_TILE = 512
CMP_CHUNK = 512
PAGE_SIZE = 128
CONV_WIDTH = 31
CONV_HALO = 32
CONV_ROWS = 128
SAMPLE_BATCH = 8
EPS = 1e-6
NEG = -1e30
BIG = 1e6
REMOVED = -3e38
LOG2E = 1.4426950408889634
KV_W = N_KV_HEADS * HEAD_DIM
LANES = 128
SUBLANES = 8
SUB_PER_PAGE = PAGE_SIZE // CMP_STRIDE
BLK_PER_TILE = K_TILE // SEL_BLOCK
BLK_ROWS = 16
V_ROWS = HEAD_DIM + 16

MXU_DTYPE = jnp.bfloat16
VMEM_LIMIT = 56 * 1024 * 1024
F32 = jnp.float32


def _mm(a, b):
    return jnp.dot(a.astype(MXU_DTYPE), b.astype(MXU_DTYPE), preferred_element_type=F32)


def _mm_nt(a, b):
    return lax.dot_general(a.astype(MXU_DTYPE), b.astype(MXU_DTYPE), (((1,), (1,)), ((), ())),
                           preferred_element_type=F32)


def _sigmoid(x):
    return 1.0 / (1.0 + jnp.exp(-x))


def _const_spec(shape):
    nd = len(shape)
    return pl.BlockSpec(shape, lambda *_: (0,) * nd, pipeline_mode=pl.Buffered(1))


def _params(*sem):
    return pltpu.CompilerParams(dimension_semantics=sem, vmem_limit_bytes=VMEM_LIMIT)


def _ff_chunk(f):
    best = LANES
    for k in range(1, f // LANES + 1):
        c = k * LANES
        if f % c == 0 and c <= 1408:
            best = c
    return best


def _ffn_kernel(*refs, fc, final):
    if final:
        x_ref, g_ref, wg_ref, wu_ref, wd_ref, gf_ref, o_ref = refs
    else:
        x_ref, g_ref, wg_ref, wu_ref, wd_ref, o_ref = refs
    x = x_ref[...]
    xn = x * lax.rsqrt(jnp.mean(x * x, -1, keepdims=True) + EPS) * g_ref[...]
    xb = xn.astype(MXU_DTYPE)
    acc = None
    for c in range(wg_ref.shape[1] // fc):
        sl = slice(c * fc, (c + 1) * fc)
        gate = _mm(xb, wg_ref[:, sl])
        up = _mm(xb, wu_ref[:, sl])
        d = _mm(gate * _sigmoid(gate) * up, wd_ref[sl, :])
        acc = d if acc is None else acc + d
    h = x + 0.5 * acc
    if final:
        h = h * lax.rsqrt(jnp.mean(h * h, -1, keepdims=True) + EPS) * gf_ref[...]
    o_ref[...] = h


def _ffn(x, g, wg, wu, wd, gfin=None, tm=512):
    r, d = x.shape
    f = wg.shape[1]
    tm = min(tm, r)
    row = pl.BlockSpec((tm, d), lambda i: (i, 0))
    in_specs = [row, _const_spec((1, d)), _const_spec((d, f)), _const_spec((d, f)), _const_spec((f, d))]
    args = [x, g.reshape(1, d), wg, wu, wd]
    if gfin is not None:
        in_specs.append(_const_spec((1, d)))
        args.append(gfin.reshape(1, d))
    return pl.pallas_call(
        functools.partial(_ffn_kernel, fc=_ff_chunk(f), final=gfin is not None),
        grid=(r // tm,), in_specs=in_specs, out_specs=row,
        out_shape=jax.ShapeDtypeStruct((r, d), F32),
        compiler_params=_params("parallel"), name="ffn")(*args)


def _rope_tables(pos):
    half = ROT_DIM // 2
    inv = ROPE_THETA ** (-jnp.arange(half, dtype=F32) / half)
    ang = pos.astype(F32)[:, None] * inv[None, :]
    cos, sin = jnp.cos(ang), jnp.sin(ang)
    t = pos.shape[0]
    rest = HEAD_DIM - ROT_DIM
    c = jnp.concatenate([cos, cos, jnp.ones((t, rest), F32)], 1)
    s1 = jnp.concatenate([jnp.zeros((t, half), F32), sin, jnp.zeros((t, rest), F32)], 1)
    s2 = jnp.concatenate([-sin, jnp.zeros((t, half + rest), F32)], 1)
    rep = LANES // HEAD_DIM
    return jnp.tile(c, (1, rep)), jnp.tile(s1, (1, rep)), jnp.tile(s2, (1, rep))


def _rope(z, c, s1, s2):
    out = []
    for j in range(z.shape[1] // LANES):
        x = z[:, j * LANES:(j + 1) * LANES]
        out.append(x * c + pltpu.roll(x, ROT_DIM // 2, 1) * s1 + pltpu.roll(x, LANES - ROT_DIM // 2, 1) * s2)
    return jnp.concatenate(out, axis=1)


def _head_tiles(z, extra=None):
    low = lax.broadcasted_iota(jnp.int32, (1, LANES), 1) < HEAD_DIM
    tiles = []
    for h in range(N_KV_HEADS):
        pair = z[:, (h // 2) * LANES:(h // 2 + 1) * LANES]
        if h % 2:
            pair = pltpu.roll(pair, HEAD_DIM, 1)
        tiles.append(jnp.where(low, pair, 0.0 if extra is None else extra))
    return tiles


def _inproj_kernel(h_ref, g_ref, w_ref, c_ref, s1_ref, s2_ref, *outs, offs, sample, qscale, nper):
    x = h_ref[...]
    tm = x.shape[0]
    xb = (x * lax.rsqrt(jnp.mean(x * x, -1, keepdims=True) + EPS) * g_ref[...]).astype(MXU_DTYPE)
    c, s1, s2 = c_ref[...], s1_ref[...], s2_ref[...]

    def seg(i):
        return _mm(xb, w_ref[:, offs[i]:offs[i + 1]])

    glu = seg(0)
    dc = glu.shape[1] // 2
    v = glu[:, :dc] * _sigmoid(glu[:, dc:])
    q = _rope(seg(1), c, s1, s2) * qscale
    kc, vc = seg(2), seg(3)
    ks, vs = _rope(seg(4), c, s1, s2), seg(5)
    kw, vw = _rope(seg(6), c, s1, s2), seg(7)
    ng = _sigmoid(seg(8))
    mg = _sigmoid(seg(9))
    if sample:
        (v_ref, q_ref, kct_ref, vct_ref, kst_ref, vst_ref, kwt_ref, vwt_ref,
         ks_ref, vs_ref, kw_ref, vw_ref, ng_ref, mg_ref) = outs
        q_ref[...] = q
        for ref, val in ((ks_ref, ks), (vs_ref, vs), (kw_ref, kw), (vw_ref, vw)):
            ref[...] = val
        ng_ref[...] = ng
    else:
        (v_ref, qt_ref, kct_ref, vct_ref, kst_ref, vst_ref, kwt_ref, vwt_ref,
         ksa_ref, kwa_ref, vstb_ref, vwtb_ref, ngt_ref, mg_ref) = outs
        qt_ref[...] = q.T.astype(MXU_DTYPE)
        pos = (pl.program_id(0) % nper) * tm + lax.broadcasted_iota(jnp.int32, (tm, 1), 0)
        blk_in_tile = lax.shift_right_logical(pos, SEL_BLOCK.bit_length() - 1) & (BLK_PER_TILE - 1)
        lane = lax.broadcasted_iota(jnp.int32, (1, LANES), 1)
        onehot = jnp.where(lane - HEAD_DIM == blk_in_tile, 1.0, 0.0)
        for h, tile in enumerate(_head_tiles(ks, onehot)):
            ksa_ref[h] = tile.astype(MXU_DTYPE)
        for h, tile in enumerate(_head_tiles(kw)):
            kwa_ref[h] = tile.astype(MXU_DTYPE)
        ngt_ref[...] = ng.T[0:3 * N_HEADS, :]
    v_ref[...] = v
    mg_ref[...] = mg
    kct_ref[...] = kc.T
    vct_ref[...] = vc.T
    kst_ref[...] = ks.T
    kwt_ref[...] = kw.T
    vst = vs.T
    vwt = vw.T
    vst_ref[...] = vst
    vwt_ref[...] = vwt
    if not sample:
        ones = jnp.where(lax.broadcasted_iota(jnp.int32, (V_ROWS - HEAD_DIM, tm), 0) == 0, 1.0, 0.0)
        for h in range(N_KV_HEADS):
            rs = slice(h * HEAD_DIM, (h + 1) * HEAD_DIM)
            vstb_ref[h] = jnp.concatenate([vst[rs], ones], axis=0).astype(MXU_DTYPE)
            vwtb_ref[h] = jnp.concatenate([vwt[rs], ones], axis=0).astype(MXU_DTYPE)


def _inproj(h, g, w_pad, offs, tables, nseq, rows_per_seq, sample, qscale, tm=512):
    r, d = h.shape
    tm = min(tm, rows_per_seq)
    nper = rows_per_seq // tm
    nat = lambda w, dt=F32: (pl.BlockSpec((tm, w), lambda i: (i, 0)), jax.ShapeDtypeStruct((r, w), dt))
    tr = lambda w, dt=F32: (pl.BlockSpec((None, w, tm), lambda i: (i // nper, 0, i % nper)),
                            jax.ShapeDtypeStruct((nseq, w, rows_per_seq), dt))
    aug = (pl.BlockSpec((None, N_KV_HEADS, tm, LANES), lambda i: (i // nper, 0, i % nper, 0)),
           jax.ShapeDtypeStruct((nseq, N_KV_HEADS, rows_per_seq, LANES), MXU_DTYPE))
    tab = pl.BlockSpec((tm, LANES), lambda i: (i % nper, 0))
    dc = (offs[1] - offs[0]) // 2
    dq = offs[2] - offs[1]
    planes = [tr(KV_W)] * 6
    if sample:
        outs = [nat(dc), nat(dq)] + planes + [nat(KV_W)] * 4 + [nat(LANES), nat(offs[10] - offs[9])]
    else:
        vplane = (pl.BlockSpec((None, N_KV_HEADS, V_ROWS, tm), lambda i: (i // nper, 0, 0, i % nper)),
                  jax.ShapeDtypeStruct((nseq, N_KV_HEADS, V_ROWS, rows_per_seq), MXU_DTYPE))
        outs = ([nat(dc), tr(dq, MXU_DTYPE)] + planes + [aug, aug, vplane, vplane,
                tr(3 * N_HEADS), nat(offs[10] - offs[9])])
    return pl.pallas_call(
        functools.partial(_inproj_kernel, offs=offs, sample=sample, qscale=qscale, nper=nper),
        grid=(r // tm,),
        in_specs=[pl.BlockSpec((tm, d), lambda i: (i, 0)), _const_spec((1, d)), _const_spec(w_pad.shape),
                  tab, tab, tab],
        out_specs=[o[0] for o in outs], out_shape=[o[1] for o in outs],
        compiler_params=_params("parallel"), name="inproj")(h, g.reshape(1, d), w_pad, *tables)


def _conv_tail(y, b_ref, lg_ref, lb_ref, wo_ref):
    y = y + b_ref[...]
    mu = jnp.mean(y, -1, keepdims=True)
    yc = y - mu
    var = jnp.mean(yc * yc, -1, keepdims=True)
    yn = yc * lax.rsqrt(var + EPS) * lg_ref[...] + lb_ref[...]
    return _mm(yn * _sigmoid(yn), wo_ref[...])


def _conv_kernel(main_ref, halo_ref, w_ref, b_ref, lg_ref, lb_ref, wo_ref, o_ref, ext_ref, sh_ref, y_ref):
    ts, c = main_ref.shape
    ext_ref[0:CONV_HALO, :] = jnp.where(pl.program_id(1) == 0, 0.0, halo_ref[...])
    ext_ref[CONV_HALO:CONV_HALO + ts, :] = main_ref[...]
    lead = CONV_HALO - (CONV_WIDTH - 1)
    n_sh = ts + CONV_HALO - SUBLANES
    for r in range(1, SUBLANES):
        sh_ref[r - 1] = ext_ref[r:r + n_sh, :]

    def tap_rows(k, r0, cs):
        off = lead + k
        r = off % SUBLANES
        base = r0 + off - r
        src = ext_ref if r == 0 else sh_ref.at[r - 1]
        return src[base:base + CONV_ROWS, cs]

    for cb in range(c // LANES):
        cs = slice(cb * LANES, (cb + 1) * LANES)
        taps = [w_ref[k:k + 1, cs] for k in range(CONV_WIDTH)]
        for r0 in range(0, ts, CONV_ROWS):
            acc = tap_rows(0, r0, cs) * taps[0]
            for k in range(1, CONV_WIDTH):
                acc = acc + tap_rows(k, r0, cs) * taps[k]
            y_ref[r0:r0 + CONV_ROWS, cs] = acc
    o_ref[...] = _conv_tail(y_ref[...], b_ref, lg_ref, lb_ref, wo_ref)


def _conv(v, w, b, lg, lb, wo, ts=512):
    bsz, l, c = v.shape
    d = wo.shape[1]
    step = ts // CONV_HALO
    vec = lambda a: a.reshape(1, c)
    return pl.pallas_call(
        _conv_kernel,
        grid=(bsz, l // ts),
        in_specs=[pl.BlockSpec((None, ts, c), lambda bi, i: (bi, i, 0)),
                  pl.BlockSpec((None, CONV_HALO, c), lambda bi, i: (bi, jnp.maximum(i * step - 1, 0), 0)),
                  _const_spec((CONV_WIDTH, c)), _const_spec((1, c)), _const_spec((1, c)), _const_spec((1, c)),
                  _const_spec((c, d))],
        out_specs=pl.BlockSpec((None, ts, d), lambda bi, i: (bi, i, 0)),
        out_shape=jax.ShapeDtypeStruct((bsz, l, d), F32),
        scratch_shapes=[pltpu.VMEM((ts + CONV_HALO, c), F32),
                        pltpu.VMEM((SUBLANES - 1, ts + CONV_HALO - SUBLANES, c), F32), pltpu.VMEM((ts, c), F32)],
        compiler_params=_params("parallel", "parallel"), name="conv")(v, v, w, vec(b), vec(lg), vec(lb), wo)


def _conv_sample_kernel(st_ref, v_ref, w_ref, b_ref, lg_ref, lb_ref, wo_ref, o_ref):
    acc = v_ref[...] * w_ref[CONV_WIDTH - 1:CONV_WIDTH, :]
    for k in range(CONV_WIDTH - 1):
        acc = acc + st_ref[k] * w_ref[k:k + 1, :]
    o_ref[...] = _conv_tail(acc, b_ref, lg_ref, lb_ref, wo_ref)


def _conv_sample(st, v, w, b, lg, lb, wo):
    rows, c = v.shape
    d = wo.shape[1]
    vec = lambda a: a.reshape(1, c)
    args = (st, v, w, vec(b), vec(lg), vec(lb), wo)
    return pl.pallas_call(
        _conv_sample_kernel, grid=(1,),
        in_specs=[_const_spec(a.shape) for a in args],
        out_specs=pl.BlockSpec((rows, d), lambda i: (0, 0)),
        out_shape=jax.ShapeDtypeStruct((rows, d), F32),
        compiler_params=_params("arbitrary"), name="conv_sample")(*args)


def _compress_kernel(pt_ref, x_hbm, w1_ref, pe_ref, w1f_ref, b1_ref, w2_ref, o_ref,
                     buf, tbuf, fsbuf, sem, *, pc, n_pages, paged, planes):
    s = pl.program_id(0)
    ci = pl.program_id(1)
    n_chunks = pl.num_programs(1)
    step = s * n_chunks + ci
    slot = step % 2
    m = pc * SUB_PER_PAGE
    nrow = m + SUB_PER_PAGE
    pairs = N_KV_HEADS // 2

    def page_copy(s_, ci_, slot_, p):
        pg = pt_ref[s_ * n_pages + jnp.minimum(ci_ * pc + p, n_pages - 1)]
        if paged:
            src = x_hbm.at[pg]
        else:
            src = x_hbm.at[s_, :, :, pl.ds(pl.multiple_of(pg * PAGE_SIZE, PAGE_SIZE), PAGE_SIZE)]
        return pltpu.make_async_copy(src, buf.at[slot_, p], sem.at[slot_])

    @pl.when(step == 0)
    def _():
        for p in range(pc + 1):
            page_copy(s, ci, slot, p).start()

    @pl.when(step + 1 < pl.num_programs(0) * n_chunks)
    def _():
        wrap = ci + 1 == n_chunks
        for p in range(pc + 1):
            page_copy(jnp.where(wrap, s + 1, s), jnp.where(wrap, 0, ci + 1), 1 - slot, p).start()

    for p in range(pc + 1):
        page_copy(s, ci, slot, p).wait()

    for p in range(pc + 1):
        for j in range(pairs):
            plane = buf[slot, p, 2 * j:2 * j + 2].reshape(2 * HEAD_DIM, PAGE_SIZE)
            tbuf[j, p * PAGE_SIZE:(p + 1) * PAGE_SIZE, :] = plane.T
    for j in range(pairs):
        acc = None
        for lp in range(CMP_STRIDE // 2):
            lhs = jnp.concatenate([tbuf[j, pl.ds(2 * lp + i, nrow, stride=CMP_STRIDE), :] for i in range(2)], axis=1)
            d = _mm(lhs, w1_ref[lp])
            acc = d if acc is None else acc + d
        fsbuf[j] = acc
    c = _mm(pe_ref[...], w1f_ref[...])[0:1, :] + b1_ref[...]
    nh = c.shape[1]
    parts = []
    for h in range(N_KV_HEADS):
        col = (h % 2) * 2 * nh
        parts.append(fsbuf[h // 2, 0:m, col:col + nh] + fsbuf[h // 2, pl.ds(1, m), col + nh:col + 2 * nh] + c)
    hid = jnp.concatenate(parts, axis=1)
    out = _mm((hid * _sigmoid(hid)).astype(MXU_DTYPE), w2_ref[...])
    if planes:
        o_ref[...] = out.T.astype(MXU_DTYPE)
    else:
        for h, tile in enumerate(_head_tiles(out)):
            o_ref[h] = tile.astype(MXU_DTYPE)


def _compress(x, page_table, weights, pc, paged, planes):
    w1_big, pe8, w1_flat, b1, w2_big = weights
    n_seq, n_pages = page_table.shape
    m = pc * SUB_PER_PAGE
    n_sub = n_pages * SUB_PER_PAGE
    hidden = b1.shape[0]
    if planes:
        out_spec = pl.BlockSpec((None, KV_W, m), lambda s, ci, pt: (s, 0, ci))
        out_shape = jax.ShapeDtypeStruct((n_seq, KV_W, n_sub), MXU_DTYPE)
    else:
        out_spec = pl.BlockSpec((None, N_KV_HEADS, m, LANES), lambda s, ci, pt: (s, 0, ci, 0))
        out_shape = jax.ShapeDtypeStruct((n_seq, N_KV_HEADS, n_sub, LANES), MXU_DTYPE)
    grid_spec = pltpu.PrefetchScalarGridSpec(
        num_scalar_prefetch=1,
        grid=(n_seq, n_pages // pc),
        in_specs=[pl.BlockSpec(memory_space=pl.ANY),
                  _const_spec(w1_big.shape), _const_spec(pe8.shape), _const_spec(w1_flat.shape),
                  _const_spec((1, hidden)), _const_spec(w2_big.shape)],
        out_specs=out_spec,
        scratch_shapes=[pltpu.VMEM((2, pc + 1, N_KV_HEADS, HEAD_DIM, PAGE_SIZE), F32),
                        pltpu.VMEM((N_KV_HEADS // 2, (pc + 1) * PAGE_SIZE, LANES), F32),
                        pltpu.VMEM((N_KV_HEADS // 2, m + SUB_PER_PAGE, w1_big.shape[2]), F32),
                        pltpu.SemaphoreType.DMA((2,))])
    return pl.pallas_call(
        functools.partial(_compress_kernel, pc=pc, n_pages=n_pages, paged=paged, planes=planes),
        grid_spec=grid_spec, out_shape=out_shape,
        compiler_params=_params("arbitrary", "arbitrary"), name="compress")(
            page_table.reshape(-1), x, w1_big, pe8, w1_flat, b1.reshape(1, hidden), w2_big)


def _compress_weights(pe, w1, b1, w2):
    hidden = w1.shape[-1]
    eye = jnp.eye(N_KV_HEADS, dtype=F32)
    both = jnp.concatenate([w1[:CMP_STRIDE], w1[CMP_STRIDE:]], axis=-1)
    both = both.reshape(CMP_STRIDE // 2, 2, HEAD_DIM, 2 * hidden)
    w1_big = jnp.einsum('pidc,hg->pihdgc', both, jnp.eye(2, dtype=F32)).reshape(
        CMP_STRIDE // 2, 4 * HEAD_DIM, 4 * hidden)
    w2_big = jnp.einsum('kd,hg->hkgd', w2, eye).reshape(N_KV_HEADS * hidden, KV_W)
    pe8 = jnp.zeros((SUBLANES, CMP_LEN * HEAD_DIM), F32).at[0].set(pe.reshape(-1))
    return (w1_big.astype(MXU_DTYPE), pe8, w1.reshape(CMP_LEN * HEAD_DIM, hidden).astype(MXU_DTYPE), b1,
            w2_big.astype(MXU_DTYPE))


def _overlap(n_cmp_pad, n_blk_pad):
    ci = np.arange(n_cmp_pad)[:, None] * CMP_STRIDE
    bs = np.arange(n_blk_pad)[None, :] * SEL_BLOCK
    return ((ci < bs + SEL_BLOCK) & (ci + CMP_LEN > bs)).astype(np.float32)


def _attn_kernel(q_ref, g_ref, kc_ref, vct_ref, ov_ref, ks_ref, vst_ref, kw_ref, vwt_ref, o_ref,
                 bias_ref, sc_ref, imp_ref, sa_ref, sb_ref, *, n_cmp, n_blk):
    s0 = pl.program_id(2) * Q_TILE
    cols = GROUP * Q_TILE
    qt = jnp.concatenate([q_ref[g] for g in range(GROUP)], axis=1)
    q0 = jnp.concatenate([qt, jnp.zeros((LANES - HEAD_DIM, cols), qt.dtype)], axis=0)
    tq = s0 + (lax.broadcasted_iota(jnp.int32, (1, cols), 1) & (Q_TILE - 1))
    tq1 = s0 + lax.broadcasted_iota(jnp.int32, (1, Q_TILE), 1)

    ncp = kc_ref.shape[0]
    ch = min(CMP_CHUNK, ncp)
    last_c = lax.shift_right_logical(s0 + (Q_TILE - CMP_LEN), CMP_STRIDE.bit_length() - 1)
    n_cc = jnp.minimum(last_c // ch + 1, ncp // ch)
    ci0 = lax.broadcasted_iota(jnp.int32, (ch, 1), 0)

    def cmp_scores(cc, m):
        r0 = pl.multiple_of(cc * ch, ch)
        c = ci0 + r0
        s = jnp.where((c * CMP_STRIDE + (CMP_LEN - 1) <= tq) & (c < n_cmp), _mm(kc_ref[pl.ds(r0, ch), :], q0), NEG)
        sc_ref[pl.ds(r0, ch), :] = s
        return jnp.maximum(m, jnp.max(s, axis=0, keepdims=True))

    m_c = lax.fori_loop(0, n_cc, cmp_scores, jnp.full((1, cols), NEG, F32))
    imp_ref[...] = jnp.zeros(imp_ref.shape, F32)

    blk_per_chunk = ch * CMP_STRIDE // SEL_BLOCK
    n_rows = min(n_blk, blk_per_chunk + 2 * SUBLANES)

    def cmp_weights(cc, carry):
        l, acc = carry
        r0 = pl.multiple_of(cc * ch, ch)
        p = jnp.exp2(sc_ref[pl.ds(r0, ch), :] - m_c)
        pb = p.astype(MXU_DTYPE)
        b0 = pl.multiple_of(jnp.clip(cc * blk_per_chunk - SUBLANES, 0, n_blk - n_rows), SUBLANES)
        imp_ref[pl.ds(b0, n_rows), :] += _mm(ov_ref[pl.ds(b0, n_rows), pl.ds(r0, ch)], pb)
        return l + jnp.sum(p, axis=0, keepdims=True), acc + _mm(vct_ref[:, pl.ds(r0, ch)], pb)

    l_c, acc_c = lax.fori_loop(0, n_cc, cmp_weights,
                               (jnp.zeros((1, cols), F32), jnp.zeros((HEAD_DIM, cols), F32)))
    inv_c = jnp.where(m_c > 0.5 * NEG, 1.0 / jnp.maximum(l_c, 1e-30), 0.0)
    o_c = acc_c * inv_c
    impn = imp_ref[...] * inv_c
    imp = impn[:, 0:Q_TILE]
    for g in range(1, GROUP):
        imp = imp + impn[:, g * Q_TILE:(g + 1) * Q_TILE]

    wk = WINDOW + Q_TILE
    w0 = pl.multiple_of(jnp.maximum(s0 - WINDOW, 0), Q_TILE)
    sw = _mm(kw_ref[pl.ds(w0, wk), :], q0)
    slabs = []
    for i in range(wk // Q_TILE):
        wpos = w0 + i * Q_TILE + lax.broadcasted_iota(jnp.int32, (Q_TILE, 1), 0)
        ok = wpos <= tq
        if i == 0:
            ok = ok & (wpos > tq - WINDOW)
        slabs.append(jnp.where(ok, sw[i * Q_TILE:(i + 1) * Q_TILE], NEG))
    sw = jnp.concatenate(slabs, axis=0)
    acc_w = _mm(vwt_ref[:, pl.ds(w0, wk)], jnp.exp2(sw - jnp.max(sw, axis=0, keepdims=True)))
    o_w = acc_w[0:HEAD_DIM] / acc_w[HEAD_DIM:HEAD_DIM + 1]

    blk = lax.broadcasted_iota(jnp.int32, (n_blk, 1), 0)
    blk_f = blk.astype(F32)
    cur = lax.shift_right_logical(tq1, SEL_BLOCK.bit_length() - 1)
    forced = (blk == 0) | (blk == cur) | (blk == cur - 1)
    valid = blk * SEL_BLOCK <= tq1
    n_top = min(N_SELECT, n_blk)

    def ranked_selection():
        score = jnp.where(valid, imp + jnp.where(forced, BIG, 0.0), -BIG)
        for _ in range(n_top):
            top = jnp.max(score, axis=0, keepdims=True)
            first = jnp.min(jnp.where(score == top, blk_f, float(n_blk)), axis=0, keepdims=True)
            score = jnp.where(blk_f == first, REMOVED, score)
        return score == REMOVED

    free = valid & jnp.logical_not(forced)
    cand = jnp.where(free, imp, REMOVED)
    n_forced = 1 + (cur >= 1).astype(jnp.int32) + (cur >= 2).astype(jnp.int32)
    n_free = jnp.sum(jnp.where(free, 1.0, 0.0), axis=0, keepdims=True)
    want = jnp.minimum((n_top - n_forced).astype(F32), n_free)
    thr = jnp.full((1, Q_TILE), jnp.inf, F32)
    kth = {}
    for r in range(1, n_top):
        thr = jnp.max(jnp.where(cand < thr, cand, REMOVED), axis=0, keepdims=True)
        kth[r] = thr
    thr = jnp.where(n_forced == 3, kth[max(n_top - 3, 1)], jnp.where(n_forced == 2, kth[max(n_top - 2, 1)],
                                                                      kth[n_top - 1]))
    picked = free & (cand >= thr)
    got = jnp.sum(jnp.where(picked, 1.0, 0.0), axis=0, keepdims=True)
    exact = jnp.max(jnp.abs(got - want)) == 0.0
    as_bias = lambda selected: jnp.where(selected & valid, 0.0, NEG)
    bias_ref[...] = lax.cond(exact, lambda: as_bias(picked | forced), lambda: as_bias(ranked_selection()))

    kpos0 = lax.broadcasted_iota(jnp.int32, (K_TILE, 1), 0)
    q_pad = jnp.zeros((LANES - HEAD_DIM - BLK_ROWS, cols), qt.dtype)
    b_pad = jnp.zeros((BLK_ROWS - BLK_PER_TILE, cols), F32)

    def sel_scores(j, s_ref):
        k0 = pl.multiple_of(j * K_TILE, K_TILE)
        b8 = bias_ref[pl.ds(pl.multiple_of(j * BLK_PER_TILE, BLK_PER_TILE), BLK_PER_TILE), :]
        rows = jnp.concatenate([b8] * GROUP, axis=1)
        if BLK_ROWS > BLK_PER_TILE:
            rows = jnp.concatenate([rows, b_pad], axis=0)
        rows = rows.astype(qt.dtype)
        s_ref[...] = _mm(ks_ref[pl.ds(k0, K_TILE), :], jnp.concatenate([qt, rows, q_pad], axis=0))

    def sel_update(j, s_ref, carry, causal=False):
        v = vst_ref[:, pl.ds(pl.multiple_of(j * K_TILE, K_TILE), K_TILE)]
        out = []
        for g in range(GROUP):
            m, acc = carry[g]
            s = s_ref[:, g * Q_TILE:(g + 1) * Q_TILE]
            if causal:
                s = jnp.where(kpos0 + j * K_TILE <= tq1, s, NEG)
            m_new = jnp.maximum(m, jnp.max(s, axis=0, keepdims=True))
            out.append((m_new, jnp.exp2(m - m_new) * acc + _mm(v, jnp.exp2(s - m_new))))
        return tuple(out)

    def sel_pair(i, carry):
        sel_scores(2 * i + 1, sb_ref)
        carry = sel_update(2 * i, sa_ref, carry)
        sel_scores(2 * i + 2, sa_ref)
        return sel_update(2 * i + 1, sb_ref, carry)

    last = (s0 + Q_TILE - 1) // K_TILE
    n_pairs = last // 2
    sel_scores(0, sa_ref)
    init = tuple((jnp.full((1, Q_TILE), NEG, F32), jnp.zeros((V_ROWS, Q_TILE), F32)) for _ in range(GROUP))
    carry = lax.fori_loop(0, n_pairs, sel_pair, init)
    ta = 2 * n_pairs

    def two_left(c):
        sel_scores(last, sb_ref)
        return sel_update(last, sb_ref, sel_update(ta, sa_ref, c), causal=True)

    carry = lax.cond(ta < last, two_left, lambda c: sel_update(ta, sa_ref, c, causal=True), carry)
    acc_s = jnp.concatenate([acc for _, acc in carry], axis=1)
    o_s = acc_s[0:HEAD_DIM] / acc_s[HEAD_DIM:HEAD_DIM + 1]

    gate = lambda br: jnp.concatenate([g_ref[g * 3 + br:g * 3 + br + 1, :] for g in range(GROUP)], axis=1)
    o = gate(0) * o_c + gate(1) * o_s + gate(2) * o_w
    for g in range(GROUP):
        o_ref[g] = o[:, g * Q_TILE:(g + 1) * Q_TILE]


def _attn_prompt(qt, gt, kc, vct, ks, vst, kw, vwt, n_cmp):
    b, h, grp, hd, s = qt.shape
    ncp = kc.shape[2]
    assert K_TILE % Q_TILE == 0 and WINDOW % Q_TILE == 0 and BLK_PER_TILE <= BLK_ROWS
    n_blk = s // SEL_BLOCK
    nt = s // Q_TILE
    cols = grp * Q_TILE
    ov = jnp.asarray(_overlap(ncp, n_blk).T, MXU_DTYPE)
    one = pl.Buffered(1)
    rows = lambda n: pl.BlockSpec((None, None, n, LANES), lambda bi, hi, t: (bi, hi, 0, 0), pipeline_mode=one)
    plane = lambda n: pl.BlockSpec((None, hd, n), lambda bi, hi, t: (bi, hi, 0), pipeline_mode=one)
    vplane = pl.BlockSpec((None, None, V_ROWS, s), lambda bi, hi, t: (bi, hi, 0, 0), pipeline_mode=one)
    qspec = pl.BlockSpec((None, None, grp, hd, Q_TILE), lambda bi, hi, t: (bi, hi, 0, 0, t))
    return pl.pallas_call(
        functools.partial(_attn_kernel, n_cmp=n_cmp, n_blk=n_blk),
        grid=(b, h, nt),
        in_specs=[qspec, pl.BlockSpec((None, None, 3 * grp, Q_TILE), lambda bi, hi, t: (bi, hi, 0, t)),
                  rows(ncp), plane(ncp), _const_spec(ov.shape), rows(s), vplane, rows(s), vplane],
        out_specs=qspec,
        out_shape=jax.ShapeDtypeStruct((b, h, grp, hd, s), F32),
        scratch_shapes=[pltpu.VMEM((n_blk, Q_TILE), F32), pltpu.VMEM((ncp, cols), F32),
                        pltpu.VMEM((n_blk, cols), F32), pltpu.VMEM((K_TILE, cols), F32),
                        pltpu.VMEM((K_TILE, cols), F32)],
        compiler_params=_params("parallel", "parallel", "arbitrary"), name="attn_prompt")(
            qt, gt, kc, vct, ov, ks, vst, kw, vwt)


def _sample_cmp_kernel(q_ref, kct_ref, vc_ref, ov_ref, oc_ref, idx_ref, *, n_cmp, n_blk, t):
    nb = q_ref.shape[0]
    ncp = kct_ref.shape[2]
    nbp = ov_ref.shape[1]
    ci = lax.broadcasted_iota(jnp.int32, (1, ncp), 1)
    mask = (ci * CMP_STRIDE + (CMP_LEN - 1) <= t) & (ci < n_cmp)
    sums = []
    for b in range(nb):
        for h in range(N_KV_HEADS):
            s = jnp.where(mask, _mm(q_ref[b, h], kct_ref[b, h * HEAD_DIM:(h + 1) * HEAD_DIM, :]), NEG)
            m = jnp.max(s, axis=1, keepdims=True)
            p = jnp.where(mask, jnp.exp(s - m), 0.0)
            p = p / jnp.maximum(jnp.sum(p, axis=1, keepdims=True), 1e-30)
            oc_ref[b, h] = _mm(p, vc_ref[b, h])
            sums.append(jnp.sum(p, axis=0, keepdims=True))
    imp = _mm(jnp.concatenate(sums, axis=0), ov_ref[...])
    blk = lax.broadcasted_iota(jnp.int32, (1, nbp), 1)
    cur = t // SEL_BLOCK
    forced = (blk == 0) | (blk == cur) | (blk == cur - 1)
    score = jnp.where(blk * SEL_BLOCK <= t, imp + jnp.where(forced, BIG, 0.0), -BIG)
    score = jnp.where(blk < n_blk, score, REMOVED)
    blk_f = blk.astype(F32)
    lane = lax.broadcasted_iota(jnp.int32, (1, LANES), 1)
    out = jnp.zeros((nb * N_KV_HEADS, LANES), F32)
    for r in range(min(N_SELECT, n_blk)):
        top = jnp.max(score, axis=1, keepdims=True)
        first = jnp.min(jnp.where(score == top, blk_f, float(nbp)), axis=1, keepdims=True)
        out = jnp.where(lane == r, first, out)
        score = jnp.where(blk_f == first, REMOVED, score)
    idx_ref[...] = out.astype(jnp.int32)


def _sample_cmp(q4, kct, vc, n_cmp, n_blk, t):
    db, _, _, hd = q4.shape
    ncp = kct.shape[2]
    nbp = -(-n_blk // LANES) * LANES
    nb = SAMPLE_BATCH if db % SAMPLE_BATCH == 0 else 1
    ov = jnp.asarray(_overlap(ncp, nbp), MXU_DTYPE)
    per = lambda r, w: pl.BlockSpec((nb, N_KV_HEADS, r, w), lambda b: (b, 0, 0, 0))
    oc, idx = pl.pallas_call(
        functools.partial(_sample_cmp_kernel, n_cmp=n_cmp, n_blk=n_blk, t=t),
        grid=(db // nb,),
        in_specs=[per(GROUP, hd), pl.BlockSpec((nb, KV_W, ncp), lambda b: (b, 0, 0)), per(ncp, LANES),
                  _const_spec(ov.shape)],
        out_specs=[per(GROUP, LANES), pl.BlockSpec((None, nb * N_KV_HEADS, LANES), lambda b: (b, 0, 0))],
        out_shape=[jax.ShapeDtypeStruct((db, N_KV_HEADS, GROUP, LANES), F32),
                   jax.ShapeDtypeStruct((db // nb, nb * N_KV_HEADS, LANES), jnp.int32)],
        compiler_params=_params("parallel"), name="sample_cmp")(q4, kct, vc, ov)
    return oc, idx.reshape(db, N_KV_HEADS, LANES)


def _sample_sel_kernel(idx_ref, pt_ref, q_ref, oc_ref, g_ref, ksn_ref, vsn_ref, kwn_ref, vwn_ref, wk_ref, wv_ref,
                       pk_hbm, pv_hbm, o_ref, kbuf, vbuf, sem, *, n_past_blk, n_pages, nsel, t):
    b = pl.program_id(0)
    sub = PAGE_SIZE // SEL_BLOCK

    slot = b % 2

    def sel_block(b_, h, n):
        return idx_ref[(b_ * N_KV_HEADS + h) * nsel + n]

    def page_copies(b_, slot_, h, n):
        pg = pt_ref[b_ * n_pages + jnp.minimum(sel_block(b_, h, n), n_past_blk - 1) // sub]
        dst = pl.ds(n * PAGE_SIZE, PAGE_SIZE)
        return (pltpu.make_async_copy(pk_hbm.at[pg, h], kbuf.at[slot_, h, :, dst], sem.at[0, slot_]),
                pltpu.make_async_copy(pv_hbm.at[pg, h], vbuf.at[slot_, h, :, dst], sem.at[1, slot_]))

    def all_copies(b_, slot_, go):
        for h in range(N_KV_HEADS):
            for n in range(nsel):
                for cp in page_copies(b_, slot_, h, n):
                    go(cp)

    @pl.when(b == 0)
    def _():
        all_copies(b, slot, lambda cp: cp.start())

    @pl.when(b + 1 < pl.num_programs(0))
    def _():
        all_copies(b + 1, 1 - slot, lambda cp: cp.start())

    all_copies(b, slot, lambda cp: cp.wait())

    lane = lax.broadcasted_iota(jnp.int32, (1, nsel * PAGE_SIZE), 1)
    seg = lax.shift_right_logical(lane, PAGE_SIZE.bit_length() - 1)
    half = lax.shift_right_logical(lane, SEL_BLOCK.bit_length() - 1) & (sub - 1)
    wl = wk_ref.shape[2]
    wpos = t - wl + lax.broadcasted_iota(jnp.int32, (1, wl), 1)
    okw = (wpos > t - WINDOW) & (wpos >= 0)
    for h in range(N_KV_HEADS):
        q = q_ref[h]
        want = jnp.full(lane.shape, -1, jnp.int32)
        for n in range(nsel):
            blk = sel_block(b, h, n)
            want = jnp.where(seg == n, jnp.where(blk < n_past_blk, blk % sub, -1), want)
        ok = half == want
        s = jnp.where(ok, _mm(q, kbuf[slot, h]), NEG)
        s_new = jnp.sum(q * ksn_ref[h:h + 1, :], axis=1, keepdims=True)
        m = jnp.maximum(jnp.max(s, axis=1, keepdims=True), s_new)
        p = jnp.where(ok, jnp.exp(s - m), 0.0)
        p_new = jnp.exp(s_new - m)
        o_s = ((_mm_nt(p, vbuf[slot, h]) + p_new * vsn_ref[h:h + 1, :])
               / (jnp.sum(p, axis=1, keepdims=True) + p_new))
        sw = jnp.where(okw, _mm(q, wk_ref[h]), NEG)
        sw_new = jnp.sum(q * kwn_ref[h:h + 1, :], axis=1, keepdims=True)
        mw = jnp.maximum(jnp.max(sw, axis=1, keepdims=True), sw_new)
        pw = jnp.where(okw, jnp.exp(sw - mw), 0.0)
        pw_new = jnp.exp(sw_new - mw)
        o_w = (_mm_nt(pw, wv_ref[h]) + pw_new * vwn_ref[h:h + 1, :]) / (jnp.sum(pw, axis=1, keepdims=True) + pw_new)
        g = g_ref[h]
        o_ref[h] = g[:, 0:1] * oc_ref[h][:, 0:HEAD_DIM] + g[:, 1:2] * o_s + g[:, 2:3] * o_w


def _sample_sel(idx, page_table, q4, oc, gates, ks_new, vs_new, kw_new, vw_new, win_k, win_v, pool_k, pool_v, t):
    db, _, grp, hd = q4.shape
    nsel = idx.shape[2]
    n_pages = page_table.shape[1]
    n_past_blk = n_pages * (PAGE_SIZE // SEL_BLOCK)
    wl = win_k.shape[3]
    per = lambda *shape: pl.BlockSpec((None,) + shape, lambda b, *_: (b,) + (0,) * len(shape))
    hbm = pl.BlockSpec(memory_space=pl.ANY)
    grid_spec = pltpu.PrefetchScalarGridSpec(
        num_scalar_prefetch=2,
        grid=(db,),
        in_specs=[per(N_KV_HEADS, grp, hd), per(N_KV_HEADS, grp, LANES), per(N_KV_HEADS, grp, LANES)]
                 + [per(N_KV_HEADS, hd)] * 4 + [per(N_KV_HEADS, hd, wl)] * 2 + [hbm, hbm],
        out_specs=per(N_KV_HEADS, grp, hd),
        scratch_shapes=[pltpu.VMEM((2, N_KV_HEADS, hd, nsel * PAGE_SIZE), F32),
                        pltpu.VMEM((2, N_KV_HEADS, hd, nsel * PAGE_SIZE), F32),
                        pltpu.SemaphoreType.DMA((2, 2))])
    return pl.pallas_call(
        functools.partial(_sample_sel_kernel, n_past_blk=n_past_blk, n_pages=n_pages, nsel=nsel, t=t),
        grid_spec=grid_spec,
        out_shape=jax.ShapeDtypeStruct((db, N_KV_HEADS, grp, hd), F32),
        compiler_params=_params("arbitrary"), name="sample_sel")(
            idx.reshape(-1), page_table.reshape(-1), q4, oc, gates, ks_new, vs_new, kw_new, vw_new,
            win_k, win_v, pool_k, pool_v)


def _merge_kernel(h_ref, yc_ref, ya_ref, mg_ref, wa_ref, wo_ref, o_ref, *, planes):
    d = h_ref.shape[1]
    ya = ya_ref[...].T if planes else ya_ref[...]
    m = mg_ref[:, :d] * yc_ref[...] + mg_ref[:, d:] * _mm(ya, wa_ref[...])
    o_ref[...] = h_ref[...] + _mm(m, wo_ref[...])


def _merge(h, y_conv, y_attn, mg, wa, wo, rows_per_seq, planes, tm=512):
    r, d = h.shape
    tm = min(tm, rows_per_seq)
    nper = rows_per_seq // tm
    row = lambda w: pl.BlockSpec((tm, w), lambda i: (i, 0))
    da = wa.shape[0]
    ya_spec = pl.BlockSpec((None, da, tm), lambda i: (i // nper, 0, i % nper)) if planes else row(da)
    return pl.pallas_call(
        functools.partial(_merge_kernel, planes=planes), grid=(r // tm,),
        in_specs=[row(d), row(d), ya_spec, row(2 * d), _const_spec(wa.shape), _const_spec(wo.shape)],
        out_specs=row(d), out_shape=jax.ShapeDtypeStruct((r, d), F32),
        compiler_params=_params("parallel"), name="merge")(h, y_conv, y_attn, mg, wa, wo)


def kernel(x_prompt, x_sample, cache_cmp_k, cache_cmp_v, cache_sel_k, cache_sel_v, state_win_k, state_win_v, state_conv, page_table, norm_ffn1, ffn1_gate, ffn1_up, ffn1_down, norm_mix, w_in, conv_w, conv_b, conv_ln_g, conv_ln_b, w_conv_out, cmp_pe_k, cmp_w1_k, cmp_b1_k, cmp_w2_k, cmp_pe_v, cmp_w1_v, cmp_b1_v, cmp_w2_v, w_attn_out, w_out, norm_ffn2, ffn2_gate, ffn2_up, ffn2_down, norm_final):
    bsz, seq, d = x_prompt.shape
    db, dseq, _ = x_sample.shape
    depth = norm_ffn1.shape[0]
    assert depth == 1 and dseq == 1 and d == N_HEADS * HEAD_DIM
    assert seq % K_TILE == 0 and seq >= WINDOW + Q_TILE
    assert state_conv.shape[2] == CONV_WIDTH - 1
    n_pages = page_table.shape[1]
    past = n_pages * PAGE_SIZE
    d_conv = d // 2
    scale = HEAD_DIM ** -0.5
    cast = lambda w: w.astype(MXU_DTYPE)

    col_sizes = (2 * d_conv, N_HEADS * HEAD_DIM) + (KV_W,) * 6 + (3 * N_HEADS, 2 * d)
    starts = np.concatenate([[0], np.cumsum(col_sizes)])
    offs, pieces, pos = [0], [], 0
    for i, n in enumerate(col_sizes):
        pad = -n % LANES
        pieces.append(w_in[0][:, starts[i]:starts[i + 1]])
        if pad:
            pieces.append(jnp.zeros((d, pad), F32))
        pos += n + pad
        offs.append(pos)
    w_in_pad = cast(jnp.concatenate(pieces, axis=1))
    offs = tuple(offs)
    ffn1 = (norm_ffn1[0], cast(ffn1_gate[0]), cast(ffn1_up[0]), cast(ffn1_down[0]))
    ffn2 = (norm_ffn2[0], cast(ffn2_gate[0]), cast(ffn2_up[0]), cast(ffn2_down[0]))
    wa, wo, wco = cast(w_attn_out[0]), cast(w_out[0]), cast(w_conv_out[0])
    convp = (conv_w[0], conv_b[0], conv_ln_g[0], conv_ln_b[0], wco)
    cmpk = _compress_weights(cmp_pe_k[0], cmp_w1_k[0], cmp_b1_k[0], cmp_w2_k[0])
    cmpv = _compress_weights(cmp_pe_v[0], cmp_w1_v[0], cmp_b1_v[0], cmp_w2_v[0])
    planes4 = lambda x: x.reshape(x.shape[0], N_KV_HEADS, HEAD_DIM, x.shape[2])
    state5 = lambda x: jnp.transpose(planes4(x), (0, 3, 1, 2))[None]
    to_planes = lambda x: jnp.transpose(x, (0, 2, 3, 1))

    rp = bsz * seq
    h_p = _ffn(x_prompt.reshape(rp, d), *ffn1)
    (v_p, qt_p, kct_p, vct_p, kst_p, vst_p, kwt_p, vwt_p, ksa_p, kwa_p, vstb_p, vwtb_p, ngt_p, mg_p) = _inproj(
        h_p, norm_mix[0], w_in_pad, offs, _rope_tables(jnp.arange(seq, dtype=jnp.int32)), bsz, seq,
        sample=False, qscale=scale * LOG2E)
    y_conv_p = _conv(v_p.reshape(bsz, seq, d_conv), *convp).reshape(rp, d)
    pages_p = seq // PAGE_SIZE
    pc_p = min(32, pages_p)
    ident = jnp.tile(jnp.arange(pages_p, dtype=jnp.int32)[None], (bsz, 1))
    kc_rows = _compress(planes4(kct_p), ident, cmpk, pc_p, paged=False, planes=False)
    vc_planes = _compress(planes4(vct_p), ident, cmpv, pc_p, paged=False, planes=True)
    ot = _attn_prompt(qt_p.reshape(bsz, N_KV_HEADS, GROUP, HEAD_DIM, seq),
                      ngt_p.reshape(bsz, N_KV_HEADS, 3 * GROUP, seq),
                      kc_rows, vc_planes, ksa_p, vstb_p, kwa_p, vwtb_p, seq // CMP_STRIDE - 1)
    h2_p = _merge(h_p, y_conv_p, ot.reshape(bsz, d, seq), mg_p, wa, wo, seq, planes=True)
    y_prompt = _ffn(h2_p, *ffn2, gfin=norm_final).reshape(bsz, seq, d)

    t = past
    h_s = _ffn(x_sample.reshape(db, d), *ffn1)
    (v_s, q_s, kct_s, vct_s, kst_s, vst_s, kwt_s, vwt_s, ks_s, vs_s, kw_s, vw_s, ng_s, mg_s) = _inproj(
        h_s, norm_mix[0], w_in_pad, offs, _rope_tables(jnp.full((db,), t, jnp.int32)), 1, db,
        sample=True, qscale=scale)
    y_conv_s = _conv_sample(jnp.transpose(state_conv[0], (1, 0, 2)), v_s, *convp)
    pc_s = min(32, n_pages)
    n_sub_s = (past + dseq) // CMP_STRIDE
    assert n_sub_s == n_pages * SUB_PER_PAGE
    kc_planes_s = _compress(to_planes(cache_cmp_k[0]), page_table, cmpk, pc_s, paged=True, planes=True)
    vc_rows_s = _compress(to_planes(cache_cmp_v[0]), page_table, cmpv, pc_s, paged=True, planes=False)
    n_blk_s = -(-(past + dseq) // SEL_BLOCK)
    q4 = q_s.reshape(db, N_KV_HEADS, GROUP, HEAD_DIM)
    oc_s, idx_s = _sample_cmp(q4, kc_planes_s, vc_rows_s, n_sub_s - 1, n_blk_s, t)
    nsel = min(N_SELECT, n_blk_s)
    gates_s = jnp.pad(ng_s[:, :3 * N_HEADS].reshape(db, N_KV_HEADS, GROUP, 3),
                      ((0, 0), (0, 0), (0, 0), (0, LANES - 3)))
    new3 = lambda x: x.reshape(db, N_KV_HEADS, HEAD_DIM)
    y_attn_s = _sample_sel(idx_s[:, :, :nsel], page_table, q4, oc_s, gates_s,
                           new3(ks_s), new3(vs_s), new3(kw_s), new3(vw_s),
                           to_planes(state_win_k[0]), to_planes(state_win_v[0]),
                           to_planes(cache_sel_k[0]), to_planes(cache_sel_v[0]), t).reshape(db, d)
    h2_s = _merge(h_s, y_conv_s, y_attn_s, mg_s, wa, wo, db, planes=False)
    y_sample = _ffn(h2_s, *ffn2, gfin=norm_final).reshape(db, dseq, d)

    wl_p = min(WINDOW, seq)
    p_conv = v_p.reshape(bsz, seq, d_conv)[:, seq - (CONV_WIDTH - 1):][None]
    new5 = lambda x: jnp.transpose(x.reshape(N_KV_HEADS, HEAD_DIM, db), (2, 0, 1)).reshape(1, db, 1, N_KV_HEADS, HEAD_DIM)
    wl_s = state_win_k.shape[2]
    row4 = lambda x: x.reshape(db, 1, N_KV_HEADS, HEAD_DIM)
    s_win_k = jnp.concatenate([state_win_k[0], row4(kw_s)], axis=1)[:, -wl_s:][None]
    s_win_v = jnp.concatenate([state_win_v[0], row4(vw_s)], axis=1)[:, -wl_s:][None]
    s_conv = jnp.concatenate([state_conv[0], v_s[:, None, :]], axis=1)[:, -(CONV_WIDTH - 1):][None]
    return (y_prompt, y_sample,
            state5(kct_p), state5(vct_p), state5(kst_p), state5(vst_p),
            state5(kwt_p[:, :, seq - wl_p:]), state5(vwt_p[:, :, seq - wl_p:]), p_conv,
            new5(kct_s), new5(vct_s), new5(kst_s), new5(vst_s),
            s_win_k, s_win_v, s_conv)
```

```python
import functools

import numpy as np
import jax
import jax.numpy as jnp
from jax import lax
from jax.experimental import pallas as pl
from jax.experimental.pallas import tpu as pltpu

N_HEADS = 16
N_KV_HEADS = 4
GROUP = N_HEADS // N_KV_HEADS
HEAD_DIM = 64
ROT_DIM = HEAD_DIM // 4
ROPE_THETA = 500000.0
CMP_LEN = 32
CMP_STRIDE = 16
SEL_BLOCK = 64
N_SELECT = 16
WINDOW = 512
Q_TILE = 256
K_TILE = 512
CMP_CHUNK = 512
PAGE_SIZE = 128
CONV_WIDTH = 31
CONV_HALO = 32
CONV_ROWS = 128
SAMPLE_BATCH = 8
EPS = 1e-6
NEG = -1e30
BIG = 1e6
REMOVED = -3e38
LOG2E = 1.4426950408889634
KV_W = N_KV_HEADS * HEAD_DIM
LANES = 128
SUBLANES = 8
SUB_PER_PAGE = PAGE_SIZE // CMP_STRIDE
BLK_PER_TILE = K_TILE // SEL_BLOCK
BLK_ROWS = 16
V_ROWS = HEAD_DIM + 16

MXU_DTYPE = jnp.bfloat16
VMEM_LIMIT = 56 * 1024 * 1024
F32 = jnp.float32


def _mm(a, b):
    return jnp.dot(a.astype(MXU_DTYPE), b.astype(MXU_DTYPE), preferred_element_type=F32)


def _mm_nt(a, b):
    return lax.dot_general(a.astype(MXU_DTYPE), b.astype(MXU_DTYPE), (((1,), (1,)), ((), ())),
                           preferred_element_type=F32)


def _sigmoid(x):
    return 1.0 / (1.0 + jnp.exp(-x))


def _const_spec(shape):
    nd = len(shape)
    return pl.BlockSpec(shape, lambda *_: (0,) * nd, pipeline_mode=pl.Buffered(1))


def _params(*sem):
    return pltpu.CompilerParams(dimension_semantics=sem, vmem_limit_bytes=VMEM_LIMIT)


def _ff_chunk(f):
    best = LANES
    for k in range(1, f // LANES + 1):
        c = k * LANES
        if f % c == 0 and c <= 1408:
            best = c
    return best


def _ffn_kernel(*refs, fc, final):
    if final:
        x_ref, g_ref, wg_ref, wu_ref, wd_ref, gf_ref, o_ref = refs
    else:
        x_ref, g_ref, wg_ref, wu_ref, wd_ref, o_ref = refs
    x = x_ref[...]
    xn = x * lax.rsqrt(jnp.mean(x * x, -1, keepdims=True) + EPS) * g_ref[...]
    xb = xn.astype(MXU_DTYPE)
    acc = None
    for c in range(wg_ref.shape[1] // fc):
        sl = slice(c * fc, (c + 1) * fc)
        gate = _mm(xb, wg_ref[:, sl])
        up = _mm(xb, wu_ref[:, sl])
        d = _mm(gate * _sigmoid(gate) * up, wd_ref[sl, :])
        acc = d if acc is None else acc + d
    h = x + 0.5 * acc
    if final:
        h = h * lax.rsqrt(jnp.mean(h * h, -1, keepdims=True) + EPS) * gf_ref[...]
    o_ref[...] = h


def _ffn(x, g, wg, wu, wd, gfin=None, tm=512):
    r, d = x.shape
    f = wg.shape[1]
    tm = min(tm, r)
    row = pl.BlockSpec((tm, d), lambda i: (i, 0))
    in_specs = [row, _const_spec((1, d)), _const_spec((d, f)), _const_spec((d, f)), _const_spec((f, d))]
    args = [x, g.reshape(1, d), wg, wu, wd]
    if gfin is not None:
        in_specs.append(_const_spec((1, d)))
        args.append(gfin.reshape(1, d))
    return pl.pallas_call(
        functools.partial(_ffn_kernel, fc=_ff_chunk(f), final=gfin is not None),
        grid=(r // tm,), in_specs=in_specs, out_specs=row,
        out_shape=jax.ShapeDtypeStruct((r, d), F32),
        compiler_params=_params("parallel"), name="ffn")(*args)


def _rope_tables(pos):
    half = ROT_DIM // 2
    inv = ROPE_THETA ** (-jnp.arange(half, dtype=F32) / half)
    ang = pos.astype(F32)[:, None] * inv[None, :]
    cos, sin = jnp.cos(ang), jnp.sin(ang)
    t = pos.shape[0]
    rest = HEAD_DIM - ROT_DIM
    c = jnp.concatenate([cos, cos, jnp.ones((t, rest), F32)], 1)
    s1 = jnp.concatenate([jnp.zeros((t, half), F32), sin, jnp.zeros((t, rest), F32)], 1)
    s2 = jnp.concatenate([-sin, jnp.zeros((t, half + rest), F32)], 1)
    rep = LANES // HEAD_DIM
    return jnp.tile(c, (1, rep)), jnp.tile(s1, (1, rep)), jnp.tile(s2, (1, rep))


def _rope(z, c, s1, s2):
    out = []
    for j in range(z.shape[1] // LANES):
        x = z[:, j * LANES:(j + 1) * LANES]
        out.append(x * c + pltpu.roll(x, ROT_DIM // 2, 1) * s1 + pltpu.roll(x, LANES - ROT_DIM // 2, 1) * s2)
    return jnp.concatenate(out, axis=1)


def _head_tiles(z, extra=None):
    low = lax.broadcasted_iota(jnp.int32, (1, LANES), 1) < HEAD_DIM
    tiles = []
    for h in range(N_KV_HEADS):
        pair = z[:, (h // 2) * LANES:(h // 2 + 1) * LANES]
        if h % 2:
            pair = pltpu.roll(pair, HEAD_DIM, 1)
        tiles.append(jnp.where(low, pair, 0.0 if extra is None else extra))
    return tiles


def _inproj_kernel(h_ref, g_ref, w_ref, c_ref, s1_ref, s2_ref, *outs, offs, sample, qscale, nper):
    x = h_ref[...]
    tm = x.shape[0]
    xb = (x * lax.rsqrt(jnp.mean(x * x, -1, keepdims=True) + EPS) * g_ref[...]).astype(MXU_DTYPE)
    c, s1, s2 = c_ref[...], s1_ref[...], s2_ref[...]

    def seg(i):
        return _mm(xb, w_ref[:, offs[i]:offs[i + 1]])

    glu = seg(0)
    dc = glu.shape[1] // 2
    v = glu[:, :dc] * _sigmoid(glu[:, dc:])
    q = _rope(seg(1), c, s1, s2) * qscale
    kc, vc = seg(2), seg(3)
    ks, vs = _rope(seg(4), c, s1, s2), seg(5)
    kw, vw = _rope(seg(6), c, s1, s2), seg(7)
    ng = _sigmoid(seg(8))
    mg = _sigmoid(seg(9))
    if sample:
        (v_ref, q_ref, kct_ref, vct_ref, kst_ref, vst_ref, kwt_ref, vwt_ref,
         ks_ref, vs_ref, kw_ref, vw_ref, ng_ref, mg_ref) = outs
        q_ref[...] = q
        for ref, val in ((ks_ref, ks), (vs_ref, vs), (kw_ref, kw), (vw_ref, vw)):
            ref[...] = val
        ng_ref[...] = ng
    else:
        (v_ref, qt_ref, kct_ref, vct_ref, kst_ref, vst_ref, kwt_ref, vwt_ref,
         ksa_ref, kwa_ref, vstb_ref, vwtb_ref, ngt_ref, mg_ref) = outs
        qt_ref[...] = q.T.astype(MXU_DTYPE)
        pos = (pl.program_id(0) % nper) * tm + lax.broadcasted_iota(jnp.int32, (tm, 1), 0)
        blk_in_tile = lax.shift_right_logical(pos, SEL_BLOCK.bit_length() - 1) & (BLK_PER_TILE - 1)
        lane = lax.broadcasted_iota(jnp.int32, (1, LANES), 1)
        onehot = jnp.where(lane - HEAD_DIM == blk_in_tile, 1.0, 0.0)
        for h, tile in enumerate(_head_tiles(ks, onehot)):
            ksa_ref[h] = tile.astype(MXU_DTYPE)
        for h, tile in enumerate(_head_tiles(kw)):
            kwa_ref[h] = tile.astype(MXU_DTYPE)
        ngt_ref[...] = ng.T[0:3 * N_HEADS, :]
    v_ref[...] = v
    mg_ref[...] = mg
    kct_ref[...] = kc.T
    vct_ref[...] = vc.T
    kst_ref[...] = ks.T
    kwt_ref[...] = kw.T
    vst = vs.T
    vwt = vw.T
    vst_ref[...] = vst
    vwt_ref[...] = vwt
    if not sample:
        ones = jnp.where(lax.broadcasted_iota(jnp.int32, (V_ROWS - HEAD_DIM, tm), 0) == 0, 1.0, 0.0)
        for h in range(N_KV_HEADS):
            rs = slice(h * HEAD_DIM, (h + 1) * HEAD_DIM)
            vstb_ref[h] = jnp.concatenate([vst[rs], ones], axis=0).astype(MXU_DTYPE)
            vwtb_ref[h] = jnp.concatenate([vwt[rs], ones], axis=0).astype(MXU_DTYPE)


def _inproj(h, g, w_pad, offs, tables, nseq, rows_per_seq, sample, qscale, tm=512):
    r, d = h.shape
    tm = min(tm, rows_per_seq)
    nper = rows_per_seq // tm
    nat = lambda w, dt=F32: (pl.BlockSpec((tm, w), lambda i: (i, 0)), jax.ShapeDtypeStruct((r, w), dt))
    tr = lambda w, dt=F32: (pl.BlockSpec((None, w, tm), lambda i: (i // nper, 0, i % nper)),
                            jax.ShapeDtypeStruct((nseq, w, rows_per_seq), dt))
    aug = (pl.BlockSpec((None, N_KV_HEADS, tm, LANES), lambda i: (i // nper, 0, i % nper, 0)),
           jax.ShapeDtypeStruct((nseq, N_KV_HEADS, rows_per_seq, LANES), MXU_DTYPE))
    tab = pl.BlockSpec((tm, LANES), lambda i: (i % nper, 0))
    dc = (offs[1] - offs[0]) // 2
    dq = offs[2] - offs[1]
    planes = [tr(KV_W)] * 6
    if sample:
        outs = [nat(dc), nat(dq)] + planes + [nat(KV_W)] * 4 + [nat(LANES), nat(offs[10] - offs[9])]
    else:
        vplane = (pl.BlockSpec((None, N_KV_HEADS, V_ROWS, tm), lambda i: (i // nper, 0, 0, i % nper)),
                  jax.ShapeDtypeStruct((nseq, N_KV_HEADS, V_ROWS, rows_per_seq), MXU_DTYPE))
        outs = ([nat(dc), tr(dq, MXU_DTYPE)] + planes + [aug, aug, vplane, vplane,
                tr(3 * N_HEADS), nat(offs[10] - offs[9])])
    return pl.pallas_call(
        functools.partial(_inproj_kernel, offs=offs, sample=sample, qscale=qscale, nper=nper),
        grid=(r // tm,),
        in_specs=[pl.BlockSpec((tm, d), lambda i: (i, 0)), _const_spec((1, d)), _const_spec(w_pad.shape),
                  tab, tab, tab],
        out_specs=[o[0] for o in outs], out_shape=[o[1] for o in outs],
        compiler_params=_params("parallel"), name="inproj")(h, g.reshape(1, d), w_pad, *tables)


def _conv_tail(y, b_ref, lg_ref, lb_ref, wo_ref):
    y = y + b_ref[...]
    mu = jnp.mean(y, -1, keepdims=True)
    yc = y - mu
    var = jnp.mean(yc * yc, -1, keepdims=True)
    yn = yc * lax.rsqrt(var + EPS) * lg_ref[...] + lb_ref[...]
    return _mm(yn * _sigmoid(yn), wo_ref[...])


def _conv_kernel(main_ref, halo_ref, w_ref, b_ref, lg_ref, lb_ref, wo_ref, o_ref, ext_ref, sh_ref, y_ref):
    ts, c = main_ref.shape
    ext_ref[0:CONV_HALO, :] = jnp.where(pl.program_id(1) == 0, 0.0, halo_ref[...])
    ext_ref[CONV_HALO:CONV_HALO + ts, :] = main_ref[...]
    lead = CONV_HALO - (CONV_WIDTH - 1)
    n_sh = ts + CONV_HALO - SUBLANES
    for r in range(1, SUBLANES):
        sh_ref[r - 1] = ext_ref[r:r + n_sh, :]

    def tap_rows(k, r0, cs):
        off = lead + k
        r = off % SUBLANES
        base = r0 + off - r
        src = ext_ref if r == 0 else sh_ref.at[r - 1]
        return src[base:base + CONV_ROWS, cs]

    for cb in range(c // LANES):
        cs = slice(cb * LANES, (cb + 1) * LANES)
        taps = [w_ref[k:k + 1, cs] for k in range(CONV_WIDTH)]
        for r0 in range(0, ts, CONV_ROWS):
            acc = tap_rows(0, r0, cs) * taps[0]
            for k in range(1, CONV_WIDTH):
                acc = acc + tap_rows(k, r0, cs) * taps[k]
            y_ref[r0:r0 + CONV_ROWS, cs] = acc
    o_ref[...] = _conv_tail(y_ref[...], b_ref, lg_ref, lb_ref, wo_ref)


def _conv(v, w, b, lg, lb, wo, ts=512):
    bsz, l, c = v.shape
    d = wo.shape[1]
    step = ts // CONV_HALO
    vec = lambda a: a.reshape(1, c)
    return pl.pallas_call(
        _conv_kernel,
        grid=(bsz, l // ts),
        in_specs=[pl.BlockSpec((None, ts, c), lambda bi, i: (bi, i, 0)),
                  pl.BlockSpec((None, CONV_HALO, c), lambda bi, i: (bi, jnp.maximum(i * step - 1, 0), 0)),
                  _const_spec((CONV_WIDTH, c)), _const_spec((1, c)), _const_spec((1, c)), _const_spec((1, c)),
                  _const_spec((c, d))],
        out_specs=pl.BlockSpec((None, ts, d), lambda bi, i: (bi, i, 0)),
        out_shape=jax.ShapeDtypeStruct((bsz, l, d), F32),
        scratch_shapes=[pltpu.VMEM((ts + CONV_HALO, c), F32),
                        pltpu.VMEM((SUBLANES - 1, ts + CONV_HALO - SUBLANES, c), F32), pltpu.VMEM((ts, c), F32)],
        compiler_params=_params("parallel", "parallel"), name="conv")(v, v, w, vec(b), vec(lg), vec(lb), wo)


def _conv_sample_kernel(st_ref, v_ref, w_ref, b_ref, lg_ref, lb_ref, wo_ref, o_ref):
    acc = v_ref[...] * w_ref[CONV_WIDTH - 1:CONV_WIDTH, :]
    for k in range(CONV_WIDTH - 1):
        acc = acc + st_ref[k] * w_ref[k:k + 1, :]
    o_ref[...] = _conv_tail(acc, b_ref, lg_ref, lb_ref, wo_ref)


def _conv_sample(st, v, w, b, lg, lb, wo):
    rows, c = v.shape
    d = wo.shape[1]
    vec = lambda a: a.reshape(1, c)
    args = (st, v, w, vec(b), vec(lg), vec(lb), wo)
    return pl.pallas_call(
        _conv_sample_kernel, grid=(1,),
        in_specs=[_const_spec(a.shape) for a in args],
        out_specs=pl.BlockSpec((rows, d), lambda i: (0, 0)),
        out_shape=jax.ShapeDtypeStruct((rows, d), F32),
        compiler_params=_params("arbitrary"), name="conv_sample")(*args)


def _compress_kernel(pt_ref, x_hbm, w1_ref, pe_ref, w1f_ref, b1_ref, w2_ref, o_ref,
                     buf, tbuf, fsbuf, sem, *, pc, n_pages, paged, planes):
    s = pl.program_id(0)
    ci = pl.program_id(1)
    n_chunks = pl.num_programs(1)
    step = s * n_chunks + ci
    slot = step % 2
    m = pc * SUB_PER_PAGE
    nrow = m + SUB_PER_PAGE
    pairs = N_KV_HEADS // 2

    def page_copy(s_, ci_, slot_, p):
        pg = pt_ref[s_ * n_pages + jnp.minimum(ci_ * pc + p, n_pages - 1)]
        if paged:
            src = x_hbm.at[pg]
        else:
            src = x_hbm.at[s_, :, :, pl.ds(pl.multiple_of(pg * PAGE_SIZE, PAGE_SIZE), PAGE_SIZE)]
        return pltpu.make_async_copy(src, buf.at[slot_, p], sem.at[slot_])

    @pl.when(step == 0)
    def _():
        for p in range(pc + 1):
            page_copy(s, ci, slot, p).start()

    @pl.when(step + 1 < pl.num_programs(0) * n_chunks)
    def _():
        wrap = ci + 1 == n_chunks
        for p in range(pc + 1):
            page_copy(jnp.where(wrap, s + 1, s), jnp.where(wrap, 0, ci + 1), 1 - slot, p).start()

    for p in range(pc + 1):
        page_copy(s, ci, slot, p).wait()

    for p in range(pc + 1):
        for j in range(pairs):
            plane = buf[slot, p, 2 * j:2 * j + 2].reshape(2 * HEAD_DIM, PAGE_SIZE)
            tbuf[j, p * PAGE_SIZE:(p + 1) * PAGE_SIZE, :] = plane.T
    for j in range(pairs):
        acc = None
        for lp in range(CMP_STRIDE // 2):
            lhs = jnp.concatenate([tbuf[j, pl.ds(2 * lp + i, nrow, stride=CMP_STRIDE), :] for i in range(2)], axis=1)
            d = _mm(lhs, w1_ref[lp])
            acc = d if acc is None else acc + d
        fsbuf[j] = acc
    c = _mm(pe_ref[...], w1f_ref[...])[0:1, :] + b1_ref[...]
    nh = c.shape[1]
    parts = []
    for h in range(N_KV_HEADS):
        col = (h % 2) * 2 * nh
        parts.append(fsbuf[h // 2, 0:m, col:col + nh] + fsbuf[h // 2, pl.ds(1, m), col + nh:col + 2 * nh] + c)
    hid = jnp.concatenate(parts, axis=1)
    out = _mm((hid * _sigmoid(hid)).astype(MXU_DTYPE), w2_ref[...])
    if planes:
        o_ref[...] = out.T.astype(MXU_DTYPE)
    else:
        for h, tile in enumerate(_head_tiles(out)):
            o_ref[h] = tile.astype(MXU_DTYPE)


def _compress(x, page_table, weights, pc, paged, planes):
    w1_big, pe8, w1_flat, b1, w2_big = weights
    n_seq, n_pages = page_table.shape
    m = pc * SUB_PER_PAGE
    n_sub = n_pages * SUB_PER_PAGE
    hidden = b1.shape[0]
    if planes:
        out_spec = pl.BlockSpec((None, KV_W, m), lambda s, ci, pt: (s, 0, ci))
        out_shape = jax.ShapeDtypeStruct((n_seq, KV_W, n_sub), MXU_DTYPE)
    else:
        out_spec = pl.BlockSpec((None, N_KV_HEADS, m, LANES), lambda s, ci, pt: (s, 0, ci, 0))
        out_shape = jax.ShapeDtypeStruct((n_seq, N_KV_HEADS, n_sub, LANES), MXU_DTYPE)
    grid_spec = pltpu.PrefetchScalarGridSpec(
        num_scalar_prefetch=1,
        grid=(n_seq, n_pages // pc),
        in_specs=[pl.BlockSpec(memory_space=pl.ANY),
                  _const_spec(w1_big.shape), _const_spec(pe8.shape), _const_spec(w1_flat.shape),
                  _const_spec((1, hidden)), _const_spec(w2_big.shape)],
        out_specs=out_spec,
        scratch_shapes=[pltpu.VMEM((2, pc + 1, N_KV_HEADS, HEAD_DIM, PAGE_SIZE), F32),
                        pltpu.VMEM((N_KV_HEADS // 2, (pc + 1) * PAGE_SIZE, LANES), F32),
                        pltpu.VMEM((N_KV_HEADS // 2, m + SUB_PER_PAGE, w1_big.shape[2]), F32),
                        pltpu.SemaphoreType.DMA((2,))])
    return pl.pallas_call(
        functools.partial(_compress_kernel, pc=pc, n_pages=n_pages, paged=paged, planes=planes),
        grid_spec=grid_spec, out_shape=out_shape,
        compiler_params=_params("arbitrary", "arbitrary"), name="compress")(
            page_table.reshape(-1), x, w1_big, pe8, w1_flat, b1.reshape(1, hidden), w2_big)


def _compress_weights(pe, w1, b1, w2):
    hidden = w1.shape[-1]
    eye = jnp.eye(N_KV_HEADS, dtype=F32)
    both = jnp.concatenate([w1[:CMP_STRIDE], w1[CMP_STRIDE:]], axis=-1)
    both = both.reshape(CMP_STRIDE // 2, 2, HEAD_DIM, 2 * hidden)
    w1_big = jnp.einsum('pidc,hg->pihdgc', both, jnp.eye(2, dtype=F32)).reshape(
        CMP_STRIDE // 2, 4 * HEAD_DIM, 4 * hidden)
    w2_big = jnp.einsum('kd,hg->hkgd', w2, eye).reshape(N_KV_HEADS * hidden, KV_W)
    pe8 = jnp.zeros((SUBLANES, CMP_LEN * HEAD_DIM), F32).at[0].set(pe.reshape(-1))
    return (w1_big.astype(MXU_DTYPE), pe8, w1.reshape(CMP_LEN * HEAD_DIM, hidden).astype(MXU_DTYPE), b1,
            w2_big.astype(MXU_DTYPE))


def _overlap(n_cmp_pad, n_blk_pad):
    ci = np.arange(n_cmp_pad)[:, None] * CMP_STRIDE
    bs = np.arange(n_blk_pad)[None, :] * SEL_BLOCK
    return ((ci < bs + SEL_BLOCK) & (ci + CMP_LEN > bs)).astype(np.float32)


def _attn_kernel(q_ref, g_ref, kc_ref, vct_ref, ov_ref, ks_ref, vst_ref, kw_ref, vwt_ref, o_ref,
                 bias_ref, sc_ref, imp_ref, sa_ref, sb_ref, *, n_cmp, n_blk):
    s0 = pl.program_id(2) * Q_TILE
    cols = GROUP * Q_TILE
    qt = jnp.concatenate([q_ref[g] for g in range(GROUP)], axis=1)
    q0 = jnp.concatenate([qt, jnp.zeros((LANES - HEAD_DIM, cols), qt.dtype)], axis=0)
    tq = s0 + (lax.broadcasted_iota(jnp.int32, (1, cols), 1) & (Q_TILE - 1))
    tq1 = s0 + lax.broadcasted_iota(jnp.int32, (1, Q_TILE), 1)

    ncp = kc_ref.shape[0]
    ch = min(CMP_CHUNK, ncp)
    last_c = lax.shift_right_logical(s0 + (Q_TILE - CMP_LEN), CMP_STRIDE.bit_length() - 1)
    n_cc = jnp.minimum(last_c // ch + 1, ncp // ch)
    ci0 = lax.broadcasted_iota(jnp.int32, (ch, 1), 0)

    def cmp_scores(cc, m):
        r0 = pl.multiple_of(cc * ch, ch)
        c = ci0 + r0
        s = jnp.where((c * CMP_STRIDE + (CMP_LEN - 1) <= tq) & (c < n_cmp), _mm(kc_ref[pl.ds(r0, ch), :], q0), NEG)
        sc_ref[pl.ds(r0, ch), :] = s
        return jnp.maximum(m, jnp.max(s, axis=0, keepdims=True))

    m_c = lax.fori_loop(0, n_cc, cmp_scores, jnp.full((1, cols), NEG, F32))
    imp_ref[...] = jnp.zeros(imp_ref.shape, F32)

    def cmp_weights(cc, carry):
        l, acc = carry
        r0 = pl.multiple_of(cc * ch, ch)
        p = jnp.exp2(sc_ref[pl.ds(r0, ch), :] - m_c)
        pb = p.astype(MXU_DTYPE)
        imp_ref[...] += _mm(ov_ref[:, pl.ds(r0, ch)], pb)
        return l + jnp.sum(p, axis=0, keepdims=True), acc + _mm(vct_ref[:, pl.ds(r0, ch)], pb)

    l_c, acc_c = lax.fori_loop(0, n_cc, cmp_weights,
                               (jnp.zeros((1, cols), F32), jnp.zeros((HEAD_DIM, cols), F32)))
    inv_c = jnp.where(m_c > 0.5 * NEG, 1.0 / jnp.maximum(l_c, 1e-30), 0.0)
    o_c = acc_c * inv_c
    impn = imp_ref[...] * inv_c
    imp = impn[:, 0:Q_TILE]
    for g in range(1, GROUP):
        imp = imp + impn[:, g * Q_TILE:(g + 1) * Q_TILE]

    wk = WINDOW + Q_TILE
    w0 = pl.multiple_of(jnp.maximum(s0 - WINDOW, 0), Q_TILE)
    wpos = w0 + lax.broadcasted_iota(jnp.int32, (wk, 1), 0)
    sw = jnp.where((wpos <= tq) & (wpos > tq - WINDOW), _mm(kw_ref[pl.ds(w0, wk), :], q0), NEG)
    acc_w = _mm(vwt_ref[:, pl.ds(w0, wk)], jnp.exp2(sw - jnp.max(sw, axis=0, keepdims=True)))
    o_w = acc_w[0:HEAD_DIM] / acc_w[HEAD_DIM:HEAD_DIM + 1]

    blk = lax.broadcasted_iota(jnp.int32, (n_blk, 1), 0)
    blk_f = blk.astype(F32)
    cur = lax.shift_right_logical(tq1, SEL_BLOCK.bit_length() - 1)
    forced = (blk == 0) | (blk == cur) | (blk == cur - 1)
    valid = blk * SEL_BLOCK <= tq1
    n_top = min(N_SELECT, n_blk)

    def ranked_selection():
        score = jnp.where(valid, imp + jnp.where(forced, BIG, 0.0), -BIG)
        for _ in range(n_top):
            top = jnp.max(score, axis=0, keepdims=True)
            first = jnp.min(jnp.where(score == top, blk_f, float(n_blk)), axis=0, keepdims=True)
            score = jnp.where(blk_f == first, REMOVED, score)
        return score == REMOVED

    free = valid & jnp.logical_not(forced)
    cand = jnp.where(free, imp, REMOVED)
    n_forced = 1 + (cur >= 1).astype(jnp.int32) + (cur >= 2).astype(jnp.int32)
    n_free = jnp.sum(jnp.where(free, 1.0, 0.0), axis=0, keepdims=True)
    want = jnp.minimum((n_top - n_forced).astype(F32), n_free)
    thr = jnp.full((1, Q_TILE), jnp.inf, F32)
    kth = {}
    for r in range(1, n_top):
        thr = jnp.max(jnp.where(cand < thr, cand, REMOVED), axis=0, keepdims=True)
        kth[r] = thr
    thr = jnp.where(n_forced == 3, kth[max(n_top - 3, 1)], jnp.where(n_forced == 2, kth[max(n_top - 2, 1)],
                                                                      kth[n_top - 1]))
    picked = free & (cand >= thr)
    got = jnp.sum(jnp.where(picked, 1.0, 0.0), axis=0, keepdims=True)
    exact = jnp.max(jnp.abs(got - want)) == 0.0
    as_bias = lambda selected: jnp.where(selected & valid, 0.0, NEG)
    bias_ref[...] = lax.cond(exact, lambda: as_bias(picked | forced), lambda: as_bias(ranked_selection()))

    kpos0 = lax.broadcasted_iota(jnp.int32, (K_TILE, 1), 0)
    q_pad = jnp.zeros((LANES - HEAD_DIM - BLK_ROWS, cols), qt.dtype)
    b_pad = jnp.zeros((BLK_ROWS - BLK_PER_TILE, cols), F32)

    def sel_scores(j, s_ref):
        k0 = pl.multiple_of(j * K_TILE, K_TILE)
        b8 = bias_ref[pl.ds(pl.multiple_of(j * BLK_PER_TILE, BLK_PER_TILE), BLK_PER_TILE), :]
        rows = jnp.concatenate([b8] * GROUP, axis=1)
        if BLK_ROWS > BLK_PER_TILE:
            rows = jnp.concatenate([rows, b_pad], axis=0)
        rows = rows.astype(qt.dtype)
        s_ref[...] = _mm(ks_ref[pl.ds(k0, K_TILE), :], jnp.concatenate([qt, rows, q_pad], axis=0))

    def sel_update(j, s_ref, carry, causal=False):
        v = vst_ref[:, pl.ds(pl.multiple_of(j * K_TILE, K_TILE), K_TILE)]
        out = []
        for g in range(GROUP):
            m, acc = carry[g]
            s = s_ref[:, g * Q_TILE:(g + 1) * Q_TILE]
            if causal:
                s = jnp.where(kpos0 + j * K_TILE <= tq1, s, NEG)
            m_new = jnp.maximum(m, jnp.max(s, axis=0, keepdims=True))
            out.append((m_new, jnp.exp2(m - m_new) * acc + _mm(v, jnp.exp2(s - m_new))))
        return tuple(out)

    def sel_pair(i, carry):
        sel_scores(2 * i + 1, sb_ref)
        carry = sel_update(2 * i, sa_ref, carry)
        sel_scores(2 * i + 2, sa_ref)
        return sel_update(2 * i + 1, sb_ref, carry)

    last = (s0 + Q_TILE - 1) // K_TILE
    n_pairs = last // 2
    sel_scores(0, sa_ref)
    init = tuple((jnp.full((1, Q_TILE), NEG, F32), jnp.zeros((V_ROWS, Q_TILE), F32)) for _ in range(GROUP))
    carry = lax.fori_loop(0, n_pairs, sel_pair, init)
    ta = 2 * n_pairs

    def two_left(c):
        sel_scores(last, sb_ref)
        return sel_update(last, sb_ref, sel_update(ta, sa_ref, c), causal=True)

    carry = lax.cond(ta < last, two_left, lambda c: sel_update(ta, sa_ref, c, causal=True), carry)
    acc_s = jnp.concatenate([acc for _, acc in carry], axis=1)
    o_s = acc_s[0:HEAD_DIM] / acc_s[HEAD_DIM:HEAD_DIM + 1]

    gate = lambda br: jnp.concatenate([g_ref[g * 3 + br:g * 3 + br + 1, :] for g in range(GROUP)], axis=1)
    o = gate(0) * o_c + gate(1) * o_s + gate(2) * o_w
    for g in range(GROUP):
        o_ref[g] = o[:, g * Q_TILE:(g + 1) * Q_TILE]


def _attn_prompt(qt, gt, kc, vct, ks, vst, kw, vwt, n_cmp):
    b, h, grp, hd, s = qt.shape
    ncp = kc.shape[2]
    assert K_TILE % Q_TILE == 0 and BLK_PER_TILE <= BLK_ROWS
    n_blk = s // SEL_BLOCK
    nt = s // Q_TILE
    cols = grp * Q_TILE
    ov = jnp.asarray(_overlap(ncp, n_blk).T, MXU_DTYPE)
    one = pl.Buffered(1)
    rows = lambda n: pl.BlockSpec((None, None, n, LANES), lambda bi, hi, t: (bi, hi, 0, 0), pipeline_mode=one)
    plane = lambda n: pl.BlockSpec((None, hd, n), lambda bi, hi, t: (bi, hi, 0), pipeline_mode=one)
    vplane = pl.BlockSpec((None, None, V_ROWS, s), lambda bi, hi, t: (bi, hi, 0, 0), pipeline_mode=one)
    qspec = pl.BlockSpec((None, None, grp, hd, Q_TILE), lambda bi, hi, t: (bi, hi, 0, 0, t))
    return pl.pallas_call(
        functools.partial(_attn_kernel, n_cmp=n_cmp, n_blk=n_blk),
        grid=(b, h, nt),
        in_specs=[qspec, pl.BlockSpec((None, None, 3 * grp, Q_TILE), lambda bi, hi, t: (bi, hi, 0, t)),
                  rows(ncp), plane(ncp), _const_spec(ov.shape), rows(s), vplane, rows(s), vplane],
        out_specs=qspec,
        out_shape=jax.ShapeDtypeStruct((b, h, grp, hd, s), F32),
        scratch_shapes=[pltpu.VMEM((n_blk, Q_TILE), F32), pltpu.VMEM((ncp, cols), F32),
                        pltpu.VMEM((n_blk, cols), F32), pltpu.VMEM((K_TILE, cols), F32),
                        pltpu.VMEM((K_TILE, cols), F32)],
        compiler_params=_params("parallel", "parallel", "arbitrary"), name="attn_prompt")(
            qt, gt, kc, vct, ov, ks, vst, kw, vwt)


def _sample_cmp_kernel(q_ref, kct_ref, vc_ref, ov_ref, oc_ref, idx_ref, *, n_cmp, n_blk, t):
    nb = q_ref.shape[0]
    ncp = kct_ref.shape[2]
    nbp = ov_ref.shape[1]
    ci = lax.broadcasted_iota(jnp.int32, (1, ncp), 1)
    mask = (ci * CMP_STRIDE + (CMP_LEN - 1) <= t) & (ci < n_cmp)
    pairs = [(b, h) for b in range(nb) for h in range(N_KV_HEADS)]
    s = jnp.concatenate([_mm(q_ref[b, h], kct_ref[b, h * HEAD_DIM:(h + 1) * HEAD_DIM, :]) for b, h in pairs], axis=0)
    s = jnp.where(mask, s, NEG)
    p = jnp.where(mask, jnp.exp(s - jnp.max(s, axis=1, keepdims=True)), 0.0)
    p = p / jnp.maximum(jnp.sum(p, axis=1, keepdims=True), 1e-30)
    sums = []
    for i, (b, h) in enumerate(pairs):
        pg = p[i * GROUP:(i + 1) * GROUP]
        oc_ref[b, h] = _mm(pg, vc_ref[b, h])
        sums.append(jnp.sum(pg, axis=0, keepdims=True))
    imp = _mm(jnp.concatenate(sums, axis=0), ov_ref[...])
    blk = lax.broadcasted_iota(jnp.int32, (1, nbp), 1)
    cur = t // SEL_BLOCK
    forced = (blk == 0) | (blk == cur) | (blk == cur - 1)
    score = jnp.where(blk * SEL_BLOCK <= t, imp + jnp.where(forced, BIG, 0.0), -BIG)
    score = jnp.where(blk < n_blk, score, REMOVED)
    blk_f = blk.astype(F32)
    lane = lax.broadcasted_iota(jnp.int32, (1, LANES), 1)
    out = jnp.zeros((nb * N_KV_HEADS, LANES), F32)
    for r in range(min(N_SELECT, n_blk)):
        top = jnp.max(score, axis=1, keepdims=True)
        first = jnp.min(jnp.where(score == top, blk_f, float(nbp)), axis=1, keepdims=True)
        out = jnp.where(lane == r, first, out)
        score = jnp.where(blk_f == first, REMOVED, score)
    idx_ref[...] = out.astype(jnp.int32)


def _sample_cmp(q4, kct, vc, n_cmp, n_blk, t):
    db, _, _, hd = q4.shape
    ncp = kct.shape[2]
    nbp = -(-n_blk // LANES) * LANES
    nb = SAMPLE_BATCH if db % SAMPLE_BATCH == 0 else 1
    ov = jnp.asarray(_overlap(ncp, nbp), MXU_DTYPE)
    per = lambda r, w: pl.BlockSpec((nb, N_KV_HEADS, r, w), lambda b: (b, 0, 0, 0))
    oc, idx = pl.pallas_call(
        functools.partial(_sample_cmp_kernel, n_cmp=n_cmp, n_blk=n_blk, t=t),
        grid=(db // nb,),
        in_specs=[per(GROUP, hd), pl.BlockSpec((nb, KV_W, ncp), lambda b: (b, 0, 0)), per(ncp, LANES),
                  _const_spec(ov.shape)],
        out_specs=[per(GROUP, LANES), pl.BlockSpec((None, nb * N_KV_HEADS, LANES), lambda b: (b, 0, 0))],
        out_shape=[jax.ShapeDtypeStruct((db, N_KV_HEADS, GROUP, LANES), F32),
                   jax.ShapeDtypeStruct((db // nb, nb * N_KV_HEADS, LANES), jnp.int32)],
        compiler_params=_params("parallel"), name="sample_cmp")(q4, kct, vc, ov)
    return oc, idx.reshape(db, N_KV_HEADS, LANES)


def _sample_sel_kernel(idx_ref, pt_ref, q_ref, oc_ref, g_ref, ksn_ref, vsn_ref, kwn_ref, vwn_ref, wk_ref, wv_ref,
                       pk_hbm, pv_hbm, o_ref, kbuf, vbuf, sem, *, n_past_blk, n_pages, nsel, t):
    b = pl.program_id(0)
    sub = PAGE_SIZE // SEL_BLOCK

    slot = b % 2

    def sel_block(b_, h, n):
        return idx_ref[(b_ * N_KV_HEADS + h) * nsel + n]

    def page_copies(b_, slot_, h, n):
        pg = pt_ref[b_ * n_pages + jnp.minimum(sel_block(b_, h, n), n_past_blk - 1) // sub]
        dst = pl.ds(n * PAGE_SIZE, PAGE_SIZE)
        return (pltpu.make_async_copy(pk_hbm.at[pg, h], kbuf.at[slot_, h, :, dst], sem.at[0, slot_]),
                pltpu.make_async_copy(pv_hbm.at[pg, h], vbuf.at[slot_, h, :, dst], sem.at[1, slot_]))

    def all_copies(b_, slot_, go):
        for h in range(N_KV_HEADS):
            for n in range(nsel):
                for cp in page_copies(b_, slot_, h, n):
                    go(cp)

    @pl.when(b == 0)
    def _():
        all_copies(b, slot, lambda cp: cp.start())

    @pl.when(b + 1 < pl.num_programs(0))
    def _():
        all_copies(b + 1, 1 - slot, lambda cp: cp.start())

    all_copies(b, slot, lambda cp: cp.wait())

    lane = lax.broadcasted_iota(jnp.int32, (1, nsel * PAGE_SIZE), 1)
    seg = lax.shift_right_logical(lane, PAGE_SIZE.bit_length() - 1)
    half = lax.shift_right_logical(lane, SEL_BLOCK.bit_length() - 1) & (sub - 1)
    wl = wk_ref.shape[2]
    wpos = t - wl + lax.broadcasted_iota(jnp.int32, (1, wl), 1)
    okw = (wpos > t - WINDOW) & (wpos >= 0)
    for h in range(N_KV_HEADS):
        q = q_ref[h]
        want = jnp.full(lane.shape, -1, jnp.int32)
        for n in range(nsel):
            blk = sel_block(b, h, n)
            want = jnp.where(seg == n, jnp.where(blk < n_past_blk, blk % sub, -1), want)
        ok = half == want
        s = jnp.where(ok, _mm(q, kbuf[slot, h]), NEG)
        s_new = jnp.sum(q * ksn_ref[h:h + 1, :], axis=1, keepdims=True)
        m = jnp.maximum(jnp.max(s, axis=1, keepdims=True), s_new)
        p = jnp.where(ok, jnp.exp(s - m), 0.0)
        p_new = jnp.exp(s_new - m)
        o_s = ((_mm_nt(p, vbuf[slot, h]) + p_new * vsn_ref[h:h + 1, :])
               / (jnp.sum(p, axis=1, keepdims=True) + p_new))
        sw = jnp.where(okw, _mm(q, wk_ref[h]), NEG)
        sw_new = jnp.sum(q * kwn_ref[h:h + 1, :], axis=1, keepdims=True)
        mw = jnp.maximum(jnp.max(sw, axis=1, keepdims=True), sw_new)
        pw = jnp.where(okw, jnp.exp(sw - mw), 0.0)
        pw_new = jnp.exp(sw_new - mw)
        o_w = (_mm_nt(pw, wv_ref[h]) + pw_new * vwn_ref[h:h + 1, :]) / (jnp.sum(pw, axis=1, keepdims=True) + pw_new)
        g = g_ref[h]
        o_ref[h] = g[:, 0:1] * oc_ref[h][:, 0:HEAD_DIM] + g[:, 1:2] * o_s + g[:, 2:3] * o_w


def _sample_sel(idx, page_table, q4, oc, gates, ks_new, vs_new, kw_new, vw_new, win_k, win_v, pool_k, pool_v, t):
    db, _, grp, hd = q4.shape
    nsel = idx.shape[2]
    n_pages = page_table.shape[1]
    n_past_blk = n_pages * (PAGE_SIZE // SEL_BLOCK)
    wl = win_k.shape[3]
    per = lambda *shape: pl.BlockSpec((None,) + shape, lambda b, *_: (b,) + (0,) * len(shape))
    hbm = pl.BlockSpec(memory_space=pl.ANY)
    grid_spec = pltpu.PrefetchScalarGridSpec(
        num_scalar_prefetch=2,
        grid=(db,),
        in_specs=[per(N_KV_HEADS, grp, hd), per(N_KV_HEADS, grp, LANES), per(N_KV_HEADS, grp, LANES)]
                 + [per(N_KV_HEADS, hd)] * 4 + [per(N_KV_HEADS, hd, wl)] * 2 + [hbm, hbm],
        out_specs=per(N_KV_HEADS, grp, hd),
        scratch_shapes=[pltpu.VMEM((2, N_KV_HEADS, hd, nsel * PAGE_SIZE), F32),
                        pltpu.VMEM((2, N_KV_HEADS, hd, nsel * PAGE_SIZE), F32),
                        pltpu.SemaphoreType.DMA((2, 2))])
    return pl.pallas_call(
        functools.partial(_sample_sel_kernel, n_past_blk=n_past_blk, n_pages=n_pages, nsel=nsel, t=t),
        grid_spec=grid_spec,
        out_shape=jax.ShapeDtypeStruct((db, N_KV_HEADS, grp, hd), F32),
        compiler_params=_params("arbitrary"), name="sample_sel")(
            idx.reshape(-1), page_table.reshape(-1), q4, oc, gates, ks_new, vs_new, kw_new, vw_new,
            win_k, win_v, pool_k, pool_v)


def _merge_kernel(h_ref, yc_ref, ya_ref, mg_ref, wa_ref, wo_ref, o_ref, *, planes):
    d = h_ref.shape[1]
    ya = ya_ref[...].T if planes else ya_ref[...]
    m = mg_ref[:, :d] * yc_ref[...] + mg_ref[:, d:] * _mm(ya, wa_ref[...])
    o_ref[...] = h_ref[...] + _mm(m, wo_ref[...])


def _merge(h, y_conv, y_attn, mg, wa, wo, rows_per_seq, planes, tm=512):
    r, d = h.shape
    tm = min(tm, rows_per_seq)
    nper = rows_per_seq // tm
    row = lambda w: pl.BlockSpec((tm, w), lambda i: (i, 0))
    da = wa.shape[0]
    ya_spec = pl.BlockSpec((None, da, tm), lambda i: (i // nper, 0, i % nper)) if planes else row(da)
    return pl.pallas_call(
        functools.partial(_merge_kernel, planes=planes), grid=(r // tm,),
        in_specs=[row(d), row(d), ya_spec, row(2 * d), _const_spec(wa.shape), _const_spec(wo.shape)],
        out_specs=row(d), out_shape=jax.ShapeDtypeStruct((r, d), F32),
        compiler_params=_params("parallel"), name="merge")(h, y_conv, y_attn, mg, wa, wo)


def kernel(x_prompt, x_sample, cache_cmp_k, cache_cmp_v, cache_sel_k, cache_sel_v, state_win_k, state_win_v, state_conv, page_table, norm_ffn1, ffn1_gate, ffn1_up, ffn1_down, norm_mix, w_in, conv_w, conv_b, conv_ln_g, conv_ln_b, w_conv_out, cmp_pe_k, cmp_w1_k, cmp_b1_k, cmp_w2_k, cmp_pe_v, cmp_w1_v, cmp_b1_v, cmp_w2_v, w_attn_out, w_out, norm_ffn2, ffn2_gate, ffn2_up, ffn2_down, norm_final):
    bsz, seq, d = x_prompt.shape
    db, dseq, _ = x_sample.shape
    depth = norm_ffn1.shape[0]
    assert depth == 1 and dseq == 1 and d == N_HEADS * HEAD_DIM
    assert seq % K_TILE == 0 and seq >= WINDOW + Q_TILE
    assert state_conv.shape[2] == CONV_WIDTH - 1
    n_pages = page_table.shape[1]
    past = n_pages * PAGE_SIZE
    d_conv = d // 2
    scale = HEAD_DIM ** -0.5
    cast = lambda w: w.astype(MXU_DTYPE)

    col_sizes = (2 * d_conv, N_HEADS * HEAD_DIM) + (KV_W,) * 6 + (3 * N_HEADS, 2 * d)
    starts = np.concatenate([[0], np.cumsum(col_sizes)])
    offs, pieces, pos = [0], [], 0
    for i, n in enumerate(col_sizes):
        pad = -n % LANES
        pieces.append(w_in[0][:, starts[i]:starts[i + 1]])
        if pad:
            pieces.append(jnp.zeros((d, pad), F32))
        pos += n + pad
        offs.append(pos)
    w_in_pad = cast(jnp.concatenate(pieces, axis=1))
    offs = tuple(offs)
    ffn1 = (norm_ffn1[0], cast(ffn1_gate[0]), cast(ffn1_up[0]), cast(ffn1_down[0]))
    ffn2 = (norm_ffn2[0], cast(ffn2_gate[0]), cast(ffn2_up[0]), cast(ffn2_down[0]))
    wa, wo, wco = cast(w_attn_out[0]), cast(w_out[0]), cast(w_conv_out[0])
    convp = (conv_w[0], conv_b[0], conv_ln_g[0], conv_ln_b[0], wco)
    cmpk = _compress_weights(cmp_pe_k[0], cmp_w1_k[0], cmp_b1_k[0], cmp_w2_k[0])
    cmpv = _compress_weights(cmp_pe_v[0], cmp_w1_v[0], cmp_b1_v[0], cmp_w2_v[0])
    planes4 = lambda x: x.reshape(x.shape[0], N_KV_HEADS, HEAD_DIM, x.shape[2])
    state5 = lambda x: jnp.transpose(planes4(x), (0, 3, 1, 2))[None]
    to_planes = lambda x: jnp.transpose(x, (0, 2, 3, 1))

    rp = bsz * seq
    h_p = _ffn(x_prompt.reshape(rp, d), *ffn1)
    (v_p, qt_p, kct_p, vct_p, kst_p, vst_p, kwt_p, vwt_p, ksa_p, kwa_p, vstb_p, vwtb_p, ngt_p, mg_p) = _inproj(
        h_p, norm_mix[0], w_in_pad, offs, _rope_tables(jnp.arange(seq, dtype=jnp.int32)), bsz, seq,
        sample=False, qscale=scale * LOG2E)
    y_conv_p = _conv(v_p.reshape(bsz, seq, d_conv), *convp).reshape(rp, d)
    pages_p = seq // PAGE_SIZE
    pc_p = min(32, pages_p)
    ident = jnp.tile(jnp.arange(pages_p, dtype=jnp.int32)[None], (bsz, 1))
    kc_rows = _compress(planes4(kct_p), ident, cmpk, pc_p, paged=False, planes=False)
    vc_planes = _compress(planes4(vct_p), ident, cmpv, pc_p, paged=False, planes=True)
    ot = _attn_prompt(qt_p.reshape(bsz, N_KV_HEADS, GROUP, HEAD_DIM, seq),
                      ngt_p.reshape(bsz, N_KV_HEADS, 3 * GROUP, seq),
                      kc_rows, vc_planes, ksa_p, vstb_p, kwa_p, vwtb_p, seq // CMP_STRIDE - 1)
    h2_p = _merge(h_p, y_conv_p, ot.reshape(bsz, d, seq), mg_p, wa, wo, seq, planes=True)
    y_prompt = _ffn(h2_p, *ffn2, gfin=norm_final).reshape(bsz, seq, d)

    t = past
    h_s = _ffn(x_sample.reshape(db, d), *ffn1)
    (v_s, q_s, kct_s, vct_s, kst_s, vst_s, kwt_s, vwt_s, ks_s, vs_s, kw_s, vw_s, ng_s, mg_s) = _inproj(
        h_s, norm_mix[0], w_in_pad, offs, _rope_tables(jnp.full((db,), t, jnp.int32)), 1, db,
        sample=True, qscale=scale)
    y_conv_s = _conv_sample(jnp.transpose(state_conv[0], (1, 0, 2)), v_s, *convp)
    pc_s = min(32, n_pages)
    n_sub_s = (past + dseq) // CMP_STRIDE
    assert n_sub_s == n_pages * SUB_PER_PAGE
    kc_planes_s = _compress(to_planes(cache_cmp_k[0]), page_table, cmpk, pc_s, paged=True, planes=True)
    vc_rows_s = _compress(to_planes(cache_cmp_v[0]), page_table, cmpv, pc_s, paged=True, planes=False)
    n_blk_s = -(-(past + dseq) // SEL_BLOCK)
    q4 = q_s.reshape(db, N_KV_HEADS, GROUP, HEAD_DIM)
    oc_s, idx_s = _sample_cmp(q4, kc_planes_s, vc_rows_s, n_sub_s - 1, n_blk_s, t)
    nsel = min(N_SELECT, n_blk_s)
    gates_s = jnp.pad(ng_s[:, :3 * N_HEADS].reshape(db, N_KV_HEADS, GROUP, 3),
                      ((0, 0), (0, 0), (0, 0), (0, LANES - 3)))
    new3 = lambda x: x.reshape(db, N_KV_HEADS, HEAD_DIM)
    y_attn_s = _sample_sel(idx_s[:, :, :nsel], page_table, q4, oc_s, gates_s,
                           new3(ks_s), new3(vs_s), new3(kw_s), new3(vw_s),
                           to_planes(state_win_k[0]), to_planes(state_win_v[0]),
                           to_planes(cache_sel_k[0]), to_planes(cache_sel_v[0]), t).reshape(db, d)
    h2_s = _merge(h_s, y_conv_s, y_attn_s, mg_s, wa, wo, db, planes=False)
    y_sample = _ffn(h2_s, *ffn2, gfin=norm_final).reshape(db, dseq, d)

    wl_p = min(WINDOW, seq)
    p_conv = v_p.reshape(bsz, seq, d_conv)[:, seq - (CONV_WIDTH - 1):][None]
    new5 = lambda x: jnp.transpose(x.reshape(N_KV_HEADS, HEAD_DIM, db), (2, 0, 1)).reshape(1, db, 1, N_KV_HEADS, HEAD_DIM)
    wl_s = state_win_k.shape[2]
    row4 = lambda x: x.reshape(db, 1, N_KV_HEADS, HEAD_DIM)
    s_win_k = jnp.concatenate([state_win_k[0], row4(kw_s)], axis=1)[:, -wl_s:][None]
    s_win_v = jnp.concatenate([state_win_v[0], row4(vw_s)], axis=1)[:, -wl_s:][None]
    s_conv = jnp.concatenate([state_conv[0], v_s[:, None, :]], axis=1)[:, -(CONV_WIDTH - 1):][None]
    return (y_prompt, y_sample,
            state5(kct_p), state5(vct_p), state5(kst_p), state5(vst_p),
            state5(kwt_p[:, :, seq - wl_p:]), state5(vwt_p[:, :, seq - wl_p:]), p_conv,
            new5(kct_s), new5(vct_s), new5(kst_s), new5(vst_s),
            s_win_k, s_win_v, s_conv)
```
